```python
import jax, jax.numpy as jnp
from jax import lax
import numpy as np

D_MODEL = 1024
BATCH = 2
SEQ = 8192
DEPTH = 1
DEC_BATCH = 32
DEC_SEQ = 64
PAST_LEN = 1024

CHUNK = 64
N_HEADS = 8
N_KV_HEADS = 2
HEAD_DIM = 64
GROUP = N_HEADS // N_KV_HEADS
IDX_HEADS = 8
IDX_DIM = 64
TOPK_MAX = 256
Q_BLOCK = 128
CONV_CH = 512
CONV_WIDTH = 31
D_FF = 2816
ROPE_THETA = 10000.0
EPS = 1e-6

Q_COLS = N_HEADS * HEAD_DIM
KV_COLS = N_KV_HEADS * HEAD_DIM
IDXQ_COLS = IDX_HEADS * IDX_DIM
GATE_COLS = 2 * D_MODEL
SPLIT_SIZES = (Q_COLS, KV_COLS, KV_COLS, IDXQ_COLS, IDX_DIM, IDX_HEADS, 2 * CONV_CH, GATE_COLS)
IN_COLS = sum(SPLIT_SIZES)

kernel_name = "dsa_conformer_gated_streaming_step"


def _rms(x, g):
    x32 = x.astype(jnp.float32)
    y = x32 * lax.rsqrt(jnp.mean(x32 * x32, axis=-1, keepdims=True) + EPS)
    return (y * g.astype(jnp.float32)).astype(x.dtype)


def _layernorm(x, g, b):
    x32 = x.astype(jnp.float32)
    mu = jnp.mean(x32, axis=-1, keepdims=True)
    var = jnp.mean(jnp.square(x32 - mu), axis=-1, keepdims=True)
    y = (x32 - mu) * lax.rsqrt(var + EPS)
    return (y * g.astype(jnp.float32) + b.astype(jnp.float32)).astype(x.dtype)


def _rope(x, pos):
    d = x.shape[-1]
    inv = ROPE_THETA ** (-jnp.arange(0, d, 2, dtype=jnp.float32) / d)
    ang = pos.astype(jnp.float32)[:, None] * inv[None, :]
    cos = jnp.cos(ang)[None, :, None, :]
    sin = jnp.sin(ang)[None, :, None, :]
    x32 = x.astype(jnp.float32)
    x1, x2 = x32[..., : d // 2], x32[..., d // 2:]
    out = jnp.concatenate([x1 * cos - x2 * sin, x2 * cos + x1 * sin], axis=-1)
    return out.astype(x.dtype)


def _swiglu(x, g, w_in, w_out):
    a, b = jnp.split(_rms(x, g) @ w_in, 2, axis=-1)
    return (jax.nn.silu(a) * b) @ w_out


def _dsa_attention(q, k, v, q_idx, k_idx, w_idx, q_pos):
    B, T = q.shape[0], q.shape[1]
    L = k.shape[1]
    top = min(TOPK_MAX, L // 4)
    qb = min(Q_BLOCK, T)
    nb = T // qb
    key_pos = jnp.arange(L, dtype=jnp.int32)

    def blocks(a):
        return jnp.moveaxis(a.reshape((B, nb, qb) + a.shape[2:]), 1, 0)

    def one_block(args):
        qq, qi, wi, pos = args
        limit = (pos // CHUNK + 1) * CHUNK
        admissible = key_pos[None, :] < limit[:, None]
        s = jnp.einsum('bqhd,bsd->bqhs', qi, k_idx).astype(jnp.float32)
        score = jnp.einsum('bqh,bqhs->bqs', wi.astype(jnp.float32), jax.nn.relu(s))
        score = jnp.where(admissible[None], score, -jnp.inf)
        _, sel = lax.top_k(score, top)
        valid = sel < limit[None, :, None]
        kg = jax.vmap(lambda kb, ib: kb[ib])(k, sel)
        vg = jax.vmap(lambda vb, ib: vb[ib])(v, sel)
        qg = qq.reshape(B, qb, N_KV_HEADS, GROUP, HEAD_DIM)
        logits = jnp.einsum('bqcgd,bqncd->bqcgn', qg, kg).astype(jnp.float32) * (HEAD_DIM ** -0.5)
        logits = jnp.where(valid[:, :, None, None, :], logits, -jnp.inf)
        p = jax.nn.softmax(logits, axis=-1).astype(vg.dtype)
        o = jnp.einsum('bqcgn,bqncd->bqcgd', p, vg)
        return o.reshape(B, qb, N_HEADS * HEAD_DIM)

    out = lax.map(one_block, (blocks(q), blocks(q_idx), blocks(w_idx), q_pos.reshape(nb, qb)))
    return jnp.moveaxis(out, 0, 1).reshape(B, T, N_HEADS * HEAD_DIM)


def _layer(x, past_k, past_v, past_idx_k, past_conv,
           ffn1_norm, ffn1_w_in, ffn1_w_out, mix_norm, w_in, b_gate,
           conv_w, conv_b, conv_ln_g, conv_ln_b, conv_w_out, attn_w_out, w_out,
           ffn2_norm, ffn2_w_in, ffn2_w_out):
    B, T, _ = x.shape
    P = past_k.shape[1]
    pos = P + jnp.arange(T, dtype=jnp.int32)

    h = x + 0.5 * _swiglu(x, ffn1_norm, ffn1_w_in, ffn1_w_out)
    u = _rms(h, mix_norm)
    z = u @ w_in
    points = np.cumsum(SPLIT_SIZES)[:-1].tolist()
    q, k, v, qi, ki, wi, conv_in, gate = jnp.split(z, points, axis=-1)

    q = _rope(q.reshape(B, T, N_HEADS, HEAD_DIM), pos)
    k = _rope(k.reshape(B, T, N_KV_HEADS, HEAD_DIM), pos)
    v = v.reshape(B, T, N_KV_HEADS, HEAD_DIM)
    qi = _rope(qi.reshape(B, T, IDX_HEADS, IDX_DIM), pos)
    ki = _rope(ki.reshape(B, T, 1, IDX_DIM), pos)[:, :, 0]
    wi = wi * (IDX_HEADS ** -0.5)
    k_all = jnp.concatenate([past_k, k], axis=1)
    v_all = jnp.concatenate([past_v, v], axis=1)
    ki_all = jnp.concatenate([past_idx_k, ki], axis=1)
    attn = _dsa_attention(q, k_all, v_all, qi, ki_all, wi, pos)

    ca, cb = jnp.split(conv_in, 2, axis=-1)
    c = ca * jax.nn.sigmoid(cb)
    c_pad = jnp.concatenate([past_conv, c], axis=1)
    dc = lax.conv_general_dilated(c_pad, conv_w[:, None, :], window_strides=(1,), padding='VALID',
                                  dimension_numbers=('NWC', 'WIO', 'NWC'),
                                  feature_group_count=CONV_CH) + conv_b
    conv_out = jax.nn.silu(_layernorm(dc, conv_ln_g, conv_ln_b)) @ conv_w_out

    g_a, g_c = jnp.split(jax.nn.sigmoid(gate + b_gate), 2, axis=-1)
    merged = g_a * (attn @ attn_w_out) + g_c * conv_out
    h = h + merged @ w_out
    h = h + 0.5 * _swiglu(h, ffn2_norm, ffn2_w_in, ffn2_w_out)
    return h, k, v, ki, c_pad[:, -(CONV_WIDTH - 1):]


def setup_inputs(seed: int = 0) -> dict:
    key = jax.random.key(seed)
    ks = jax.random.split(key, 32)
    f32 = jnp.float32

    def nrm(i, shape, scale):
        return jax.random.normal(ks[i], shape, f32) * scale

    def gain(i, shape):
        return 1.0 + 0.02 * jax.random.normal(ks[i], shape, f32)

    return {
        "x_prompt": nrm(0, (BATCH, SEQ, D_MODEL), 1.0),
        "x_sample": nrm(1, (DEC_BATCH, DEC_SEQ, D_MODEL), 1.0),
        "cache_k": nrm(2, (DEPTH, DEC_BATCH, PAST_LEN, N_KV_HEADS, HEAD_DIM), 1.0),
        "cache_v": nrm(3, (DEPTH, DEC_BATCH, PAST_LEN, N_KV_HEADS, HEAD_DIM), 1.0),
        "cache_idx_k": nrm(4, (DEPTH, DEC_BATCH, PAST_LEN, IDX_DIM), 1.0),
        "state_conv": nrm(5, (DEPTH, DEC_BATCH, CONV_WIDTH - 1, CONV_CH), 0.5),
        "ffn1_norm": gain(6, (DEPTH, D_MODEL)),
        "ffn1_w_in": nrm(7, (DEPTH, D_MODEL, 2 * D_FF), D_MODEL ** -0.5),
        "ffn1_w_out": nrm(8, (DEPTH, D_FF, D_MODEL), D_FF ** -0.5),
        "mix_norm": gain(9, (DEPTH, D_MODEL)),
        "w_in": nrm(10, (DEPTH, D_MODEL, IN_COLS), D_MODEL ** -0.5),
        "b_gate": nrm(11, (DEPTH, GATE_COLS), 0.01),
        "conv_w": nrm(12, (DEPTH, CONV_WIDTH, CONV_CH), CONV_WIDTH ** -0.5),
        "conv_b": nrm(13, (DEPTH, CONV_CH), 0.01),
        "conv_ln_g": gain(14, (DEPTH, CONV_CH)),
        "conv_ln_b": nrm(15, (DEPTH, CONV_CH), 0.01),
        "conv_w_out": nrm(16, (DEPTH, CONV_CH, D_MODEL), CONV_CH ** -0.5),
        "attn_w_out": nrm(17, (DEPTH, Q_COLS, D_MODEL), Q_COLS ** -0.5),
        "w_out": nrm(18, (DEPTH, D_MODEL, D_MODEL), D_MODEL ** -0.5),
        "ffn2_norm": gain(19, (DEPTH, D_MODEL)),
        "ffn2_w_in": nrm(20, (DEPTH, D_MODEL, 2 * D_FF), D_MODEL ** -0.5),
        "ffn2_w_out": nrm(21, (DEPTH, D_FF, D_MODEL), D_FF ** -0.5),
        "final_norm": gain(22, (D_MODEL,)),
    }


def reference(x_prompt, x_sample, cache_k, cache_v, cache_idx_k, state_conv,
              ffn1_norm, ffn1_w_in, ffn1_w_out, mix_norm, w_in, b_gate,
              conv_w, conv_b, conv_ln_g, conv_ln_b, conv_w_out, attn_w_out, w_out,
              ffn2_norm, ffn2_w_in, ffn2_w_out, final_norm):
    hp, hs = x_prompt, x_sample
    Bp = x_prompt.shape[0]
    dt = x_prompt.dtype
    kp_l, vp_l, ip_l, cp_l = [], [], [], []
    ks_l, vs_l, is_l, cs_l = [], [], [], []
    for l in range(DEPTH):
        w = (ffn1_norm[l], ffn1_w_in[l], ffn1_w_out[l], mix_norm[l], w_in[l], b_gate[l],
             conv_w[l], conv_b[l], conv_ln_g[l], conv_ln_b[l], conv_w_out[l], attn_w_out[l], w_out[l],
             ffn2_norm[l], ffn2_w_in[l], ffn2_w_out[l])
        hp, kp, vp, ip, cp = _layer(
            hp,
            jnp.zeros((Bp, 0, N_KV_HEADS, HEAD_DIM), dt),
            jnp.zeros((Bp, 0, N_KV_HEADS, HEAD_DIM), dt),
            jnp.zeros((Bp, 0, IDX_DIM), dt),
            jnp.zeros((Bp, CONV_WIDTH - 1, CONV_CH), dt),
            *w)
        hs, ks_, vs_, is_, cs_ = _layer(hs, cache_k[l], cache_v[l], cache_idx_k[l], state_conv[l], *w)
        kp_l.append(kp); vp_l.append(vp); ip_l.append(ip); cp_l.append(cp)
        ks_l.append(ks_); vs_l.append(vs_); is_l.append(is_); cs_l.append(cs_)
    y_prompt = _rms(hp, final_norm)
    y_sample = _rms(hs, final_norm)
    return (y_prompt, y_sample,
            jnp.stack(kp_l), jnp.stack(vp_l), jnp.stack(ip_l), jnp.stack(cp_l),
            jnp.stack(ks_l), jnp.stack(vs_l), jnp.stack(is_l), jnp.stack(cs_l))
```

```python
import functools

import jax
import jax.numpy as jnp
from jax import lax
from jax.experimental import pallas as pl
from jax.experimental.pallas import tpu as pltpu

D_MODEL = 1024
CHUNK = 64
N_HEADS = 8
N_KV_HEADS = 2
HEAD_DIM = 64
GROUP = N_HEADS // N_KV_HEADS
IDX_HEADS = 8
IDX_DIM = 64
TOPK_MAX = 256
CONV_CH = 512
CONV_WIDTH = 31
D_FF = 2816
ROPE_THETA = 10000.0
EPS = 1e-6

Q_COLS = N_HEADS * HEAD_DIM
KV_COLS = N_KV_HEADS * HEAD_DIM
IDXQ_COLS = IDX_HEADS * IDX_DIM

LANES = 128
_MXU_DTYPE = jnp.bfloat16
_VMEM_LIMIT = 56 * 1024 * 1024
_FF_CHUNK = 256
_HALO = 32
_NEG = -1e30

_C_Q = 0
_C_QI = _C_Q + Q_COLS
_C_K = _C_QI + IDXQ_COLS
_C_KW = _C_K + KV_COLS
_C_V = _C_KW + LANES
_C_CA = _C_V + KV_COLS
_C_CB = _C_CA + CONV_CH
_C_GA = _C_CB + CONV_CH
_C_GC = _C_GA + D_MODEL
_C_END = _C_GC + D_MODEL

_INT_MIN = -2147483648
_KEY_NEG_FLT_MAX = -2139095040


def _params(sem):
    return pltpu.CompilerParams(dimension_semantics=sem, vmem_limit_bytes=_VMEM_LIMIT)


def _resident(shape):
    nd = len(shape)
    return pl.BlockSpec(shape, lambda *_: (0,) * nd, pipeline_mode=pl.Buffered(1))


def _rms(x, g):
    return x * lax.rsqrt(jnp.mean(x * x, axis=-1, keepdims=True) + EPS) * g


def _dot(a, b):
    return jnp.dot(a, b, preferred_element_type=jnp.float32)


def _dot_nt(a, b):
    return lax.dot_general(a, b, (((1,), (1,)), ((), ())), preferred_element_type=jnp.float32)


def _ffn_kernel(*refs, final_norm):
    if final_norm:
        x_ref, g_ref, wa_ref, wb_ref, wo_ref, gf_ref, o_ref = refs
    else:
        x_ref, g_ref, wa_ref, wb_ref, wo_ref, o_ref = refs
    x = x_ref[...]
    u = _rms(x, g_ref[...]).astype(wa_ref.dtype)
    acc = jnp.zeros_like(x)
    for c in range(D_FF // _FF_CHUNK):
        sl = slice(c * _FF_CHUNK, (c + 1) * _FF_CHUNK)
        a = _dot(u, wa_ref[:, sl])
        b = _dot(u, wb_ref[:, sl])
        act = (a * jax.nn.sigmoid(a) * b).astype(wo_ref.dtype)
        acc = acc + _dot(act, wo_ref[sl, :])
    h = x + 0.5 * acc
    if final_norm:
        h = _rms(h, gf_ref[...])
    o_ref[...] = h


def _ffn(x, g, w_in, w_out, g_final, tm):
    n = x.shape[0]
    wa = w_in[:, :D_FF].astype(_MXU_DTYPE)
    wb = w_in[:, D_FF:].astype(_MXU_DTYPE)
    wo = w_out.astype(_MXU_DTYPE)
    row = pl.BlockSpec((tm, D_MODEL), lambda i: (i, 0))
    vec = _resident((1, D_MODEL))
    in_specs = [row, vec, _resident(wa.shape), _resident(wb.shape), _resident(wo.shape)]
    args = [x, g.reshape(1, D_MODEL), wa, wb, wo]
    if g_final is not None:
        in_specs.append(vec)
        args.append(g_final.reshape(1, D_MODEL))
    return pl.pallas_call(
        functools.partial(_ffn_kernel, final_norm=g_final is not None),
        grid=(n // tm,),
        in_specs=in_specs,
        out_specs=row,
        out_shape=jax.ShapeDtypeStruct((n, D_MODEL), jnp.float32),
        compiler_params=_params(("parallel",)),
        name="ffn_final" if g_final is not None else "ffn",
    )(*args)


def _swap_halves(x):
    lane = lax.broadcasted_iota(jnp.int32, x.shape, 1)
    first = (lane & (HEAD_DIM - 1)) < HEAD_DIM // 2
    return jnp.where(first, pltpu.roll(x, LANES - HEAD_DIM // 2, 1), pltpu.roll(x, HEAD_DIM // 2, 1))


def _rope(x, cos, sin_signed):
    tiles = []
    for j in range(x.shape[1] // LANES):
        xt = x[:, j * LANES:(j + 1) * LANES]
        tiles.append(xt * cos + _swap_halves(xt) * sin_signed)
    return tiles[0] if len(tiles) == 1 else jnp.concatenate(tiles, axis=1)


def _proj_kernel(h_ref, g_ref, w_ref, bg_ref, cos_ref, sin_ref,
                 q_ref, qi_ref, k_ref, kw_ref, v_ref, c_ref, ga_ref, gc_ref):
    u = _rms(h_ref[...], g_ref[...]).astype(w_ref.dtype)
    cos = cos_ref[...]
    sin = sin_ref[...]

    def proj(lo, hi):
        return _dot(u, w_ref[:, lo:hi])

    q_ref[...] = (_rope(proj(_C_Q, _C_QI), cos, sin) * (HEAD_DIM ** -0.5)).astype(q_ref.dtype)
    qi_ref[...] = _rope(proj(_C_QI, _C_K), cos, sin).astype(qi_ref.dtype)
    k_ref[...] = _rope(proj(_C_K, _C_KW), cos, sin)
    zkw = proj(_C_KW, _C_V)
    lane = lax.broadcasted_iota(jnp.int32, zkw.shape, 1)
    kw_ref[...] = jnp.where(lane < IDX_DIM, _rope(zkw, cos, sin), zkw * (IDX_HEADS ** -0.5))
    v_ref[...] = proj(_C_V, _C_CA)
    c_ref[...] = proj(_C_CA, _C_CB) * jax.nn.sigmoid(proj(_C_CB, _C_GA))
    ga_ref[...] = jax.nn.sigmoid(proj(_C_GA, _C_GC) + bg_ref[:, :D_MODEL])
    gc_ref[...] = jax.nn.sigmoid(proj(_C_GC, _C_END) + bg_ref[:, D_MODEL:])


def _proj(h, g, w_main, b_gate, cos, sin, tm):
    n = h.shape[0]

    def row(width):
        return pl.BlockSpec((tm, width), lambda i: (i, 0))

    f32 = jnp.float32
    out_shape = (
        jax.ShapeDtypeStruct((n, Q_COLS), _MXU_DTYPE),
        jax.ShapeDtypeStruct((n, IDXQ_COLS), _MXU_DTYPE),
        jax.ShapeDtypeStruct((n, KV_COLS), f32),
        jax.ShapeDtypeStruct((n, LANES), f32),
        jax.ShapeDtypeStruct((n, KV_COLS), f32),
        jax.ShapeDtypeStruct((n, CONV_CH), f32),
        jax.ShapeDtypeStruct((n, D_MODEL), f32),
        jax.ShapeDtypeStruct((n, D_MODEL), f32),
    )
    return pl.pallas_call(
        _proj_kernel,
        grid=(n // tm,),
        in_specs=[row(D_MODEL), _resident((1, D_MODEL)), _resident(w_main.shape),
                  _resident((1, 2 * D_MODEL)), row(LANES), row(LANES)],
        out_specs=tuple(row(s.shape[1]) for s in out_shape),
        out_shape=out_shape,
        compiler_params=_params(("parallel",)),
        name="proj",
    )(h, g.reshape(1, D_MODEL), w_main, b_gate.reshape(1, 2 * D_MODEL), cos, sin)


def _key_to_float(u):
    key = u ^ _INT_MIN
    bits = jnp.where(key >= 0, key, key ^ 0x7FFFFFFF)
    return lax.bitcast_convert_type(bits, jnp.float32)


def _attn_kernel(qi_ref, wit_ref, q_ref, kidx_ref, k_ref, vt_ref, o_ref,
                 s_ref, m_ref, l_ref, acc_ref, *, past, tq, kt, top):
    t0 = pl.program_id(1) * tq
    n_kt = (past + t0 + tq + kt - 1) // kt
    lane_q = lax.broadcasted_iota(jnp.int32, (1, tq), 1)
    limit = past + ((t0 + lane_q) // CHUNK + 1) * CHUNK
    w = wit_ref[...]

    def score_tile(j, last):
        off = pl.multiple_of(j * kt, kt)
        kx = kidx_ref[pl.ds(off, kt), :]
        acc = jnp.zeros((kt, tq), jnp.float32)
        for h in range(IDX_HEADS):
            s = _dot_nt(kx, qi_ref[:, h * IDX_DIM:(h + 1) * IDX_DIM])
            acc = acc + w[h:h + 1, :] * jnp.maximum(s, 0.0)
        if last:
            key = off + lax.broadcasted_iota(jnp.int32, (kt, tq), 0)
            acc = jnp.where(key < limit, acc, -jnp.inf)
        s_ref[pl.ds(off, kt), :] = acc

    def score_body(j, carry):
        score_tile(j, False)
        return carry

    lax.fori_loop(0, n_kt - 1, score_body, 0)
    score_tile(n_kt - 1, True)

    def count_ge(thr):
        def body(j, acc):
            blk = s_ref[pl.ds(pl.multiple_of(j * kt, kt), kt), :]
            ones = jnp.where(blk >= thr, 1.0, 0.0)
            return acc + ones.reshape(kt // 8, 8, tq).sum(axis=0)
        acc = lax.fori_loop(0, n_kt, body, jnp.zeros((8, tq), jnp.float32))
        return acc.sum(axis=0, keepdims=True)

    def descend(p, prefix):
        cand = prefix | lax.shift_left(jnp.int32(1), 31 - p)
        return jnp.where(count_ge(_key_to_float(cand)) >= top, cand, prefix)

    prefix = lax.fori_loop(0, 32, descend, jnp.zeros((1, tq), jnp.int32))
    key = jnp.maximum(prefix ^ _INT_MIN, _KEY_NEG_FLT_MAX)
    thr = lax.bitcast_convert_type(jnp.where(key >= 0, key, key ^ 0x7FFFFFFF), jnp.float32)

    def drop_lowest(cnt):
        over = cnt > top

        def tile(j):
            off = pl.multiple_of(j * kt, kt)
            return off, s_ref[pl.ds(off, kt), :]

        def min_body(j, vmin):
            _, blk = tile(j)
            cand = jnp.where(blk >= thr, blk, jnp.inf)
            return jnp.minimum(vmin, cand.reshape(kt // 8, 8, tq).min(axis=0))

        vmin = lax.fori_loop(0, n_kt, min_body, jnp.full((8, tq), jnp.inf, jnp.float32))
        vmin = vmin.min(axis=0, keepdims=True)

        def idx_body(j, imax):
            off, blk = tile(j)
            key_idx = off + lax.broadcasted_iota(jnp.int32, (kt, tq), 0)
            cand = jnp.where(blk == vmin, key_idx, -1)
            return jnp.maximum(imax, cand.reshape(kt // 8, 8, tq).max(axis=0))

        imax = lax.fori_loop(0, n_kt, idx_body, jnp.full((8, tq), -1, jnp.int32))
        kill = jnp.where(over, imax.max(axis=0, keepdims=True), -1)

        def kill_body(j, carry):
            off, blk = tile(j)
            key_idx = off + lax.broadcasted_iota(jnp.int32, (kt, tq), 0)
            s_ref[pl.ds(off, kt), :] = jnp.where(key_idx == kill, -jnp.inf, blk)
            return carry

        lax.fori_loop(0, n_kt, kill_body, 0)
        return jnp.where(over, cnt - 1.0, cnt)

    lax.while_loop(lambda cnt: jnp.max(cnt) > top, drop_lowest, count_ge(thr))

    m_ref[...] = jnp.full(m_ref.shape, _NEG, jnp.float32)
    l_ref[...] = jnp.zeros(l_ref.shape, jnp.float32)
    acc_ref[...] = jnp.zeros(acc_ref.shape, jnp.float32)

    def attn_tile(j, carry):
        off = pl.multiple_of(j * kt, kt)
        sel = s_ref[pl.ds(off, kt), :] >= thr
        for c in range(N_KV_HEADS):
            kx = k_ref[pl.ds(off, kt), c * HEAD_DIM:(c + 1) * HEAD_DIM]
            vx = vt_ref[c * HEAD_DIM:(c + 1) * HEAD_DIM, pl.ds(off, kt)]
            for g in range(GROUP):
                h = c * GROUP + g
                lg = _dot_nt(kx, q_ref[:, h * HEAD_DIM:(h + 1) * HEAD_DIM])
                lg = jnp.where(sel, lg, _NEG)
                m_old = m_ref[h:h + 1, :]
                m_new = jnp.maximum(m_old, lg.max(axis=0, keepdims=True))
                alpha = jnp.exp(m_old - m_new)
                p = jnp.exp(lg - m_new)
                l_ref[h:h + 1, :] = alpha * l_ref[h:h + 1, :] + p.sum(axis=0, keepdims=True)
                acc_ref[h] = alpha * acc_ref[h] + _dot(vx, p.astype(vx.dtype))
                m_ref[h:h + 1, :] = m_new
        return carry

    lax.fori_loop(0, n_kt, attn_tile, 0)
    for h in range(N_HEADS):
        o_ref[h * HEAD_DIM:(h + 1) * HEAD_DIM, :] = acc_ref[h] / l_ref[h:h + 1, :]


def _attn(qi, wit, q, kidx, k, vt, *, past, tq, kt):
    b, t, _ = q.shape
    lp = k.shape[1]
    n_keys = past + t
    assert t % tq == 0 and tq % CHUNK == 0 and lp % kt == 0 and lp >= n_keys
    assert t == tq or (past == 0 and kt == tq)
    top = min(TOPK_MAX, n_keys // 4)
    return pl.pallas_call(
        functools.partial(_attn_kernel, past=past, tq=tq, kt=kt, top=top),
        grid=(b, t // tq),
        in_specs=[
            pl.BlockSpec((None, tq, IDXQ_COLS), lambda bi, i: (bi, i, 0)),
            pl.BlockSpec((None, IDX_HEADS, tq), lambda bi, i: (bi, 0, i)),
            pl.BlockSpec((None, tq, Q_COLS), lambda bi, i: (bi, i, 0)),
            pl.BlockSpec((None, lp, IDX_DIM), lambda bi, i: (bi, 0, 0)),
            pl.BlockSpec((None, lp, KV_COLS), lambda bi, i: (bi, 0, 0)),
            pl.BlockSpec((None, KV_COLS, lp), lambda bi, i: (bi, 0, 0)),
        ],
        out_specs=pl.BlockSpec((None, Q_COLS, tq), lambda bi, i: (bi, 0, i)),
        out_shape=jax.ShapeDtypeStruct((b, Q_COLS, t), jnp.float32),
        scratch_shapes=[
            pltpu.VMEM((lp, tq), jnp.float32),
            pltpu.VMEM((N_HEADS, tq), jnp.float32),
            pltpu.VMEM((N_HEADS, tq), jnp.float32),
            pltpu.VMEM((N_HEADS, HEAD_DIM, tq), jnp.float32),
        ],
        compiler_params=_params(("parallel", "arbitrary")),
        name=f"attn_tq{tq}",
    )(qi, wit, q, kidx, k, vt)


def _merge_kernel(h_ref, c_ref, halo_ref, at_ref, ga_ref, gc_ref, cw_ref, cb_ref, lng_ref, lnb_ref,
                  wco_ref, wao_ref, wo_ref, o_ref, win_ref):
    n_chunk = c_ref.shape[0] // CHUNK
    lead = _HALO - (CONV_WIDTH - 1)
    dcs = []
    for j in range(n_chunk):
        win_ref[j, 0:_HALO, :] = halo_ref[j]
        win_ref[j, _HALO:_HALO + CHUNK, :] = c_ref[j * CHUNK:(j + 1) * CHUNK, :]
        acc = jnp.zeros((CHUNK, CONV_CH), jnp.float32)
        for tap in range(CONV_WIDTH):
            acc = acc + win_ref[j, lead + tap:lead + tap + CHUNK, :] * cw_ref[tap:tap + 1, :]
        dcs.append(acc)
    dc = jnp.concatenate(dcs, axis=0) + cb_ref[...]
    mu = jnp.mean(dc, axis=-1, keepdims=True)
    var = jnp.mean(jnp.square(dc - mu), axis=-1, keepdims=True)
    y = (dc - mu) * lax.rsqrt(var + EPS) * lng_ref[...] + lnb_ref[...]
    conv_out = _dot((y * jax.nn.sigmoid(y)).astype(wco_ref.dtype), wco_ref[...])
    attn_out = _dot(at_ref[...], wao_ref[...])
    merged = ga_ref[...] * attn_out + gc_ref[...] * conv_out
    o_ref[...] = h_ref[...] + _dot(merged.astype(wo_ref.dtype), wo_ref[...])


def _merge(h, c, halo, attn, ga, gc, conv_w, conv_b, ln_g, ln_b, conv_w_out, attn_w_out, w_out, tm):
    n = h.shape[0]

    def row(width):
        return pl.BlockSpec((tm, width), lambda i: (i, 0))

    cw = jnp.zeros((_HALO, CONV_CH), jnp.float32).at[:CONV_WIDTH].set(conv_w)
    vec = _resident((1, CONV_CH))
    return pl.pallas_call(
        _merge_kernel,
        grid=(n // tm,),
        in_specs=[
            row(D_MODEL), row(CONV_CH),
            pl.BlockSpec((tm // CHUNK, _HALO, CONV_CH), lambda i: (i, 0, 0)),
            row(Q_COLS), row(D_MODEL), row(D_MODEL),
            _resident(cw.shape), vec, vec, vec,
            _resident((CONV_CH, D_MODEL)), _resident((Q_COLS, D_MODEL)), _resident((D_MODEL, D_MODEL)),
        ],
        out_specs=row(D_MODEL),
        out_shape=jax.ShapeDtypeStruct((n, D_MODEL), jnp.float32),
        scratch_shapes=[pltpu.VMEM((tm // CHUNK, _HALO + CHUNK, CONV_CH), jnp.float32)],
        compiler_params=_params(("parallel",)),
        name="merge",
    )(h, c, halo, attn, ga, gc, cw, conv_b.reshape(1, CONV_CH), ln_g.reshape(1, CONV_CH),
      ln_b.reshape(1, CONV_CH), conv_w_out.astype(_MXU_DTYPE), attn_w_out.astype(_MXU_DTYPE),
      w_out.astype(_MXU_DTYPE))


def _row_tile(n, candidates):
    for tm in candidates:
        if n % tm == 0:
            return tm
    raise ValueError(f"row count {n} is not a multiple of {candidates[-1]}")


def _rope_tables(pos):
    inv = ROPE_THETA ** (-jnp.arange(0, HEAD_DIM, 2, dtype=jnp.float32) / HEAD_DIM)
    ang = pos.astype(jnp.float32)[:, None] * inv[None, :]
    cos, sin = jnp.cos(ang), jnp.sin(ang)
    return jnp.tile(cos, (1, 4)), jnp.tile(jnp.concatenate([-sin, sin], axis=1), (1, 2))


def _main_weight(w_in):
    sizes = (Q_COLS, KV_COLS, KV_COLS, IDXQ_COLS, IDX_DIM, IDX_HEADS, 2 * CONV_CH, 2 * D_MODEL)
    offs = [0]
    for s in sizes:
        offs.append(offs[-1] + s)
    wq, wk, wv, wqi, wki, wwi, wconv, wgate = (w_in[:, offs[i]:offs[i + 1]] for i in range(len(sizes)))
    pad = jnp.zeros((D_MODEL, LANES - IDX_DIM - IDX_HEADS), w_in.dtype)
    return jnp.concatenate([wq, wqi, wk, wki, wwi, pad, wv, wconv, wgate], axis=1).astype(_MXU_DTYPE)


def kernel(x_prompt, x_sample, cache_k, cache_v, cache_idx_k, state_conv, ffn1_norm, ffn1_w_in, ffn1_w_out, mix_norm, w_in, b_gate, conv_w, conv_b, conv_ln_g, conv_ln_b, conv_w_out, attn_w_out, w_out, ffn2_norm, ffn2_w_in, ffn2_w_out, final_norm):
    assert ffn1_norm.shape[0] == 1, "one layer"
    bp, tp, _ = x_prompt.shape
    bs, ts, _ = x_sample.shape
    past = cache_k.shape[2]
    n_p, n_s = bp * tp, bs * ts
    n = n_p + n_s
    assert tp % CHUNK == 0 and ts % CHUNK == 0
    tm = _row_tile(n, (512, 256, 128, 64))
    tm_merge = _row_tile(n, (256, 128, 64))

    x = jnp.concatenate([x_prompt.reshape(n_p, D_MODEL), x_sample.reshape(n_s, D_MODEL)], axis=0)
    h = _ffn(x, ffn1_norm[0], ffn1_w_in[0], ffn1_w_out[0], None, tm)

    cos_p, sin_p = _rope_tables(jnp.arange(tp, dtype=jnp.int32))
    cos_s, sin_s = _rope_tables(past + jnp.arange(ts, dtype=jnp.int32))
    cos = jnp.concatenate([jnp.tile(cos_p, (bp, 1)), jnp.tile(cos_s, (bs, 1))], axis=0)
    sin = jnp.concatenate([jnp.tile(sin_p, (bp, 1)), jnp.tile(sin_s, (bs, 1))], axis=0)
    q, qi, k, kw, v, c, ga, gc = _proj(h, mix_norm[0], _main_weight(w_in[0]), b_gate[0], cos, sin, tm)
    ki = kw[:, :IDX_DIM]
    wi = kw[:, IDX_DIM:IDX_DIM + IDX_HEADS]

    def per_seq(a, lo, b, t):
        return a[lo:lo + b * t].reshape(b, t, a.shape[1])

    tq_p = _row_tile(tp, (256, 128, 64))
    attn_p = _attn(
        per_seq(qi, 0, bp, tp), per_seq(wi, 0, bp, tp).transpose(0, 2, 1), per_seq(q, 0, bp, tp),
        per_seq(ki, 0, bp, tp).astype(_MXU_DTYPE), per_seq(k, 0, bp, tp).astype(_MXU_DTYPE),
        per_seq(v, 0, bp, tp).transpose(0, 2, 1).astype(_MXU_DTYPE),
        past=0, tq=tq_p, kt=tq_p)

    kt_s = LANES
    n_keys = past + ts
    pad_keys = -n_keys % kt_s

    def with_cache(cache, new):
        cache = cache.reshape(bs, past, -1)
        pad = jnp.zeros((bs, pad_keys, cache.shape[2]), cache.dtype)
        return jnp.concatenate([cache, new, pad], axis=1).astype(_MXU_DTYPE)

    attn_s = _attn(
        per_seq(qi, n_p, bs, ts), per_seq(wi, n_p, bs, ts).transpose(0, 2, 1), per_seq(q, n_p, bs, ts),
        with_cache(cache_idx_k[0], per_seq(ki, n_p, bs, ts)),
        with_cache(cache_k[0], per_seq(k, n_p, bs, ts)),
        with_cache(cache_v[0], per_seq(v, n_p, bs, ts)).transpose(0, 2, 1),
        past=past, tq=ts, kt=kt_s)
    attn = jnp.concatenate([attn_p.transpose(0, 2, 1).reshape(n_p, Q_COLS),
                            attn_s.transpose(0, 2, 1).reshape(n_s, Q_COLS)], axis=0).astype(_MXU_DTYPE)

    tails = c.reshape(n // CHUNK, CHUNK, CONV_CH)[:, CHUNK - _HALO:, :]
    prev = jnp.concatenate([jnp.zeros((1, _HALO, CONV_CH), c.dtype), tails[:-1]], axis=0)
    state = jnp.concatenate(
        [jnp.zeros((bs, _HALO - (CONV_WIDTH - 1), CONV_CH), c.dtype), state_conv[0]], axis=1)
    halo_p = prev[:n_p // CHUNK].reshape(bp, tp // CHUNK, _HALO, CONV_CH).at[:, 0].set(0.0)
    halo_s = prev[n_p // CHUNK:].reshape(bs, ts // CHUNK, _HALO, CONV_CH).at[:, 0].set(state)
    halo = jnp.concatenate([halo_p.reshape(-1, _HALO, CONV_CH), halo_s.reshape(-1, _HALO, CONV_CH)], axis=0)

    h2 = _merge(h, c, halo, attn, ga, gc, conv_w[0], conv_b[0], conv_ln_g[0], conv_ln_b[0],
                conv_w_out[0], attn_w_out[0], w_out[0], tm_merge)
    y = _ffn(h2, ffn2_norm[0], ffn2_w_in[0], ffn2_w_out[0], final_norm, tm)

    c_p = per_seq(c, 0, bp, tp)
    c_s = jnp.concatenate([state_conv[0], per_seq(c, n_p, bs, ts)], axis=1)
    keep = CONV_WIDTH - 1
    assert tp >= keep
    return (
        y[:n_p].reshape(bp, tp, D_MODEL),
        y[n_p:].reshape(bs, ts, D_MODEL),
        per_seq(k, 0, bp, tp).reshape(1, bp, tp, N_KV_HEADS, HEAD_DIM),
        per_seq(v, 0, bp, tp).reshape(1, bp, tp, N_KV_HEADS, HEAD_DIM),
        per_seq(ki, 0, bp, tp)[None],
        c_p[:, tp - keep:][None],
        per_seq(k, n_p, bs, ts).reshape(1, bs, ts, N_KV_HEADS, HEAD_DIM),
        per_seq(v, n_p, bs, ts).reshape(1, bs, ts, N_KV_HEADS, HEAD_DIM),
        per_seq(ki, n_p, bs, ts)[None],
        c_s[:, -keep:][None],
    )
```

```python
import functools

import jax
import jax.numpy as jnp
from jax import lax
from jax.experimental import pallas as pl
from jax.experimental.pallas import tpu as pltpu

D_MODEL = 1024
CHUNK = 64
N_HEADS = 8
N_KV_HEADS = 2
HEAD_DIM = 64
GROUP = N_HEADS // N_KV_HEADS
IDX_HEADS = 8
IDX_DIM = 64
TOPK_MAX = 256
CONV_CH = 512
CONV_WIDTH = 31
D_FF = 2816
ROPE_THETA = 10000.0
EPS = 1e-6

Q_COLS = N_HEADS * HEAD_DIM
KV_COLS = N_KV_HEADS * HEAD_DIM
IDXQ_COLS = IDX_HEADS * IDX_DIM

LANES = 128
_MXU_DTYPE = jnp.bfloat16
_VMEM_LIMIT = 56 * 1024 * 1024
_FF_CHUNK = 256
_HALO = 32
_NEG = -1e30

_C_Q = 0
_C_QI = _C_Q + Q_COLS
_C_K = _C_QI + IDXQ_COLS
_C_KW = _C_K + KV_COLS
_C_V = _C_KW + LANES
_C_CA = _C_V + KV_COLS
_C_CB = _C_CA + CONV_CH
_C_GA = _C_CB + CONV_CH
_C_GC = _C_GA + D_MODEL
_C_END = _C_GC + D_MODEL

_INT_MIN = -2147483648
_KEY_NEG_FLT_MAX = -2139095040


def _params(sem):
    return pltpu.CompilerParams(dimension_semantics=sem, vmem_limit_bytes=_VMEM_LIMIT)


def _resident(shape):
    nd = len(shape)
    return pl.BlockSpec(shape, lambda *_: (0,) * nd, pipeline_mode=pl.Buffered(1))


def _rms(x, g):
    return x * lax.rsqrt(jnp.mean(x * x, axis=-1, keepdims=True) + EPS) * g


def _dot(a, b):
    return jnp.dot(a, b, preferred_element_type=jnp.float32)


def _dot_nt(a, b):
    return lax.dot_general(a, b, (((1,), (1,)), ((), ())), preferred_element_type=jnp.float32)


def _ffn_kernel(*refs, final_norm):
    if final_norm:
        x_ref, g_ref, wa_ref, wb_ref, wo_ref, gf_ref, o_ref = refs
    else:
        x_ref, g_ref, wa_ref, wb_ref, wo_ref, o_ref = refs
    x = x_ref[...]
    u = _rms(x, g_ref[...]).astype(wa_ref.dtype)
    acc = jnp.zeros_like(x)
    for c in range(D_FF // _FF_CHUNK):
        sl = slice(c * _FF_CHUNK, (c + 1) * _FF_CHUNK)
        a = _dot(u, wa_ref[:, sl])
        b = _dot(u, wb_ref[:, sl])
        act = (a * jax.nn.sigmoid(a) * b).astype(wo_ref.dtype)
        acc = acc + _dot(act, wo_ref[sl, :])
    h = x + 0.5 * acc
    if final_norm:
        h = _rms(h, gf_ref[...])
    o_ref[...] = h


def _ffn(x, g, w_in, w_out, g_final, tm):
    n = x.shape[0]
    wa = w_in[:, :D_FF].astype(_MXU_DTYPE)
    wb = w_in[:, D_FF:].astype(_MXU_DTYPE)
    wo = w_out.astype(_MXU_DTYPE)
    row = pl.BlockSpec((tm, D_MODEL), lambda i: (i, 0))
    vec = _resident((1, D_MODEL))
    in_specs = [row, vec, _resident(wa.shape), _resident(wb.shape), _resident(wo.shape)]
    args = [x, g.reshape(1, D_MODEL), wa, wb, wo]
    if g_final is not None:
        in_specs.append(vec)
        args.append(g_final.reshape(1, D_MODEL))
    return pl.pallas_call(
        functools.partial(_ffn_kernel, final_norm=g_final is not None),
        grid=(n // tm,),
        in_specs=in_specs,
        out_specs=row,
        out_shape=jax.ShapeDtypeStruct((n, D_MODEL), jnp.float32),
        compiler_params=_params(("parallel",)),
        name="ffn_final" if g_final is not None else "ffn",
    )(*args)


def _swap_halves(x):
    lane = lax.broadcasted_iota(jnp.int32, x.shape, 1)
    first = (lane & (HEAD_DIM - 1)) < HEAD_DIM // 2
    return jnp.where(first, pltpu.roll(x, LANES - HEAD_DIM // 2, 1), pltpu.roll(x, HEAD_DIM // 2, 1))


def _rope(x, cos, sin_signed):
    tiles = []
    for j in range(x.shape[1] // LANES):
        xt = x[:, j * LANES:(j + 1) * LANES]
        tiles.append(xt * cos + _swap_halves(xt) * sin_signed)
    return tiles[0] if len(tiles) == 1 else jnp.concatenate(tiles, axis=1)


def _proj_kernel(h_ref, g_ref, w_ref, bg_ref, cos_ref, sin_ref,
                 q_ref, qi_ref, k_ref, kw_ref, v_ref, c_ref, ga_ref, gc_ref):
    u = _rms(h_ref[...], g_ref[...]).astype(w_ref.dtype)
    cos = cos_ref[...]
    sin = sin_ref[...]

    def proj(lo, hi):
        return _dot(u, w_ref[:, lo:hi])

    q_ref[...] = (_rope(proj(_C_Q, _C_QI), cos, sin) * (HEAD_DIM ** -0.5)).astype(q_ref.dtype)
    qi_ref[...] = _rope(proj(_C_QI, _C_K), cos, sin).astype(qi_ref.dtype)
    k_ref[...] = _rope(proj(_C_K, _C_KW), cos, sin)
    zkw = proj(_C_KW, _C_V)
    lane = lax.broadcasted_iota(jnp.int32, zkw.shape, 1)
    kw_ref[...] = jnp.where(lane < IDX_DIM, _rope(zkw, cos, sin), zkw * (IDX_HEADS ** -0.5))
    v_ref[...] = proj(_C_V, _C_CA)
    c_ref[...] = proj(_C_CA, _C_CB) * jax.nn.sigmoid(proj(_C_CB, _C_GA))
    ga_ref[...] = jax.nn.sigmoid(proj(_C_GA, _C_GC) + bg_ref[:, :D_MODEL])
    gc_ref[...] = jax.nn.sigmoid(proj(_C_GC, _C_END) + bg_ref[:, D_MODEL:])


def _proj(h, g, w_main, b_gate, cos, sin, tm):
    n = h.shape[0]

    def row(width):
        return pl.BlockSpec((tm, width), lambda i: (i, 0))

    f32 = jnp.float32
    out_shape = (
        jax.ShapeDtypeStruct((n, Q_COLS), _MXU_DTYPE),
        jax.ShapeDtypeStruct((n, IDXQ_COLS), _MXU_DTYPE),
        jax.ShapeDtypeStruct((n, KV_COLS), f32),
        jax.ShapeDtypeStruct((n, LANES), f32),
        jax.ShapeDtypeStruct((n, KV_COLS), f32),
        jax.ShapeDtypeStruct((n, CONV_CH), f32),
        jax.ShapeDtypeStruct((n, D_MODEL), f32),
        jax.ShapeDtypeStruct((n, D_MODEL), f32),
    )
    return pl.pallas_call(
        _proj_kernel,
        grid=(n // tm,),
        in_specs=[row(D_MODEL), _resident((1, D_MODEL)), _resident(w_main.shape),
                  _resident((1, 2 * D_MODEL)), row(LANES), row(LANES)],
        out_specs=tuple(row(s.shape[1]) for s in out_shape),
        out_shape=out_shape,
        compiler_params=_params(("parallel",)),
        name="proj",
    )(h, g.reshape(1, D_MODEL), w_main, b_gate.reshape(1, 2 * D_MODEL), cos, sin)


def _key_to_float(u):
    key = u ^ _INT_MIN
    bits = jnp.where(key >= 0, key, key ^ 0x7FFFFFFF)
    return lax.bitcast_convert_type(bits, jnp.float32)


def _attn_kernel(qi_ref, wit_ref, q_ref, kidx_ref, k_ref, vt_ref, o_ref,
                 s_ref, eq_ref, bias_ref, lg_ref, acc_ref, *, past, tq, kt, top):
    t0 = pl.program_id(1) * tq
    n_kt = (past + t0 + tq + kt - 1) // kt
    lane_q = lax.broadcasted_iota(jnp.int32, (1, tq), 1)
    limit = past + ((t0 + lane_q) // CHUNK + 1) * CHUNK
    w = wit_ref[...]

    def score_tile(j, last):
        off = pl.multiple_of(j * kt, kt)
        kx = kidx_ref[pl.ds(off, kt), :]
        acc = jnp.zeros((kt, tq), jnp.float32)
        for h in range(IDX_HEADS):
            s = _dot_nt(kx, qi_ref[:, h * IDX_DIM:(h + 1) * IDX_DIM])
            acc = acc + w[h:h + 1, :] * jnp.maximum(s, 0.0)
        if last:
            key = off + lax.broadcasted_iota(jnp.int32, (kt, tq), 0)
            acc = jnp.where(key < limit, acc, -jnp.inf)
        s_ref[pl.ds(off, kt), :] = acc

    def score_body(j, carry):
        score_tile(j, False)
        return carry

    lax.fori_loop(0, n_kt - 1, score_body, 0)
    score_tile(n_kt - 1, True)

    def count_ge(thr):
        def body(j, acc):
            blk = s_ref[pl.ds(pl.multiple_of(j * kt, kt), kt), :]
            ones = jnp.where(blk >= thr, 1.0, 0.0)
            return acc + ones.reshape(kt // 8, 8, tq).sum(axis=0)
        acc = lax.fori_loop(0, n_kt, body, jnp.zeros((8, tq), jnp.float32))
        return acc.sum(axis=0, keepdims=True)

    def descend(p, prefix):
        cand = prefix | lax.shift_left(jnp.int32(1), 31 - p)
        return jnp.where(count_ge(_key_to_float(cand)) >= top, cand, prefix)

    prefix = lax.fori_loop(0, 32, descend, jnp.zeros((1, tq), jnp.int32))
    key = jnp.maximum(prefix ^ _INT_MIN, _KEY_NEG_FLT_MAX)
    thr = lax.bitcast_convert_type(jnp.where(key >= 0, key, key ^ 0x7FFFFFFF), jnp.float32)

    idx_bits = max(1, (s_ref.shape[0] - 1).bit_length())

    def resolve_ties(cnt):
        def mark(j, carry):
            off = pl.multiple_of(j * kt, kt)
            key_idx = off + lax.broadcasted_iota(jnp.int32, (kt, tq), 0)
            eq_ref[pl.ds(off, kt), :] = jnp.where(s_ref[pl.ds(off, kt), :] == thr, key_idx, -1)
            return carry

        lax.fori_loop(0, n_kt, mark, 0)

        def count_idx_ge(x):
            def body(j, acc):
                blk = eq_ref[pl.ds(pl.multiple_of(j * kt, kt), kt), :]
                ones = jnp.where(blk >= x, 1.0, 0.0)
                return acc + ones.reshape(kt // 8, 8, tq).sum(axis=0)
            acc = lax.fori_loop(0, n_kt, body, jnp.zeros((8, tq), jnp.float32))
            return acc.sum(axis=0, keepdims=True)

        want = cnt - top + 1.0

        def descend_idx(p, x):
            cand = x | lax.shift_left(jnp.int32(1), idx_bits - 1 - p)
            return jnp.where(count_idx_ge(cand) >= want, cand, x)

        last = lax.fori_loop(0, idx_bits, descend_idx, jnp.zeros((1, tq), jnp.int32))

        def kill(j, carry):
            rows = pl.ds(pl.multiple_of(j * kt, kt), kt)
            s_ref[rows, :] = jnp.where(eq_ref[rows, :] > last, -jnp.inf, s_ref[rows, :])
            return carry

        lax.fori_loop(0, n_kt, kill, 0)
        return count_ge(thr)

    cnt = count_ge(thr)
    cnt = lax.cond(jnp.max(cnt) > top, resolve_ties, lambda c: c, cnt)

    def drop_lowest(cnt):
        over = cnt > top

        def tile(j):
            off = pl.multiple_of(j * kt, kt)
            return off, s_ref[pl.ds(off, kt), :]

        def min_body(j, vmin):
            _, blk = tile(j)
            cand = jnp.where(blk >= thr, blk, jnp.inf)
            return jnp.minimum(vmin, cand.reshape(kt // 8, 8, tq).min(axis=0))

        vmin = lax.fori_loop(0, n_kt, min_body, jnp.full((8, tq), jnp.inf, jnp.float32))
        vmin = vmin.min(axis=0, keepdims=True)

        def idx_body(j, imax):
            off, blk = tile(j)
            key_idx = off + lax.broadcasted_iota(jnp.int32, (kt, tq), 0)
            cand = jnp.where(blk == vmin, key_idx, -1)
            return jnp.maximum(imax, cand.reshape(kt // 8, 8, tq).max(axis=0))

        imax = lax.fori_loop(0, n_kt, idx_body, jnp.full((8, tq), -1, jnp.int32))
        kill = jnp.where(over, imax.max(axis=0, keepdims=True), -1)

        def kill_body(j, carry):
            off, blk = tile(j)
            key_idx = off + lax.broadcasted_iota(jnp.int32, (kt, tq), 0)
            s_ref[pl.ds(off, kt), :] = jnp.where(key_idx == kill, -jnp.inf, blk)
            return carry

        lax.fori_loop(0, n_kt, kill_body, 0)
        return jnp.where(over, cnt - 1.0, cnt)

    lax.while_loop(lambda cnt: jnp.max(cnt) > top, drop_lowest, cnt)

    acc_ref[...] = jnp.zeros(acc_ref.shape, jnp.float32)

    def attn_tile(j, carry):
        m, l = carry
        off = pl.multiple_of(j * kt, kt)
        bias_ref[...] = jnp.where(s_ref[pl.ds(off, kt), :] >= thr, 0.0, _NEG)
        tile_max = []
        for h in range(N_HEADS):
            c = h // GROUP
            kx = k_ref[pl.ds(off, kt), c * HEAD_DIM:(c + 1) * HEAD_DIM]
            lg = _dot_nt(kx, q_ref[:, h * HEAD_DIM:(h + 1) * HEAD_DIM]) + bias_ref[...]
            lg_ref[h] = lg
            tile_max.append(lg.max(axis=0, keepdims=True))
        m_new = jnp.maximum(m, jnp.concatenate(tile_max, axis=0))
        alpha = jnp.exp(m - m_new)
        tile_sum = []
        for h in range(N_HEADS):
            c = h // GROUP
            vx = vt_ref[c * HEAD_DIM:(c + 1) * HEAD_DIM, pl.ds(off, kt)]
            p = jnp.exp(lg_ref[h] - m_new[h:h + 1, :])
            tile_sum.append(p.sum(axis=0, keepdims=True))
            acc_ref[h] = alpha[h:h + 1, :] * acc_ref[h] + _dot(vx, p.astype(vx.dtype))
        return m_new, alpha * l + jnp.concatenate(tile_sum, axis=0)

    init = (jnp.full((N_HEADS, tq), _NEG, jnp.float32), jnp.zeros((N_HEADS, tq), jnp.float32))
    _, l = lax.fori_loop(0, n_kt, attn_tile, init)
    for h in range(N_HEADS):
        o_ref[h * HEAD_DIM:(h + 1) * HEAD_DIM, :] = acc_ref[h] / l[h:h + 1, :]


def _attn(qi, wit, q, kidx, k, vt, *, past, tq, kt):
    b, t, _ = q.shape
    lp = k.shape[1]
    n_keys = past + t
    assert t % tq == 0 and tq % CHUNK == 0 and lp % kt == 0 and lp >= n_keys
    assert t == tq or (past == 0 and kt == tq)
    top = min(TOPK_MAX, n_keys // 4)
    return pl.pallas_call(
        functools.partial(_attn_kernel, past=past, tq=tq, kt=kt, top=top),
        grid=(b, t // tq),
        in_specs=[
            pl.BlockSpec((None, tq, IDXQ_COLS), lambda bi, i: (bi, i, 0)),
            pl.BlockSpec((None, IDX_HEADS, tq), lambda bi, i: (bi, 0, i)),
            pl.BlockSpec((None, tq, Q_COLS), lambda bi, i: (bi, i, 0)),
            pl.BlockSpec((None, lp, IDX_DIM), lambda bi, i: (bi, 0, 0)),
            pl.BlockSpec((None, lp, KV_COLS), lambda bi, i: (bi, 0, 0)),
            pl.BlockSpec((None, KV_COLS, lp), lambda bi, i: (bi, 0, 0)),
        ],
        out_specs=pl.BlockSpec((None, Q_COLS, tq), lambda bi, i: (bi, 0, i)),
        out_shape=jax.ShapeDtypeStruct((b, Q_COLS, t), jnp.float32),
        scratch_shapes=[
            pltpu.VMEM((lp, tq), jnp.float32),
            pltpu.VMEM((lp, tq), jnp.int32),
            pltpu.VMEM((kt, tq), jnp.float32),
            pltpu.VMEM((N_HEADS, kt, tq), jnp.float32),
            pltpu.VMEM((N_HEADS, HEAD_DIM, tq), jnp.float32),
        ],
        compiler_params=_params(("parallel", "arbitrary")),
        name=f"attn_tq{tq}",
    )(qi, wit, q, kidx, k, vt)


def _merge_kernel(h_ref, c_ref, halo_ref, at_ref, ga_ref, gc_ref, cw_ref, cb_ref, lng_ref, lnb_ref,
                  wco_ref, wao_ref, wo_ref, o_ref, win_ref):
    n_chunk = c_ref.shape[0] // CHUNK
    lead = _HALO - (CONV_WIDTH - 1)
    dcs = []
    for j in range(n_chunk):
        win_ref[j, 0:_HALO, :] = halo_ref[j]
        win_ref[j, _HALO:_HALO + CHUNK, :] = c_ref[j * CHUNK:(j + 1) * CHUNK, :]
        acc = jnp.zeros((CHUNK, CONV_CH), jnp.float32)
        for tap in range(CONV_WIDTH):
            acc = acc + win_ref[j, lead + tap:lead + tap + CHUNK, :] * cw_ref[tap:tap + 1, :]
        dcs.append(acc)
    dc = jnp.concatenate(dcs, axis=0) + cb_ref[...]
    mu = jnp.mean(dc, axis=-1, keepdims=True)
    var = jnp.mean(jnp.square(dc - mu), axis=-1, keepdims=True)
    y = (dc - mu) * lax.rsqrt(var + EPS) * lng_ref[...] + lnb_ref[...]
    conv_out = _dot((y * jax.nn.sigmoid(y)).astype(wco_ref.dtype), wco_ref[...])
    attn_out = _dot(at_ref[...], wao_ref[...])
    merged = ga_ref[...] * attn_out + gc_ref[...] * conv_out
    o_ref[...] = h_ref[...] + _dot(merged.astype(wo_ref.dtype), wo_ref[...])


def _merge(h, c, halo, attn, ga, gc, conv_w, conv_b, ln_g, ln_b, conv_w_out, attn_w_out, w_out, tm):
    n = h.shape[0]

    def row(width):
        return pl.BlockSpec((tm, width), lambda i: (i, 0))

    cw = jnp.zeros((_HALO, CONV_CH), jnp.float32).at[:CONV_WIDTH].set(conv_w)
    vec = _resident((1, CONV_CH))
    return pl.pallas_call(
        _merge_kernel,
        grid=(n // tm,),
        in_specs=[
            row(D_MODEL), row(CONV_CH),
            pl.BlockSpec((tm // CHUNK, _HALO, CONV_CH), lambda i: (i, 0, 0)),
            row(Q_COLS), row(D_MODEL), row(D_MODEL),
            _resident(cw.shape), vec, vec, vec,
            _resident((CONV_CH, D_MODEL)), _resident((Q_COLS, D_MODEL)), _resident((D_MODEL, D_MODEL)),
        ],
        out_specs=row(D_MODEL),
        out_shape=jax.ShapeDtypeStruct((n, D_MODEL), jnp.float32),
        scratch_shapes=[pltpu.VMEM((tm // CHUNK, _HALO + CHUNK, CONV_CH), jnp.float32)],
        compiler_params=_params(("parallel",)),
        name="merge",
    )(h, c, halo, attn, ga, gc, cw, conv_b.reshape(1, CONV_CH), ln_g.reshape(1, CONV_CH),
      ln_b.reshape(1, CONV_CH), conv_w_out.astype(_MXU_DTYPE), attn_w_out.astype(_MXU_DTYPE),
      w_out.astype(_MXU_DTYPE))


def _row_tile(n, candidates):
    for tm in candidates:
        if n % tm == 0:
            return tm
    raise ValueError(f"row count {n} is not a multiple of {candidates[-1]}")


def _rope_tables(pos):
    inv = ROPE_THETA ** (-jnp.arange(0, HEAD_DIM, 2, dtype=jnp.float32) / HEAD_DIM)
    ang = pos.astype(jnp.float32)[:, None] * inv[None, :]
    cos, sin = jnp.cos(ang), jnp.sin(ang)
    return jnp.tile(cos, (1, 4)), jnp.tile(jnp.concatenate([-sin, sin], axis=1), (1, 2))


def _main_weight(w_in):
    sizes = (Q_COLS, KV_COLS, KV_COLS, IDXQ_COLS, IDX_DIM, IDX_HEADS, 2 * CONV_CH, 2 * D_MODEL)
    offs = [0]
    for s in sizes:
        offs.append(offs[-1] + s)
    wq, wk, wv, wqi, wki, wwi, wconv, wgate = (w_in[:, offs[i]:offs[i + 1]] for i in range(len(sizes)))
    pad = jnp.zeros((D_MODEL, LANES - IDX_DIM - IDX_HEADS), w_in.dtype)
    return jnp.concatenate([wq, wqi, wk, wki, wwi, pad, wv, wconv, wgate], axis=1).astype(_MXU_DTYPE)


def kernel(x_prompt, x_sample, cache_k, cache_v, cache_idx_k, state_conv, ffn1_norm, ffn1_w_in, ffn1_w_out, mix_norm, w_in, b_gate, conv_w, conv_b, conv_ln_g, conv_ln_b, conv_w_out, attn_w_out, w_out, ffn2_norm, ffn2_w_in, ffn2_w_out, final_norm):
    assert ffn1_norm.shape[0] == 1, "one layer"
    bp, tp, _ = x_prompt.shape
    bs, ts, _ = x_sample.shape
    past = cache_k.shape[2]
    n_p, n_s = bp * tp, bs * ts
    n = n_p + n_s
    assert tp % CHUNK == 0 and ts % CHUNK == 0
    tm = _row_tile(n, (512, 256, 128, 64))
    tm_merge = _row_tile(n, (256, 128, 64))

    x = jnp.concatenate([x_prompt.reshape(n_p, D_MODEL), x_sample.reshape(n_s, D_MODEL)], axis=0)
    h = _ffn(x, ffn1_norm[0], ffn1_w_in[0], ffn1_w_out[0], None, tm)

    cos_p, sin_p = _rope_tables(jnp.arange(tp, dtype=jnp.int32))
    cos_s, sin_s = _rope_tables(past + jnp.arange(ts, dtype=jnp.int32))
    cos = jnp.concatenate([jnp.tile(cos_p, (bp, 1)), jnp.tile(cos_s, (bs, 1))], axis=0)
    sin = jnp.concatenate([jnp.tile(sin_p, (bp, 1)), jnp.tile(sin_s, (bs, 1))], axis=0)
    q, qi, k, kw, v, c, ga, gc = _proj(h, mix_norm[0], _main_weight(w_in[0]), b_gate[0], cos, sin, tm)
    ki = kw[:, :IDX_DIM]
    wi = kw[:, IDX_DIM:IDX_DIM + IDX_HEADS]

    def per_seq(a, lo, b, t):
        return a[lo:lo + b * t].reshape(b, t, a.shape[1])

    tq_p = _row_tile(tp, (256, 128, 64))
    attn_p = _attn(
        per_seq(qi, 0, bp, tp), per_seq(wi, 0, bp, tp).transpose(0, 2, 1), per_seq(q, 0, bp, tp),
        per_seq(ki, 0, bp, tp).astype(_MXU_DTYPE), per_seq(k, 0, bp, tp).astype(_MXU_DTYPE),
        per_seq(v, 0, bp, tp).transpose(0, 2, 1).astype(_MXU_DTYPE),
        past=0, tq=tq_p, kt=tq_p)

    kt_s = LANES
    n_keys = past + ts
    pad_keys = -n_keys % kt_s

    def with_cache(cache, new):
        cache = cache.reshape(bs, past, -1)
        pad = jnp.zeros((bs, pad_keys, cache.shape[2]), cache.dtype)
        return jnp.concatenate([cache, new, pad], axis=1).astype(_MXU_DTYPE)

    attn_s = _attn(
        per_seq(qi, n_p, bs, ts), per_seq(wi, n_p, bs, ts).transpose(0, 2, 1), per_seq(q, n_p, bs, ts),
        with_cache(cache_idx_k[0], per_seq(ki, n_p, bs, ts)),
        with_cache(cache_k[0], per_seq(k, n_p, bs, ts)),
        with_cache(cache_v[0], per_seq(v, n_p, bs, ts)).transpose(0, 2, 1),
        past=past, tq=ts, kt=kt_s)
    attn = jnp.concatenate([attn_p.transpose(0, 2, 1).reshape(n_p, Q_COLS),
                            attn_s.transpose(0, 2, 1).reshape(n_s, Q_COLS)], axis=0).astype(_MXU_DTYPE)

    tails = c.reshape(n // CHUNK, CHUNK, CONV_CH)[:, CHUNK - _HALO:, :]
    prev = jnp.concatenate([jnp.zeros((1, _HALO, CONV_CH), c.dtype), tails[:-1]], axis=0)
    state = jnp.concatenate(
        [jnp.zeros((bs, _HALO - (CONV_WIDTH - 1), CONV_CH), c.dtype), state_conv[0]], axis=1)
    halo_p = prev[:n_p // CHUNK].reshape(bp, tp // CHUNK, _HALO, CONV_CH).at[:, 0].set(0.0)
    halo_s = prev[n_p // CHUNK:].reshape(bs, ts // CHUNK, _HALO, CONV_CH).at[:, 0].set(state)
    halo = jnp.concatenate([halo_p.reshape(-1, _HALO, CONV_CH), halo_s.reshape(-1, _HALO, CONV_CH)], axis=0)

    h2 = _merge(h, c, halo, attn, ga, gc, conv_w[0], conv_b[0], conv_ln_g[0], conv_ln_b[0],
                conv_w_out[0], attn_w_out[0], w_out[0], tm_merge)
    y = _ffn(h2, ffn2_norm[0], ffn2_w_in[0], ffn2_w_out[0], final_norm, tm)

    c_p = per_seq(c, 0, bp, tp)
    c_s = jnp.concatenate([state_conv[0], per_seq(c, n_p, bs, ts)], axis=1)
    keep = CONV_WIDTH - 1
    assert tp >= keep
    return (
        y[:n_p].reshape(bp, tp, D_MODEL),
        y[n_p:].reshape(bs, ts, D_MODEL),
        per_seq(k, 0, bp, tp).reshape(1, bp, tp, N_KV_HEADS, HEAD_DIM),
        per_seq(v, 0, bp, tp).reshape(1, bp, tp, N_KV_HEADS, HEAD_DIM),
        per_seq(ki, 0, bp, tp)[None],
        c_p[:, tp - keep:][None],
        per_seq(k, n_p, bs, ts).reshape(1, bs, ts, N_KV_HEADS, HEAD_DIM),
        per_seq(v, n_p, bs, ts).reshape(1, bs, ts, N_KV_HEADS, HEAD_DIM),
        per_seq(ki, n_p, bs, ts)[None],
        c_s[:, -keep:][None],
    )
```

```python
import functools

import jax
import jax.numpy as jnp
from jax import lax
from jax.experimental import pallas as pl
from jax.experimental.pallas import tpu as pltpu

D_MODEL = 1024
CHUNK = 64
N_HEADS = 8
N_KV_HEADS = 2
HEAD_DIM = 64
GROUP = N_HEADS // N_KV_HEADS
IDX_HEADS = 8
IDX_DIM = 64
TOPK_MAX = 256
CONV_CH = 512
CONV_WIDTH = 31
D_FF = 2816
ROPE_THETA = 10000.0
EPS = 1e-6

Q_COLS = N_HEADS * HEAD_DIM
KV_COLS = N_KV_HEADS * HEAD_DIM
IDXQ_COLS = IDX_HEADS * IDX_DIM

LANES = 128
_MXU_DTYPE = jnp.bfloat16
_VMEM_LIMIT = 56 * 1024 * 1024
_FF_CHUNK = 256
_HALO = 32
_NEG = -1e30

_C_Q = 0
_C_QI = _C_Q + Q_COLS
_C_K = _C_QI + IDXQ_COLS
_C_KW = _C_K + KV_COLS
_C_V = _C_KW + LANES
_C_CA = _C_V + KV_COLS
_C_CB = _C_CA + CONV_CH
_C_GA = _C_CB + CONV_CH
_C_GC = _C_GA + D_MODEL
_C_END = _C_GC + D_MODEL

_INT_MIN = -2147483648
_KEY_NEG_FLT_MAX = -2139095040


def _params(sem):
    return pltpu.CompilerParams(dimension_semantics=sem, vmem_limit_bytes=_VMEM_LIMIT)


def _resident(shape):
    nd = len(shape)
    return pl.BlockSpec(shape, lambda *_: (0,) * nd, pipeline_mode=pl.Buffered(1))


def _rms(x, g):
    return x * lax.rsqrt(jnp.mean(x * x, axis=-1, keepdims=True) + EPS) * g


def _dot(a, b):
    return jnp.dot(a, b, preferred_element_type=jnp.float32)


def _dot_nt(a, b):
    return lax.dot_general(a, b, (((1,), (1,)), ((), ())), preferred_element_type=jnp.float32)


def _ffn_kernel(*refs, final_norm):
    if final_norm:
        x_ref, g_ref, wa_ref, wb_ref, wo_ref, gf_ref, o_ref = refs
    else:
        x_ref, g_ref, wa_ref, wb_ref, wo_ref, o_ref = refs
    x = x_ref[...]
    u = _rms(x, g_ref[...]).astype(wa_ref.dtype)
    acc = jnp.zeros_like(x)
    for c in range(D_FF // _FF_CHUNK):
        sl = slice(c * _FF_CHUNK, (c + 1) * _FF_CHUNK)
        a = _dot(u, wa_ref[:, sl])
        b = _dot(u, wb_ref[:, sl])
        act = (a * jax.nn.sigmoid(a) * b).astype(wo_ref.dtype)
        acc = acc + _dot(act, wo_ref[sl, :])
    h = x + 0.5 * acc
    if final_norm:
        h = _rms(h, gf_ref[...])
    o_ref[...] = h


def _ffn(x, g, w_in, w_out, g_final, tm):
    n = x.shape[0]
    wa = w_in[:, :D_FF].astype(_MXU_DTYPE)
    wb = w_in[:, D_FF:].astype(_MXU_DTYPE)
    wo = w_out.astype(_MXU_DTYPE)
    row = pl.BlockSpec((tm, D_MODEL), lambda i: (i, 0))
    vec = _resident((1, D_MODEL))
    in_specs = [row, vec, _resident(wa.shape), _resident(wb.shape), _resident(wo.shape)]
    args = [x, g.reshape(1, D_MODEL), wa, wb, wo]
    if g_final is not None:
        in_specs.append(vec)
        args.append(g_final.reshape(1, D_MODEL))
    return pl.pallas_call(
        functools.partial(_ffn_kernel, final_norm=g_final is not None),
        grid=(n // tm,),
        in_specs=in_specs,
        out_specs=row,
        out_shape=jax.ShapeDtypeStruct((n, D_MODEL), jnp.float32),
        compiler_params=_params(("parallel",)),
        name="ffn_final" if g_final is not None else "ffn",
    )(*args)


def _swap_halves(x):
    lane = lax.broadcasted_iota(jnp.int32, x.shape, 1)
    first = (lane & (HEAD_DIM - 1)) < HEAD_DIM // 2
    return jnp.where(first, pltpu.roll(x, LANES - HEAD_DIM // 2, 1), pltpu.roll(x, HEAD_DIM // 2, 1))


def _rope(x, cos, sin_signed):
    tiles = []
    for j in range(x.shape[1] // LANES):
        xt = x[:, j * LANES:(j + 1) * LANES]
        tiles.append(xt * cos + _swap_halves(xt) * sin_signed)
    return tiles[0] if len(tiles) == 1 else jnp.concatenate(tiles, axis=1)


def _proj_kernel(h_ref, g_ref, w_ref, bg_ref, cos_ref, sin_ref,
                 q_ref, qi_ref, k_ref, kw_ref, v_ref, c_ref, ga_ref, gc_ref):
    u = _rms(h_ref[...], g_ref[...]).astype(w_ref.dtype)
    cos = cos_ref[...]
    sin = sin_ref[...]

    def proj(lo, hi):
        return _dot(u, w_ref[:, lo:hi])

    q_ref[...] = (_rope(proj(_C_Q, _C_QI), cos, sin) * (HEAD_DIM ** -0.5)).astype(q_ref.dtype)
    qi_ref[...] = _rope(proj(_C_QI, _C_K), cos, sin).astype(qi_ref.dtype)
    k_ref[...] = _rope(proj(_C_K, _C_KW), cos, sin)
    zkw = proj(_C_KW, _C_V)
    lane = lax.broadcasted_iota(jnp.int32, zkw.shape, 1)
    kw_ref[...] = jnp.where(lane < IDX_DIM, _rope(zkw, cos, sin), zkw * (IDX_HEADS ** -0.5))
    v_ref[...] = proj(_C_V, _C_CA)
    c_ref[...] = proj(_C_CA, _C_CB) * jax.nn.sigmoid(proj(_C_CB, _C_GA))
    ga_ref[...] = jax.nn.sigmoid(proj(_C_GA, _C_GC) + bg_ref[:, :D_MODEL])
    gc_ref[...] = jax.nn.sigmoid(proj(_C_GC, _C_END) + bg_ref[:, D_MODEL:])


def _proj(h, g, w_main, b_gate, cos, sin, tm):
    n = h.shape[0]

    def row(width):
        return pl.BlockSpec((tm, width), lambda i: (i, 0))

    f32 = jnp.float32
    out_shape = (
        jax.ShapeDtypeStruct((n, Q_COLS), _MXU_DTYPE),
        jax.ShapeDtypeStruct((n, IDXQ_COLS), _MXU_DTYPE),
        jax.ShapeDtypeStruct((n, KV_COLS), f32),
        jax.ShapeDtypeStruct((n, LANES), f32),
        jax.ShapeDtypeStruct((n, KV_COLS), f32),
        jax.ShapeDtypeStruct((n, CONV_CH), f32),
        jax.ShapeDtypeStruct((n, D_MODEL), f32),
        jax.ShapeDtypeStruct((n, D_MODEL), f32),
    )
    return pl.pallas_call(
        _proj_kernel,
        grid=(n // tm,),
        in_specs=[row(D_MODEL), _resident((1, D_MODEL)), _resident(w_main.shape),
                  _resident((1, 2 * D_MODEL)), row(LANES), row(LANES)],
        out_specs=tuple(row(s.shape[1]) for s in out_shape),
        out_shape=out_shape,
        compiler_params=_params(("parallel",)),
        name="proj",
    )(h, g.reshape(1, D_MODEL), w_main, b_gate.reshape(1, 2 * D_MODEL), cos, sin)


def _key_to_float(u):
    key = u ^ _INT_MIN
    bits = jnp.where(key >= 0, key, key ^ 0x7FFFFFFF)
    return lax.bitcast_convert_type(bits, jnp.float32)


def _count_ge(ref, x, n_kt, kt):
    cols = ref.shape[1]

    def tile(j, acc):
        blk = ref[pl.ds(pl.multiple_of(j * kt, kt), kt), :]
        ones = jnp.where(blk >= x, 1.0, 0.0)
        return acc + ones.reshape(kt // 8, 8, cols).sum(axis=0)

    def pair(jj, acc):
        return tile(2 * jj + 1, tile(2 * jj, acc))

    acc = lax.fori_loop(0, n_kt // 2, pair, jnp.zeros((8, cols), jnp.float32))
    acc = lax.cond(n_kt % 2 == 1, lambda a: tile(n_kt - 1, a), lambda a: a, acc)
    return acc.sum(axis=0, keepdims=True)


def _attn_kernel(qi_ref, wit_ref, q_ref, kidx_ref, k_ref, vt_ref, o_ref,
                 s_ref, eq_ref, bias0_ref, bias1_ref, lg0_ref, lg1_ref, acc_ref, *, past, tq, kt, top):
    t0 = pl.program_id(1) * tq
    n_kt = (past + t0 + tq + kt - 1) // kt
    lane_q = lax.broadcasted_iota(jnp.int32, (1, tq), 1)
    limit = past + ((t0 + lane_q) // CHUNK + 1) * CHUNK
    w = wit_ref[...]

    def score_rows(off, rows, last):
        kx = kidx_ref[pl.ds(off, rows), :]
        acc = jnp.zeros((rows, tq), jnp.float32)
        for h in range(IDX_HEADS):
            s = _dot_nt(kx, qi_ref[:, h * IDX_DIM:(h + 1) * IDX_DIM])
            acc = acc + w[h:h + 1, :] * jnp.maximum(s, 0.0)
        if last:
            key = off + lax.broadcasted_iota(jnp.int32, (rows, tq), 0)
            acc = jnp.where(key < limit, acc, -jnp.inf)
        s_ref[pl.ds(off, rows), :] = acc

    def score_pair(jj, carry):
        score_rows(pl.multiple_of(jj * 2 * kt, 2 * kt), 2 * kt, False)
        return carry

    n_full = n_kt - 1
    lax.fori_loop(0, n_full // 2, score_pair, 0)

    @pl.when(n_full % 2 == 1)
    def _():
        score_rows(pl.multiple_of((n_full - 1) * kt, kt), kt, False)

    score_rows(pl.multiple_of(n_full * kt, kt), kt, True)

    def count_ge(thr):
        return _count_ge(s_ref, thr, n_kt, kt)

    def descend(p, prefix):
        cand = prefix | lax.shift_left(jnp.int32(1), 31 - p)
        return jnp.where(count_ge(_key_to_float(cand)) >= top, cand, prefix)

    prefix = lax.fori_loop(0, 32, descend, jnp.zeros((1, tq), jnp.int32))
    key = jnp.maximum(prefix ^ _INT_MIN, _KEY_NEG_FLT_MAX)
    thr = lax.bitcast_convert_type(jnp.where(key >= 0, key, key ^ 0x7FFFFFFF), jnp.float32)

    idx_bits = max(1, (s_ref.shape[0] - 1).bit_length())

    def resolve_ties(cnt):
        def mark(j, carry):
            off = pl.multiple_of(j * kt, kt)
            key_idx = off + lax.broadcasted_iota(jnp.int32, (kt, tq), 0)
            eq_ref[pl.ds(off, kt), :] = jnp.where(s_ref[pl.ds(off, kt), :] == thr, key_idx, -1)
            return carry

        lax.fori_loop(0, n_kt, mark, 0)

        want = cnt - top + 1.0

        def descend_idx(p, x):
            cand = x | lax.shift_left(jnp.int32(1), idx_bits - 1 - p)
            return jnp.where(_count_ge(eq_ref, cand, n_kt, kt) >= want, cand, x)

        last = lax.fori_loop(0, idx_bits, descend_idx, jnp.zeros((1, tq), jnp.int32))

        def kill(j, carry):
            rows = pl.ds(pl.multiple_of(j * kt, kt), kt)
            s_ref[rows, :] = jnp.where(eq_ref[rows, :] > last, -jnp.inf, s_ref[rows, :])
            return carry

        lax.fori_loop(0, n_kt, kill, 0)
        return count_ge(thr)

    cnt = count_ge(thr)
    cnt = lax.cond(jnp.max(cnt) > top, resolve_ties, lambda c: c, cnt)

    def drop_lowest(cnt):
        over = cnt > top

        def tile(j):
            off = pl.multiple_of(j * kt, kt)
            return off, s_ref[pl.ds(off, kt), :]

        def min_body(j, vmin):
            _, blk = tile(j)
            cand = jnp.where(blk >= thr, blk, jnp.inf)
            return jnp.minimum(vmin, cand.reshape(kt // 8, 8, tq).min(axis=0))

        vmin = lax.fori_loop(0, n_kt, min_body, jnp.full((8, tq), jnp.inf, jnp.float32))
        vmin = vmin.min(axis=0, keepdims=True)

        def idx_body(j, imax):
            off, blk = tile(j)
            key_idx = off + lax.broadcasted_iota(jnp.int32, (kt, tq), 0)
            cand = jnp.where(blk == vmin, key_idx, -1)
            return jnp.maximum(imax, cand.reshape(kt // 8, 8, tq).max(axis=0))

        imax = lax.fori_loop(0, n_kt, idx_body, jnp.full((8, tq), -1, jnp.int32))
        kill = jnp.where(over, imax.max(axis=0, keepdims=True), -1)

        def kill_body(j, carry):
            off, blk = tile(j)
            key_idx = off + lax.broadcasted_iota(jnp.int32, (kt, tq), 0)
            s_ref[pl.ds(off, kt), :] = jnp.where(key_idx == kill, -jnp.inf, blk)
            return carry

        lax.fori_loop(0, n_kt, kill_body, 0)
        return jnp.where(over, cnt - 1.0, cnt)

    lax.while_loop(lambda cnt: jnp.max(cnt) > top, drop_lowest, cnt)

    acc_ref[...] = jnp.zeros(acc_ref.shape, jnp.float32)

    def logits_sweep(j, m, bias_buf, lg_buf):
        off = pl.multiple_of(jnp.minimum(j, n_kt - 1) * kt, kt)
        bias_buf[...] = jnp.where(s_ref[pl.ds(off, kt), :] >= jnp.where(j < n_kt, thr, jnp.inf), 0.0, _NEG)
        m_new = []
        for h in range(N_HEADS):
            c = h // GROUP
            kx = k_ref[pl.ds(off, kt), c * HEAD_DIM:(c + 1) * HEAD_DIM]
            lg = _dot_nt(kx, q_ref[:, h * HEAD_DIM:(h + 1) * HEAD_DIM]) + bias_buf[...]
            lg_buf[h] = lg
            m_new.append(jnp.maximum(m[h], lg.max(axis=0, keepdims=True)))
        return tuple(m_new)

    def softmax_sweep(j, m_old, m_new, l, lg_buf):
        off = pl.multiple_of(jnp.minimum(j, n_kt - 1) * kt, kt)
        l_new = []
        for h in range(N_HEADS):
            c = h // GROUP
            vx = vt_ref[c * HEAD_DIM:(c + 1) * HEAD_DIM, pl.ds(off, kt)]
            alpha = jnp.exp(m_old[h] - m_new[h])
            p = jnp.exp(lg_buf[h] - m_new[h])
            acc_ref[h] = alpha * acc_ref[h] + _dot(vx, p.astype(vx.dtype))
            l_new.append(alpha * l[h] + p.sum(axis=0, keepdims=True))
        return tuple(l_new)

    def attn_step(jj, carry):
        m_a, m_b, l = carry
        m_c = logits_sweep(2 * jj + 1, m_b, bias1_ref, lg1_ref)
        l = softmax_sweep(2 * jj, m_a, m_b, l, lg0_ref)
        m_d = logits_sweep(2 * jj + 2, m_c, bias0_ref, lg0_ref)
        l = softmax_sweep(2 * jj + 1, m_b, m_c, l, lg1_ref)
        return m_c, m_d, l

    m_init = tuple(jnp.full((1, tq), _NEG, jnp.float32) for _ in range(N_HEADS))
    l_init = tuple(jnp.zeros((1, tq), jnp.float32) for _ in range(N_HEADS))
    m_first = logits_sweep(0, m_init, bias0_ref, lg0_ref)
    if kt == s_ref.shape[0]:
        l = softmax_sweep(0, m_init, m_first, l_init, lg0_ref)
    else:
        _, _, l = lax.fori_loop(0, (n_kt + 1) // 2, attn_step, (m_init, m_first, l_init))
    for h in range(N_HEADS):
        o_ref[h * HEAD_DIM:(h + 1) * HEAD_DIM, :] = acc_ref[h] / l[h]


def _attn(qi, wit, q, kidx, k, vt, *, past, tq, kt):
    b, t, _ = q.shape
    lp = k.shape[1]
    n_keys = past + t
    assert t % tq == 0 and tq % CHUNK == 0 and lp % kt == 0 and lp >= n_keys
    assert t == tq or (past == 0 and kt == tq)
    top = min(TOPK_MAX, n_keys // 4)
    return pl.pallas_call(
        functools.partial(_attn_kernel, past=past, tq=tq, kt=kt, top=top),
        grid=(b, t // tq),
        in_specs=[
            pl.BlockSpec((None, tq, IDXQ_COLS), lambda bi, i: (bi, i, 0)),
            pl.BlockSpec((None, IDX_HEADS, tq), lambda bi, i: (bi, 0, i)),
            pl.BlockSpec((None, tq, Q_COLS), lambda bi, i: (bi, i, 0)),
            pl.BlockSpec((None, lp, IDX_DIM), lambda bi, i: (bi, 0, 0)),
            pl.BlockSpec((None, lp, KV_COLS), lambda bi, i: (bi, 0, 0)),
            pl.BlockSpec((None, KV_COLS, lp), lambda bi, i: (bi, 0, 0)),
        ],
        out_specs=pl.BlockSpec((None, Q_COLS, tq), lambda bi, i: (bi, 0, i)),
        out_shape=jax.ShapeDtypeStruct((b, Q_COLS, t), jnp.float32),
        scratch_shapes=[
            pltpu.VMEM((lp, tq), jnp.float32),
            pltpu.VMEM((lp, tq), jnp.int32),
            pltpu.VMEM((kt, tq), jnp.float32),
            pltpu.VMEM((kt, tq), jnp.float32),
            pltpu.VMEM((N_HEADS, kt, tq), jnp.float32),
            pltpu.VMEM((N_HEADS, kt, tq), jnp.float32),
            pltpu.VMEM((N_HEADS, HEAD_DIM, tq), jnp.float32),
        ],
        compiler_params=_params(("parallel", "arbitrary")),
        name=f"attn_tq{tq}",
    )(qi, wit, q, kidx, k, vt)


def _merge_kernel(h_ref, c_ref, halo_ref, at_ref, ga_ref, gc_ref, cw_ref, cb_ref, lng_ref, lnb_ref,
                  wco_ref, wao_ref, wo_ref, o_ref, win_ref):
    n_chunk = c_ref.shape[0] // CHUNK
    lead = _HALO - (CONV_WIDTH - 1)
    dcs = []
    for j in range(n_chunk):
        win_ref[j, 0:_HALO, :] = halo_ref[j]
        win_ref[j, _HALO:_HALO + CHUNK, :] = c_ref[j * CHUNK:(j + 1) * CHUNK, :]
        acc = jnp.zeros((CHUNK, CONV_CH), jnp.float32)
        for tap in range(CONV_WIDTH):
            acc = acc + win_ref[j, lead + tap:lead + tap + CHUNK, :] * cw_ref[tap:tap + 1, :]
        dcs.append(acc)
    dc = jnp.concatenate(dcs, axis=0) + cb_ref[...]
    mu = jnp.mean(dc, axis=-1, keepdims=True)
    var = jnp.mean(jnp.square(dc - mu), axis=-1, keepdims=True)
    y = (dc - mu) * lax.rsqrt(var + EPS) * lng_ref[...] + lnb_ref[...]
    conv_out = _dot((y * jax.nn.sigmoid(y)).astype(wco_ref.dtype), wco_ref[...])
    attn_out = _dot(at_ref[...], wao_ref[...])
    merged = ga_ref[...] * attn_out + gc_ref[...] * conv_out
    o_ref[...] = h_ref[...] + _dot(merged.astype(wo_ref.dtype), wo_ref[...])


def _merge(h, c, halo, attn, ga, gc, conv_w, conv_b, ln_g, ln_b, conv_w_out, attn_w_out, w_out, tm):
    n = h.shape[0]

    def row(width):
        return pl.BlockSpec((tm, width), lambda i: (i, 0))

    cw = jnp.zeros((_HALO, CONV_CH), jnp.float32).at[:CONV_WIDTH].set(conv_w)
    vec = _resident((1, CONV_CH))
    return pl.pallas_call(
        _merge_kernel,
        grid=(n // tm,),
        in_specs=[
            row(D_MODEL), row(CONV_CH),
            pl.BlockSpec((tm // CHUNK, _HALO, CONV_CH), lambda i: (i, 0, 0)),
            row(Q_COLS), row(D_MODEL), row(D_MODEL),
            _resident(cw.shape), vec, vec, vec,
            _resident((CONV_CH, D_MODEL)), _resident((Q_COLS, D_MODEL)), _resident((D_MODEL, D_MODEL)),
        ],
        out_specs=row(D_MODEL),
        out_shape=jax.ShapeDtypeStruct((n, D_MODEL), jnp.float32),
        scratch_shapes=[pltpu.VMEM((tm // CHUNK, _HALO + CHUNK, CONV_CH), jnp.float32)],
        compiler_params=_params(("parallel",)),
        name="merge",
    )(h, c, halo, attn, ga, gc, cw, conv_b.reshape(1, CONV_CH), ln_g.reshape(1, CONV_CH),
      ln_b.reshape(1, CONV_CH), conv_w_out.astype(_MXU_DTYPE), attn_w_out.astype(_MXU_DTYPE),
      w_out.astype(_MXU_DTYPE))


def _row_tile(n, candidates):
    for tm in candidates:
        if n % tm == 0:
            return tm
    raise ValueError(f"row count {n} is not a multiple of {candidates[-1]}")


def _rope_tables(pos):
    inv = ROPE_THETA ** (-jnp.arange(0, HEAD_DIM, 2, dtype=jnp.float32) / HEAD_DIM)
    ang = pos.astype(jnp.float32)[:, None] * inv[None, :]
    cos, sin = jnp.cos(ang), jnp.sin(ang)
    return jnp.tile(cos, (1, 4)), jnp.tile(jnp.concatenate([-sin, sin], axis=1), (1, 2))


def _main_weight(w_in):
    sizes = (Q_COLS, KV_COLS, KV_COLS, IDXQ_COLS, IDX_DIM, IDX_HEADS, 2 * CONV_CH, 2 * D_MODEL)
    offs = [0]
    for s in sizes:
        offs.append(offs[-1] + s)
    wq, wk, wv, wqi, wki, wwi, wconv, wgate = (w_in[:, offs[i]:offs[i + 1]] for i in range(len(sizes)))
    pad = jnp.zeros((D_MODEL, LANES - IDX_DIM - IDX_HEADS), w_in.dtype)
    return jnp.concatenate([wq, wqi, wk, wki, wwi, pad, wv, wconv, wgate], axis=1).astype(_MXU_DTYPE)


def kernel(x_prompt, x_sample, cache_k, cache_v, cache_idx_k, state_conv, ffn1_norm, ffn1_w_in, ffn1_w_out, mix_norm, w_in, b_gate, conv_w, conv_b, conv_ln_g, conv_ln_b, conv_w_out, attn_w_out, w_out, ffn2_norm, ffn2_w_in, ffn2_w_out, final_norm):
    assert ffn1_norm.shape[0] == 1, "one layer"
    bp, tp, _ = x_prompt.shape
    bs, ts, _ = x_sample.shape
    past = cache_k.shape[2]
    n_p, n_s = bp * tp, bs * ts
    n = n_p + n_s
    assert tp % CHUNK == 0 and ts % CHUNK == 0
    tm = _row_tile(n, (512, 256, 128, 64))
    tm_merge = _row_tile(n, (256, 128, 64))

    x = jnp.concatenate([x_prompt.reshape(n_p, D_MODEL), x_sample.reshape(n_s, D_MODEL)], axis=0)
    h = _ffn(x, ffn1_norm[0], ffn1_w_in[0], ffn1_w_out[0], None, tm)

    cos_p, sin_p = _rope_tables(jnp.arange(tp, dtype=jnp.int32))
    cos_s, sin_s = _rope_tables(past + jnp.arange(ts, dtype=jnp.int32))
    cos = jnp.concatenate([jnp.tile(cos_p, (bp, 1)), jnp.tile(cos_s, (bs, 1))], axis=0)
    sin = jnp.concatenate([jnp.tile(sin_p, (bp, 1)), jnp.tile(sin_s, (bs, 1))], axis=0)
    q, qi, k, kw, v, c, ga, gc = _proj(h, mix_norm[0], _main_weight(w_in[0]), b_gate[0], cos, sin, tm)
    ki = kw[:, :IDX_DIM]
    wi = kw[:, IDX_DIM:IDX_DIM + IDX_HEADS]

    def per_seq(a, lo, b, t):
        return a[lo:lo + b * t].reshape(b, t, a.shape[1])

    tq_p = _row_tile(tp, (256, 128, 64))
    attn_p = _attn(
        per_seq(qi, 0, bp, tp), per_seq(wi, 0, bp, tp).transpose(0, 2, 1), per_seq(q, 0, bp, tp),
        per_seq(ki, 0, bp, tp).astype(_MXU_DTYPE), per_seq(k, 0, bp, tp).astype(_MXU_DTYPE),
        per_seq(v, 0, bp, tp).transpose(0, 2, 1).astype(_MXU_DTYPE),
        past=0, tq=tq_p, kt=tq_p)

    kt_s = LANES
    n_keys = past + ts
    pad_keys = -n_keys % kt_s

    def with_cache(cache, new):
        cache = cache.reshape(bs, past, -1)
        pad = jnp.zeros((bs, pad_keys, cache.shape[2]), cache.dtype)
        return jnp.concatenate([cache, new, pad], axis=1).astype(_MXU_DTYPE)

    attn_s = _attn(
        per_seq(qi, n_p, bs, ts), per_seq(wi, n_p, bs, ts).transpose(0, 2, 1), per_seq(q, n_p, bs, ts),
        with_cache(cache_idx_k[0], per_seq(ki, n_p, bs, ts)),
        with_cache(cache_k[0], per_seq(k, n_p, bs, ts)),
        with_cache(cache_v[0], per_seq(v, n_p, bs, ts)).transpose(0, 2, 1),
        past=past, tq=ts, kt=n_keys + pad_keys)
    attn = jnp.concatenate([attn_p.transpose(0, 2, 1).reshape(n_p, Q_COLS),
                            attn_s.transpose(0, 2, 1).reshape(n_s, Q_COLS)], axis=0).astype(_MXU_DTYPE)

    tails = c.reshape(n // CHUNK, CHUNK, CONV_CH)[:, CHUNK - _HALO:, :]
    prev = jnp.concatenate([jnp.zeros((1, _HALO, CONV_CH), c.dtype), tails[:-1]], axis=0)
    state = jnp.concatenate(
        [jnp.zeros((bs, _HALO - (CONV_WIDTH - 1), CONV_CH), c.dtype), state_conv[0]], axis=1)
    halo_p = prev[:n_p // CHUNK].reshape(bp, tp // CHUNK, _HALO, CONV_CH).at[:, 0].set(0.0)
    halo_s = prev[n_p // CHUNK:].reshape(bs, ts // CHUNK, _HALO, CONV_CH).at[:, 0].set(state)
    halo = jnp.concatenate([halo_p.reshape(-1, _HALO, CONV_CH), halo_s.reshape(-1, _HALO, CONV_CH)], axis=0)

    h2 = _merge(h, c, halo, attn, ga, gc, conv_w[0], conv_b[0], conv_ln_g[0], conv_ln_b[0],
                conv_w_out[0], attn_w_out[0], w_out[0], tm_merge)
    y = _ffn(h2, ffn2_norm[0], ffn2_w_in[0], ffn2_w_out[0], final_norm, tm)

    c_p = per_seq(c, 0, bp, tp)
    c_s = jnp.concatenate([state_conv[0], per_seq(c, n_p, bs, ts)], axis=1)
    keep = CONV_WIDTH - 1
    assert tp >= keep
    return (
        y[:n_p].reshape(bp, tp, D_MODEL),
        y[n_p:].reshape(bs, ts, D_MODEL),
        per_seq(k, 0, bp, tp).reshape(1, bp, tp, N_KV_HEADS, HEAD_DIM),
        per_seq(v, 0, bp, tp).reshape(1, bp, tp, N_KV_HEADS, HEAD_DIM),
        per_seq(ki, 0, bp, tp)[None],
        c_p[:, tp - keep:][None],
        per_seq(k, n_p, bs, ts).reshape(1, bs, ts, N_KV_HEADS, HEAD_DIM),
        per_seq(v, n_p, bs, ts).reshape(1, bs, ts, N_KV_HEADS, HEAD_DIM),
        per_seq(ki, n_p, bs, ts)[None],
        c_s[:, -keep:][None],
    )
```

```python
import functools

import jax
import jax.numpy as jnp
from jax import lax
from jax.experimental import pallas as pl
from jax.experimental.pallas import tpu as pltpu

D_MODEL = 1024
CHUNK = 64
N_HEADS = 8
N_KV_HEADS = 2
HEAD_DIM = 64
GROUP = N_HEADS // N_KV_HEADS
IDX_HEADS = 8
IDX_DIM = 64
TOPK_MAX = 256
CONV_CH = 512
CONV_WIDTH = 31
D_FF = 2816
ROPE_THETA = 10000.0
EPS = 1e-6

Q_COLS = N_HEADS * HEAD_DIM
KV_COLS = N_KV_HEADS * HEAD_DIM
IDXQ_COLS = IDX_HEADS * IDX_DIM

LANES = 128
_MXU_DTYPE = jnp.bfloat16
_VMEM_LIMIT = 56 * 1024 * 1024
_FF_CHUNK = 256
_HALO = 32
_NEG = -1e30

_C_Q = 0
_C_QI = _C_Q + Q_COLS
_C_K = _C_QI + IDXQ_COLS
_C_KW = _C_K + KV_COLS
_C_V = _C_KW + LANES
_C_CA = _C_V + KV_COLS
_C_CB = _C_CA + CONV_CH
_C_GA = _C_CB + CONV_CH
_C_GC = _C_GA + D_MODEL
_C_END = _C_GC + D_MODEL

_INT_MIN = -2147483648
_KEY_NEG_FLT_MAX = -2139095040


def _params(sem):
    return pltpu.CompilerParams(dimension_semantics=sem, vmem_limit_bytes=_VMEM_LIMIT)


def _resident(shape):
    nd = len(shape)
    return pl.BlockSpec(shape, lambda *_: (0,) * nd, pipeline_mode=pl.Buffered(1))


def _rms(x, g):
    return x * lax.rsqrt(jnp.mean(x * x, axis=-1, keepdims=True) + EPS) * g


def _dot(a, b):
    return jnp.dot(a, b, preferred_element_type=jnp.float32)


def _dot_nt(a, b):
    return lax.dot_general(a, b, (((1,), (1,)), ((), ())), preferred_element_type=jnp.float32)


def _ffn_kernel(*refs, final_norm):
    if final_norm:
        x_ref, g_ref, wa_ref, wb_ref, wo_ref, gf_ref, o_ref = refs
    else:
        x_ref, g_ref, wa_ref, wb_ref, wo_ref, o_ref = refs
    x = x_ref[...]
    u = _rms(x, g_ref[...]).astype(wa_ref.dtype)
    acc = jnp.zeros_like(x)
    for c in range(D_FF // _FF_CHUNK):
        sl = slice(c * _FF_CHUNK, (c + 1) * _FF_CHUNK)
        a = _dot(u, wa_ref[:, sl])
        b = _dot(u, wb_ref[:, sl])
        act = (a * jax.nn.sigmoid(a) * b).astype(wo_ref.dtype)
        acc = acc + _dot(act, wo_ref[sl, :])
    h = x + 0.5 * acc
    if final_norm:
        h = _rms(h, gf_ref[...])
    o_ref[...] = h


def _ffn_weights(w_in, w_out):
    return w_in[:, :D_FF].astype(_MXU_DTYPE), w_in[:, D_FF:].astype(_MXU_DTYPE), w_out.astype(_MXU_DTYPE)


def _ffn(x, g, weights, g_final, tm):
    n = x.shape[0]
    wa, wb, wo = weights
    row = pl.BlockSpec((tm, D_MODEL), lambda i: (i, 0))
    vec = _resident((1, D_MODEL))
    in_specs = [row, vec, _resident(wa.shape), _resident(wb.shape), _resident(wo.shape)]
    args = [x, g.reshape(1, D_MODEL), wa, wb, wo]
    if g_final is not None:
        in_specs.append(vec)
        args.append(g_final.reshape(1, D_MODEL))
    return pl.pallas_call(
        functools.partial(_ffn_kernel, final_norm=g_final is not None),
        grid=(n // tm,),
        in_specs=in_specs,
        out_specs=row,
        out_shape=jax.ShapeDtypeStruct((n, D_MODEL), jnp.float32),
        compiler_params=_params(("parallel",)),
        name=f"ffn_final_n{n}" if g_final is not None else f"ffn_n{n}",
    )(*args)


def _swap_halves(x):
    lane = lax.broadcasted_iota(jnp.int32, x.shape, 1)
    first = (lane & (HEAD_DIM - 1)) < HEAD_DIM // 2
    return jnp.where(first, pltpu.roll(x, LANES - HEAD_DIM // 2, 1), pltpu.roll(x, HEAD_DIM // 2, 1))


def _rope(x, cos, sin_signed):
    tiles = []
    for j in range(x.shape[1] // LANES):
        xt = x[:, j * LANES:(j + 1) * LANES]
        tiles.append(xt * cos + _swap_halves(xt) * sin_signed)
    return tiles[0] if len(tiles) == 1 else jnp.concatenate(tiles, axis=1)


def _proj_kernel(h_ref, g_ref, w_ref, bg_ref, cos_ref, sin_ref,
                 q_ref, qi_ref, k_ref, ki_ref, v_ref, c_ref, ga_ref, gc_ref,
                 kb_ref, kib_ref, vt_ref, wit_ref):
    u = _rms(h_ref[...], g_ref[...]).astype(w_ref.dtype)
    cos = cos_ref[...]
    sin = sin_ref[...]

    def proj(lo, hi):
        return _dot(u, w_ref[:, lo:hi])

    q_ref[...] = (_rope(proj(_C_Q, _C_QI), cos, sin) * (HEAD_DIM ** -0.5)).astype(q_ref.dtype)
    qi_ref[...] = _rope(proj(_C_QI, _C_K), cos, sin).astype(qi_ref.dtype)
    k = _rope(proj(_C_K, _C_KW), cos, sin)
    k_ref[...] = k
    kb_ref[...] = k.astype(kb_ref.dtype)
    zkw = proj(_C_KW, _C_V)
    ki = _rope(zkw, cos, sin)[:, :IDX_DIM]
    ki_ref[...] = ki
    kib_ref[...] = ki.astype(kib_ref.dtype)
    wit_ref[...] = (zkw * (IDX_HEADS ** -0.5)).T[IDX_DIM:IDX_DIM + IDX_HEADS, :]
    v = proj(_C_V, _C_CA)
    v_ref[...] = v
    vt_ref[...] = v.T.astype(vt_ref.dtype)
    c_ref[...] = proj(_C_CA, _C_CB) * jax.nn.sigmoid(proj(_C_CB, _C_GA))
    ga_ref[...] = jax.nn.sigmoid(proj(_C_GA, _C_GC) + bg_ref[:, :D_MODEL])
    gc_ref[...] = jax.nn.sigmoid(proj(_C_GC, _C_END) + bg_ref[:, D_MODEL:])


def _proj(h, g, w_main, b_gate, cos, sin, tm):
    n = h.shape[0]
    n_tab = cos.shape[0] // tm

    def row(width):
        return pl.BlockSpec((tm, width), lambda i: (i, 0))

    def col(height):
        return pl.BlockSpec((height, tm), lambda i: (0, i))

    table = pl.BlockSpec((tm, LANES), lambda i: (i % n_tab, 0))
    f32 = jnp.float32
    out_shape = (
        jax.ShapeDtypeStruct((n, Q_COLS), _MXU_DTYPE),
        jax.ShapeDtypeStruct((n, IDXQ_COLS), _MXU_DTYPE),
        jax.ShapeDtypeStruct((n, KV_COLS), f32),
        jax.ShapeDtypeStruct((n, IDX_DIM), f32),
        jax.ShapeDtypeStruct((n, KV_COLS), f32),
        jax.ShapeDtypeStruct((n, CONV_CH), f32),
        jax.ShapeDtypeStruct((n, D_MODEL), f32),
        jax.ShapeDtypeStruct((n, D_MODEL), f32),
        jax.ShapeDtypeStruct((n, KV_COLS), _MXU_DTYPE),
        jax.ShapeDtypeStruct((n, IDX_DIM), _MXU_DTYPE),
        jax.ShapeDtypeStruct((KV_COLS, n), _MXU_DTYPE),
        jax.ShapeDtypeStruct((IDX_HEADS, n), f32),
    )
    out_specs = tuple(row(s.shape[1]) for s in out_shape[:10]) + (col(KV_COLS), col(IDX_HEADS))
    return pl.pallas_call(
        _proj_kernel,
        grid=(n // tm,),
        in_specs=[row(D_MODEL), _resident((1, D_MODEL)), _resident(w_main.shape),
                  _resident((1, 2 * D_MODEL)), table, table],
        out_specs=out_specs,
        out_shape=out_shape,
        compiler_params=_params(("parallel",)),
        name=f"proj_n{n}",
    )(h, g.reshape(1, D_MODEL), w_main, b_gate.reshape(1, 2 * D_MODEL), cos, sin)


def _key_to_float(u):
    key = u ^ _INT_MIN
    bits = jnp.where(key >= 0, key, key ^ 0x7FFFFFFF)
    return lax.bitcast_convert_type(bits, jnp.float32)


def _count_ge(ref, x, n_kt, kt):
    cols = ref.shape[1]

    def tile(j, acc):
        blk = ref[pl.ds(pl.multiple_of(j * kt, kt), kt), :]
        ones = jnp.where(blk >= x, 1.0, 0.0)
        return acc + ones.reshape(kt // 8, 8, cols).sum(axis=0)

    def pair(jj, acc):
        return tile(2 * jj + 1, tile(2 * jj, acc))

    acc = lax.fori_loop(0, n_kt // 2, pair, jnp.zeros((8, cols), jnp.float32))
    acc = lax.cond(n_kt % 2 == 1, lambda a: tile(n_kt - 1, a), lambda a: a, acc)
    return acc.sum(axis=0, keepdims=True)


def _attn_kernel(qi_ref, wit_ref, q_ref, kidx_ref, k_ref, vt_ref, o_ref,
                 s_ref, eq_ref, bias0_ref, bias1_ref, lg0_ref, lg1_ref, acc_ref, *, past, tq, kt, top):
    t0 = pl.program_id(1) * tq
    n_kt = (past + t0 + tq + kt - 1) // kt
    lane_q = lax.broadcasted_iota(jnp.int32, (1, tq), 1)
    limit = past + ((t0 + lane_q) // CHUNK + 1) * CHUNK
    w = wit_ref[...]

    def score_rows(off, rows, last):
        kx = kidx_ref[pl.ds(off, rows), :]
        acc = jnp.zeros((rows, tq), jnp.float32)
        for h in range(IDX_HEADS):
            s = _dot_nt(kx, qi_ref[:, h * IDX_DIM:(h + 1) * IDX_DIM])
            acc = acc + w[h:h + 1, :] * jnp.maximum(s, 0.0)
        if last:
            key = off + lax.broadcasted_iota(jnp.int32, (rows, tq), 0)
            acc = jnp.where(key < limit, acc, -jnp.inf)
        s_ref[pl.ds(off, rows), :] = acc

    def score_pair(jj, carry):
        score_rows(pl.multiple_of(jj * 2 * kt, 2 * kt), 2 * kt, False)
        return carry

    n_full = n_kt - 1
    lax.fori_loop(0, n_full // 2, score_pair, 0)

    @pl.when(n_full % 2 == 1)
    def _():
        score_rows(pl.multiple_of((n_full - 1) * kt, kt), kt, False)

    score_rows(pl.multiple_of(n_full * kt, kt), kt, True)

    def count_ge(thr):
        return _count_ge(s_ref, thr, n_kt, kt)

    def descend(p, prefix):
        cand = prefix | lax.shift_left(jnp.int32(1), 31 - p)
        return jnp.where(count_ge(_key_to_float(cand)) >= top, cand, prefix)

    prefix = lax.fori_loop(0, 32, descend, jnp.zeros((1, tq), jnp.int32))
    key = jnp.maximum(prefix ^ _INT_MIN, _KEY_NEG_FLT_MAX)
    thr = lax.bitcast_convert_type(jnp.where(key >= 0, key, key ^ 0x7FFFFFFF), jnp.float32)

    idx_bits = max(1, (s_ref.shape[0] - 1).bit_length())

    def resolve_ties(cnt):
        def mark(j, carry):
            off = pl.multiple_of(j * kt, kt)
            key_idx = off + lax.broadcasted_iota(jnp.int32, (kt, tq), 0)
            eq_ref[pl.ds(off, kt), :] = jnp.where(s_ref[pl.ds(off, kt), :] == thr, key_idx, -1)
            return carry

        lax.fori_loop(0, n_kt, mark, 0)

        want = cnt - top + 1.0

        def descend_idx(p, x):
            cand = x | lax.shift_left(jnp.int32(1), idx_bits - 1 - p)
            return jnp.where(_count_ge(eq_ref, cand, n_kt, kt) >= want, cand, x)

        last = lax.fori_loop(0, idx_bits, descend_idx, jnp.zeros((1, tq), jnp.int32))

        def kill(j, carry):
            rows = pl.ds(pl.multiple_of(j * kt, kt), kt)
            s_ref[rows, :] = jnp.where(eq_ref[rows, :] > last, -jnp.inf, s_ref[rows, :])
            return carry

        lax.fori_loop(0, n_kt, kill, 0)
        return count_ge(thr)

    cnt = count_ge(thr)
    cnt = lax.cond(jnp.max(cnt) > top, resolve_ties, lambda c: c, cnt)

    def drop_lowest(cnt):
        over = cnt > top

        def tile(j):
            off = pl.multiple_of(j * kt, kt)
            return off, s_ref[pl.ds(off, kt), :]

        def min_body(j, vmin):
            _, blk = tile(j)
            cand = jnp.where(blk >= thr, blk, jnp.inf)
            return jnp.minimum(vmin, cand.reshape(kt // 8, 8, tq).min(axis=0))

        vmin = lax.fori_loop(0, n_kt, min_body, jnp.full((8, tq), jnp.inf, jnp.float32))
        vmin = vmin.min(axis=0, keepdims=True)

        def idx_body(j, imax):
            off, blk = tile(j)
            key_idx = off + lax.broadcasted_iota(jnp.int32, (kt, tq), 0)
            cand = jnp.where(blk == vmin, key_idx, -1)
            return jnp.maximum(imax, cand.reshape(kt // 8, 8, tq).max(axis=0))

        imax = lax.fori_loop(0, n_kt, idx_body, jnp.full((8, tq), -1, jnp.int32))
        kill = jnp.where(over, imax.max(axis=0, keepdims=True), -1)

        def kill_body(j, carry):
            off, blk = tile(j)
            key_idx = off + lax.broadcasted_iota(jnp.int32, (kt, tq), 0)
            s_ref[pl.ds(off, kt), :] = jnp.where(key_idx == kill, -jnp.inf, blk)
            return carry

        lax.fori_loop(0, n_kt, kill_body, 0)
        return jnp.where(over, cnt - 1.0, cnt)

    lax.while_loop(lambda cnt: jnp.max(cnt) > top, drop_lowest, cnt)

    acc_ref[...] = jnp.zeros(acc_ref.shape, jnp.float32)

    def logits_sweep(j, m, bias_buf, lg_buf):
        off = pl.multiple_of(jnp.minimum(j, n_kt - 1) * kt, kt)
        bias_buf[...] = jnp.where(s_ref[pl.ds(off, kt), :] >= jnp.where(j < n_kt, thr, jnp.inf), 0.0, _NEG)
        m_new = []
        for h in range(N_HEADS):
            c = h // GROUP
            kx = k_ref[pl.ds(off, kt), c * HEAD_DIM:(c + 1) * HEAD_DIM]
            lg = _dot_nt(kx, q_ref[:, h * HEAD_DIM:(h + 1) * HEAD_DIM]) + bias_buf[...]
            lg_buf[h] = lg
            m_new.append(jnp.maximum(m[h], lg.max(axis=0, keepdims=True)))
        return tuple(m_new)

    def softmax_sweep(j, m_old, m_new, l, lg_buf):
        off = pl.multiple_of(jnp.minimum(j, n_kt - 1) * kt, kt)
        l_new = []
        for h in range(N_HEADS):
            c = h // GROUP
            vx = vt_ref[c * HEAD_DIM:(c + 1) * HEAD_DIM, pl.ds(off, kt)]
            alpha = jnp.exp(m_old[h] - m_new[h])
            p = jnp.exp(lg_buf[h] - m_new[h])
            acc_ref[h] = alpha * acc_ref[h] + _dot(vx, p.astype(vx.dtype))
            l_new.append(alpha * l[h] + p.sum(axis=0, keepdims=True))
        return tuple(l_new)

    def attn_step(jj, carry):
        m_a, m_b, l = carry
        m_c = logits_sweep(2 * jj + 1, m_b, bias1_ref, lg1_ref)
        l = softmax_sweep(2 * jj, m_a, m_b, l, lg0_ref)
        m_d = logits_sweep(2 * jj + 2, m_c, bias0_ref, lg0_ref)
        l = softmax_sweep(2 * jj + 1, m_b, m_c, l, lg1_ref)
        return m_c, m_d, l

    m_init = tuple(jnp.full((1, tq), _NEG, jnp.float32) for _ in range(N_HEADS))
    l_init = tuple(jnp.zeros((1, tq), jnp.float32) for _ in range(N_HEADS))
    m_first = logits_sweep(0, m_init, bias0_ref, lg0_ref)
    if kt == s_ref.shape[0]:
        l = softmax_sweep(0, m_init, m_first, l_init, lg0_ref)
    else:
        _, _, l = lax.fori_loop(0, (n_kt + 1) // 2, attn_step, (m_init, m_first, l_init))
    if o_ref.shape[0] == tq:
        o = jnp.concatenate([acc_ref[h] / l[h] for h in range(N_HEADS)], axis=0)
        o_ref[...] = o.T.astype(o_ref.dtype)
    else:
        for h in range(N_HEADS):
            o_ref[h * HEAD_DIM:(h + 1) * HEAD_DIM, :] = (acc_ref[h] / l[h]).astype(o_ref.dtype)


def _attn(qi, wit, q, kidx, k, vt, *, batch, past, tq, kt):
    t = q.shape[0] // batch
    nq = t // tq
    if k.ndim == 2:
        lp = t
        key_specs = [
            pl.BlockSpec((lp, IDX_DIM), lambda bi, i: (bi, 0)),
            pl.BlockSpec((lp, KV_COLS), lambda bi, i: (bi, 0)),
            pl.BlockSpec((KV_COLS, lp), lambda bi, i: (0, bi)),
        ]
    else:
        lp = k.shape[1]
        key_specs = [
            pl.BlockSpec((None, lp, IDX_DIM), lambda bi, i: (bi, 0, 0)),
            pl.BlockSpec((None, lp, KV_COLS), lambda bi, i: (bi, 0, 0)),
            pl.BlockSpec((None, KV_COLS, lp), lambda bi, i: (bi, 0, 0)),
        ]
    n_keys = past + t
    assert t % tq == 0 and tq % CHUNK == 0 and lp % kt == 0 and lp >= n_keys
    assert t == tq or (past == 0 and kt == tq)
    top = min(TOPK_MAX, n_keys // 4)
    transpose_in_kernel = tq % LANES == 0
    if transpose_in_kernel:
        out_shape = jax.ShapeDtypeStruct((batch * t, Q_COLS), _MXU_DTYPE)
        out_spec = pl.BlockSpec((tq, Q_COLS), lambda bi, i: (bi * nq + i, 0))
    else:
        out_shape = jax.ShapeDtypeStruct((batch, Q_COLS, t), _MXU_DTYPE)
        out_spec = pl.BlockSpec((None, Q_COLS, tq), lambda bi, i: (bi, 0, i))
    if tq % LANES == 0:
        wit_spec = pl.BlockSpec((IDX_HEADS, tq), lambda bi, i: (0, bi * nq + i))
    else:
        wit = wit.reshape(IDX_HEADS, batch, t).transpose(1, 0, 2)
        wit_spec = pl.BlockSpec((None, IDX_HEADS, tq), lambda bi, i: (bi, 0, i))
    out = pl.pallas_call(
        functools.partial(_attn_kernel, past=past, tq=tq, kt=kt, top=top),
        grid=(batch, nq),
        in_specs=[
            pl.BlockSpec((tq, IDXQ_COLS), lambda bi, i: (bi * nq + i, 0)),
            wit_spec,
            pl.BlockSpec((tq, Q_COLS), lambda bi, i: (bi * nq + i, 0)),
        ] + key_specs,
        out_specs=out_spec,
        out_shape=out_shape,
        scratch_shapes=[
            pltpu.VMEM((lp, tq), jnp.float32),
            pltpu.VMEM((lp, tq), jnp.int32),
            pltpu.VMEM((kt, tq), jnp.float32),
            pltpu.VMEM((kt, tq), jnp.float32),
            pltpu.VMEM((N_HEADS, kt, tq), jnp.float32),
            pltpu.VMEM((N_HEADS, kt, tq), jnp.float32),
            pltpu.VMEM((N_HEADS, HEAD_DIM, tq), jnp.float32),
        ],
        compiler_params=_params(("parallel", "arbitrary")),
        name=f"attn_tq{tq}",
    )(qi, wit, q, kidx, k, vt)
    if transpose_in_kernel:
        return out
    return out.transpose(0, 2, 1).reshape(batch * t, Q_COLS)


def _merge_kernel(h_ref, c_ref, halo_ref, at_ref, ga_ref, gc_ref, cw_ref, cb_ref, lng_ref, lnb_ref,
                  wco_ref, wao_ref, wo_ref, o_ref, win_ref, *, seq_len):
    tm = c_ref.shape[0]
    n_chunk = tm // CHUNK
    lead = _HALO - (CONV_WIDTH - 1)
    dcs = []
    for j in range(n_chunk):
        if seq_len == CHUNK:
            history = halo_ref[j]
        elif j == 0:
            starts_sequence = (pl.program_id(0) * tm) % seq_len == 0
            history = jnp.where(starts_sequence, 0.0, halo_ref[...])
        else:
            history = c_ref[j * CHUNK - _HALO:j * CHUNK, :]
        win_ref[j, 0:_HALO, :] = history
        win_ref[j, _HALO:_HALO + CHUNK, :] = c_ref[j * CHUNK:(j + 1) * CHUNK, :]
        acc = jnp.zeros((CHUNK, CONV_CH), jnp.float32)
        for tap in range(CONV_WIDTH):
            acc = acc + win_ref[j, lead + tap:lead + tap + CHUNK, :] * cw_ref[tap:tap + 1, :]
        dcs.append(acc)
    dc = jnp.concatenate(dcs, axis=0) + cb_ref[...]
    mu = jnp.mean(dc, axis=-1, keepdims=True)
    var = jnp.mean(jnp.square(dc - mu), axis=-1, keepdims=True)
    y = (dc - mu) * lax.rsqrt(var + EPS) * lng_ref[...] + lnb_ref[...]
    conv_out = _dot((y * jax.nn.sigmoid(y)).astype(wco_ref.dtype), wco_ref[...])
    attn_out = _dot(at_ref[...], wao_ref[...])
    merged = ga_ref[...] * attn_out + gc_ref[...] * conv_out
    o_ref[...] = h_ref[...] + _dot(merged.astype(wo_ref.dtype), wo_ref[...])


def _merge(h, c, state, attn, ga, gc, weights, tm, seq_len):
    n = h.shape[0]
    cw, conv_b, ln_g, ln_b, wco, wao, wo = weights

    def row(width):
        return pl.BlockSpec((tm, width), lambda i: (i, 0))

    if seq_len == CHUNK:
        halo, halo_spec = state, pl.BlockSpec((tm // CHUNK, _HALO, CONV_CH), lambda i: (i, 0, 0))
    else:
        assert state is None and seq_len % tm == 0
        per_tile = tm // _HALO
        halo, halo_spec = c, pl.BlockSpec((_HALO, CONV_CH), lambda i: (jnp.maximum(i * per_tile - 1, 0), 0))
    vec = _resident((1, CONV_CH))
    return pl.pallas_call(
        functools.partial(_merge_kernel, seq_len=seq_len),
        grid=(n // tm,),
        in_specs=[
            row(D_MODEL), row(CONV_CH), halo_spec,
            row(Q_COLS), row(D_MODEL), row(D_MODEL),
            _resident(cw.shape), vec, vec, vec,
            _resident((CONV_CH, D_MODEL)), _resident((Q_COLS, D_MODEL)), _resident((D_MODEL, D_MODEL)),
        ],
        out_specs=row(D_MODEL),
        out_shape=jax.ShapeDtypeStruct((n, D_MODEL), jnp.float32),
        scratch_shapes=[pltpu.VMEM((tm // CHUNK, _HALO + CHUNK, CONV_CH), jnp.float32)],
        compiler_params=_params(("parallel",)),
        name=f"merge_n{n}",
    )(h, c, halo, attn, ga, gc, cw, conv_b, ln_g, ln_b, wco, wao, wo)


def _row_tile(n, candidates):
    for tm in candidates:
        if n % tm == 0:
            return tm
    raise ValueError(f"row count {n} is not a multiple of {candidates[-1]}")


def _rope_tables(pos, rows):
    inv = ROPE_THETA ** (-jnp.arange(0, HEAD_DIM, 2, dtype=jnp.float32) / HEAD_DIM)
    ang = pos.astype(jnp.float32)[:, None] * inv[None, :]
    cos, sin = jnp.cos(ang), jnp.sin(ang)
    reps = rows // pos.shape[0]
    return (jnp.tile(cos, (reps, 4)), jnp.tile(jnp.concatenate([-sin, sin], axis=1), (reps, 2)))


def _main_weight(w_in):
    sizes = (Q_COLS, KV_COLS, KV_COLS, IDXQ_COLS, IDX_DIM, IDX_HEADS, 2 * CONV_CH, 2 * D_MODEL)
    offs = [0]
    for s in sizes:
        offs.append(offs[-1] + s)
    wq, wk, wv, wqi, wki, wwi, wconv, wgate = (w_in[:, offs[i]:offs[i + 1]] for i in range(len(sizes)))
    pad = jnp.zeros((D_MODEL, LANES - IDX_DIM - IDX_HEADS), w_in.dtype)
    return jnp.concatenate([wq, wqi, wk, wki, wwi, pad, wv, wconv, wgate], axis=1).astype(_MXU_DTYPE)


def _layer(x, caches, weights, final_norm):
    (ffn1_norm, ffn1_w, mix_norm, w_main, b_gate, merge_w, ffn2_norm, ffn2_w) = weights
    b, t, _ = x.shape
    n = b * t
    assert t % CHUNK == 0
    tm = _row_tile(n, (512, 256, 128, 64))
    tm_merge = _row_tile(n, (256, 128, 64))
    past = 0 if caches is None else caches[0].shape[1]

    h = _ffn(x.reshape(n, D_MODEL), ffn1_norm, ffn1_w, None, tm)
    if t >= tm:
        assert t % tm == 0
    else:
        assert tm % t == 0
    cos, sin = _rope_tables(past + jnp.arange(t, dtype=jnp.int32), max(t, tm))
    q, qi, k, ki, v, c, ga, gc, kb, kib, vt, wit = _proj(h, mix_norm, w_main, b_gate, cos, sin, tm)

    if caches is None:
        tq = _row_tile(t, (256, 128, 64))
        attn = _attn(qi, wit, q, kib, kb, vt, batch=b, past=0, tq=tq, kt=tq)
        state = None
        conv_tail = c.reshape(b, t, CONV_CH)[:, t - (CONV_WIDTH - 1):]
    else:
        cache_k, cache_v, cache_idx_k, state_conv = caches
        n_keys = past + t
        pad_keys = -n_keys % LANES

        def with_cache(cache, new):
            cache = cache.reshape(b, past, -1).astype(_MXU_DTYPE)
            pad = jnp.zeros((b, pad_keys, cache.shape[2]), _MXU_DTYPE)
            return jnp.concatenate([cache, new.reshape(b, t, -1), pad], axis=1)

        vt_all = with_cache(cache_v, vt.reshape(KV_COLS, b, t).transpose(1, 2, 0)).transpose(0, 2, 1)
        attn = _attn(qi, wit, q, with_cache(cache_idx_k, kib), with_cache(cache_k, kb), vt_all,
                     batch=b, past=past, tq=t, kt=n_keys + pad_keys)
        lead = jnp.zeros((b, _HALO - (CONV_WIDTH - 1), CONV_CH), jnp.float32)
        state = jnp.concatenate([lead, state_conv], axis=1)
        conv_tail = jnp.concatenate([state_conv, c.reshape(b, t, CONV_CH)], axis=1)[:, -(CONV_WIDTH - 1):]
    if t != CHUNK:
        assert state is None, "a carried conv state is supported for 64-row sequences only"

    h2 = _merge(h, c, state, attn, ga, gc, merge_w, tm_merge, t)
    y = _ffn(h2, ffn2_norm, ffn2_w, final_norm, tm)
    return (y.reshape(b, t, D_MODEL),
            k.reshape(1, b, t, N_KV_HEADS, HEAD_DIM), v.reshape(1, b, t, N_KV_HEADS, HEAD_DIM),
            ki.reshape(1, b, t, IDX_DIM), conv_tail[None])


def kernel(x_prompt, x_sample, cache_k, cache_v, cache_idx_k, state_conv, ffn1_norm, ffn1_w_in, ffn1_w_out, mix_norm, w_in, b_gate, conv_w, conv_b, conv_ln_g, conv_ln_b, conv_w_out, attn_w_out, w_out, ffn2_norm, ffn2_w_in, ffn2_w_out, final_norm):
    assert ffn1_norm.shape[0] == 1, "one layer"
    assert x_prompt.shape[1] >= CONV_WIDTH - 1
    conv_taps = jnp.zeros((_HALO, CONV_CH), jnp.float32).at[:CONV_WIDTH].set(conv_w[0])
    merge_w = (conv_taps, conv_b[0].reshape(1, CONV_CH), conv_ln_g[0].reshape(1, CONV_CH),
               conv_ln_b[0].reshape(1, CONV_CH), conv_w_out[0].astype(_MXU_DTYPE),
               attn_w_out[0].astype(_MXU_DTYPE), w_out[0].astype(_MXU_DTYPE))
    weights = (ffn1_norm[0], _ffn_weights(ffn1_w_in[0], ffn1_w_out[0]), mix_norm[0],
               _main_weight(w_in[0]), b_gate[0], merge_w,
               ffn2_norm[0], _ffn_weights(ffn2_w_in[0], ffn2_w_out[0]))
    y_p, k_p, v_p, ki_p, conv_p = _layer(x_prompt, None, weights, final_norm)
    y_s, k_s, v_s, ki_s, conv_s = _layer(
        x_sample, (cache_k[0], cache_v[0], cache_idx_k[0], state_conv[0]), weights, final_norm)
    return (y_p, y_s, k_p, v_p, ki_p, conv_p, k_s, v_s, ki_s, conv_s)
```

```python
import functools

import jax
import jax.numpy as jnp
from jax import lax
from jax.experimental import pallas as pl
from jax.experimental.pallas import tpu as pltpu

D_MODEL = 1024
CHUNK = 64
N_HEADS = 8
N_KV_HEADS = 2
HEAD_DIM = 64
GROUP = N_HEADS // N_KV_HEADS
IDX_HEADS = 8
IDX_DIM = 64
TOPK_MAX = 256
CONV_CH = 512
CONV_WIDTH = 31
D_FF = 2816
ROPE_THETA = 10000.0
EPS = 1e-6

Q_COLS = N_HEADS * HEAD_DIM
KV_COLS = N_KV_HEADS * HEAD_DIM
IDXQ_COLS = IDX_HEADS * IDX_DIM

LANES = 128
_MXU_DTYPE = jnp.bfloat16
_VMEM_LIMIT = 56 * 1024 * 1024
_FF_CHUNK = 256
_HALO = 32
_NEG = -1e30

_C_Q = 0
_C_QI = _C_Q + Q_COLS
_C_K = _C_QI + IDXQ_COLS
_C_KW = _C_K + KV_COLS
_C_V = _C_KW + LANES
_C_CA = _C_V + KV_COLS
_C_CB = _C_CA + CONV_CH
_C_GA = _C_CB + CONV_CH
_C_GC = _C_GA + D_MODEL
_C_END = _C_GC + D_MODEL

_KEY_NEG_FLT_MAX = -2139095040
_KEY_POS_INF = 0x7F800000
_KEY_MIN_NORMAL = 0x00800000
_MAX_SEARCH_PASSES = 96


def _params(sem):
    return pltpu.CompilerParams(dimension_semantics=sem, vmem_limit_bytes=_VMEM_LIMIT)


def _resident(shape):
    nd = len(shape)
    return pl.BlockSpec(shape, lambda *_: (0,) * nd, pipeline_mode=pl.Buffered(1))


def _rms(x, g):
    return x * lax.rsqrt(jnp.mean(x * x, axis=-1, keepdims=True) + EPS) * g


def _dot(a, b):
    return jnp.dot(a, b, preferred_element_type=jnp.float32)


def _dot_nt(a, b):
    return lax.dot_general(a, b, (((1,), (1,)), ((), ())), preferred_element_type=jnp.float32)


def _ffn_kernel(*refs, final_norm):
    if final_norm:
        x_ref, g_ref, wa_ref, wb_ref, wo_ref, gf_ref, o_ref = refs
    else:
        x_ref, g_ref, wa_ref, wb_ref, wo_ref, o_ref = refs
    x = x_ref[...]
    u = _rms(x, g_ref[...]).astype(wa_ref.dtype)
    acc = jnp.zeros_like(x)
    for c in range(D_FF // _FF_CHUNK):
        sl = slice(c * _FF_CHUNK, (c + 1) * _FF_CHUNK)
        a = _dot(u, wa_ref[:, sl])
        b = _dot(u, wb_ref[:, sl])
        act = (a * jax.nn.sigmoid(a) * b).astype(wo_ref.dtype)
        acc = acc + _dot(act, wo_ref[sl, :])
    h = x + 0.5 * acc
    if final_norm:
        h = _rms(h, gf_ref[...])
    o_ref[...] = h


def _ffn_weights(w_in, w_out):
    return w_in[:, :D_FF].astype(_MXU_DTYPE), w_in[:, D_FF:].astype(_MXU_DTYPE), w_out.astype(_MXU_DTYPE)


def _ffn(x, g, weights, g_final, tm):
    n = x.shape[0]
    wa, wb, wo = weights
    row = pl.BlockSpec((tm, D_MODEL), lambda i: (i, 0))
    vec = _resident((1, D_MODEL))
    in_specs = [row, vec, _resident(wa.shape), _resident(wb.shape), _resident(wo.shape)]
    args = [x, g.reshape(1, D_MODEL), wa, wb, wo]
    if g_final is not None:
        in_specs.append(vec)
        args.append(g_final.reshape(1, D_MODEL))
    return pl.pallas_call(
        functools.partial(_ffn_kernel, final_norm=g_final is not None),
        grid=(n // tm,),
        in_specs=in_specs,
        out_specs=row,
        out_shape=jax.ShapeDtypeStruct((n, D_MODEL), jnp.float32),
        compiler_params=_params(("parallel",)),
        name=f"ffn_final_n{n}" if g_final is not None else f"ffn_n{n}",
    )(*args)


def _swap_halves(x):
    lane = lax.broadcasted_iota(jnp.int32, x.shape, 1)
    first = (lane & (HEAD_DIM - 1)) < HEAD_DIM // 2
    return jnp.where(first, pltpu.roll(x, LANES - HEAD_DIM // 2, 1), pltpu.roll(x, HEAD_DIM // 2, 1))


def _rope(x, cos, sin_signed):
    tiles = []
    for j in range(x.shape[1] // LANES):
        xt = x[:, j * LANES:(j + 1) * LANES]
        tiles.append(xt * cos + _swap_halves(xt) * sin_signed)
    return tiles[0] if len(tiles) == 1 else jnp.concatenate(tiles, axis=1)


def _proj_kernel(h_ref, g_ref, w_ref, bg_ref, cos_ref, sin_ref,
                 q_ref, qi_ref, k_ref, ki_ref, v_ref, c_ref, ga_ref, gc_ref,
                 kb_ref, kib_ref, vt_ref, wit_ref):
    u = _rms(h_ref[...], g_ref[...]).astype(w_ref.dtype)
    cos = cos_ref[...]
    sin = sin_ref[...]

    def proj(lo, hi):
        return _dot(u, w_ref[:, lo:hi])

    q_ref[...] = (_rope(proj(_C_Q, _C_QI), cos, sin) * (HEAD_DIM ** -0.5)).astype(q_ref.dtype)
    qi_ref[...] = _rope(proj(_C_QI, _C_K), cos, sin).astype(qi_ref.dtype)
    k = _rope(proj(_C_K, _C_KW), cos, sin)
    k_ref[...] = k
    kb_ref[...] = k.astype(kb_ref.dtype)
    zkw = proj(_C_KW, _C_V)
    ki = _rope(zkw, cos, sin)[:, :IDX_DIM]
    ki_ref[...] = ki
    kib_ref[...] = ki.astype(kib_ref.dtype)
    wit_ref[...] = (zkw * (IDX_HEADS ** -0.5)).T[IDX_DIM:IDX_DIM + IDX_HEADS, :]
    v = proj(_C_V, _C_CA)
    v_ref[...] = v
    vt_ref[...] = v.T.astype(vt_ref.dtype)
    c_ref[...] = proj(_C_CA, _C_CB) * jax.nn.sigmoid(proj(_C_CB, _C_GA))
    ga_ref[...] = jax.nn.sigmoid(proj(_C_GA, _C_GC) + bg_ref[:, :D_MODEL])
    gc_ref[...] = jax.nn.sigmoid(proj(_C_GC, _C_END) + bg_ref[:, D_MODEL:])


def _proj(h, g, w_main, b_gate, cos, sin, tm):
    n = h.shape[0]
    n_tab = cos.shape[0] // tm

    def row(width):
        return pl.BlockSpec((tm, width), lambda i: (i, 0))

    def col(height):
        return pl.BlockSpec((height, tm), lambda i: (0, i))

    table = pl.BlockSpec((tm, LANES), lambda i: (i % n_tab, 0))
    f32 = jnp.float32
    out_shape = (
        jax.ShapeDtypeStruct((n, Q_COLS), _MXU_DTYPE),
        jax.ShapeDtypeStruct((n, IDXQ_COLS), _MXU_DTYPE),
        jax.ShapeDtypeStruct((n, KV_COLS), f32),
        jax.ShapeDtypeStruct((n, IDX_DIM), f32),
        jax.ShapeDtypeStruct((n, KV_COLS), f32),
        jax.ShapeDtypeStruct((n, CONV_CH), f32),
        jax.ShapeDtypeStruct((n, D_MODEL), f32),
        jax.ShapeDtypeStruct((n, D_MODEL), f32),
        jax.ShapeDtypeStruct((n, KV_COLS), _MXU_DTYPE),
        jax.ShapeDtypeStruct((n, IDX_DIM), _MXU_DTYPE),
        jax.ShapeDtypeStruct((KV_COLS, n), _MXU_DTYPE),
        jax.ShapeDtypeStruct((IDX_HEADS, n), f32),
    )
    out_specs = tuple(row(s.shape[1]) for s in out_shape[:10]) + (col(KV_COLS), col(IDX_HEADS))
    return pl.pallas_call(
        _proj_kernel,
        grid=(n // tm,),
        in_specs=[row(D_MODEL), _resident((1, D_MODEL)), _resident(w_main.shape),
                  _resident((1, 2 * D_MODEL)), table, table],
        out_specs=out_specs,
        out_shape=out_shape,
        compiler_params=_params(("parallel",)),
        name=f"proj_n{n}",
    )(h, g.reshape(1, D_MODEL), w_main, b_gate.reshape(1, 2 * D_MODEL), cos, sin)


def _float_to_key(x):
    bits = lax.bitcast_convert_type(x, jnp.int32)
    return jnp.where(bits >= 0, bits, bits ^ 0x7FFFFFFF)


def _key_to_float(key):
    return lax.bitcast_convert_type(jnp.where(key >= 0, key, key ^ 0x7FFFFFFF), jnp.float32)


def _normal_quantile(p):
    tail = jnp.minimum(p, 1.0 - p)
    t = jnp.sqrt(-2.0 * jnp.log(tail))
    z = t - (2.515517 + t * (0.802853 + t * 0.010328)) / (1.0 + t * (1.432788 + t * (0.189269 + t * 0.001308)))
    return jnp.where(p < 0.5, z, -z)


def _kth_largest(s_ref, count_ge, n_valid, top, kt):
    cols = s_ref.shape[1]
    shape = (1, cols)
    head = s_ref[0:min(kt, 256), :]
    mean = head.mean(axis=0, keepdims=True)
    std = jnp.sqrt(jnp.maximum((head * head).mean(axis=0, keepdims=True) - mean * mean, 1e-30))
    guess = mean + std * _normal_quantile(jnp.minimum((top - 0.5) / jnp.maximum(n_valid, 1.0), 0.999))
    log_top = jnp.log(top - 0.5)

    def adjacent(lo_k, hi_k):
        return (hi_k <= lo_k + 1) | ((lo_k == 0) & (hi_k == _KEY_MIN_NORMAL))

    def active(state):
        _, lo_k, hi_k, clo = state[:4]
        return (clo > top) & jnp.logical_not(adjacent(lo_k, hi_k))

    def cond(state):
        return (state[0] < _MAX_SEARCH_PASSES) & (jnp.max(jnp.where(active(state), 1.0, 0.0)) > 0.0)

    def body(state):
        it, lo_k, hi_k, clo, chi, lo_real, hi_real, run, last_up = state
        lo_v, hi_v = _key_to_float(lo_k), _key_to_float(hi_k)
        both = (lo_real > 0) & (hi_real > 0)
        log_lo = jnp.log(clo)
        frac = (log_lo - log_top) / (log_lo - jnp.log(jnp.maximum(chi, 0.5)))
        frac = jnp.where(clo - chi > 16.0, frac, 0.5)
        step = std * 0.5 * lax.shift_left(jnp.int32(1), jnp.minimum(run, 20)).astype(jnp.float32)
        t = jnp.where(both, lo_v + (hi_v - lo_v) * frac, jnp.where(lo_real > 0, lo_v + step, hi_v - step))
        t = jnp.where(it == 0, guess, t)
        t_k = _float_to_key(t)
        mid_k = (lo_k >> 1) + (hi_k >> 1) + (lo_k & hi_k & 1)
        bisect = (both & (run >= 2)) | (it >= 24)
        t_k = jnp.where(bisect, mid_k, t_k)
        t_k = jnp.where((lo_k < 0) & (hi_k > _KEY_MIN_NORMAL) & (it >= 3), 0, t_k)
        t_k = jnp.where((lo_k == 0) & (hi_k > _KEY_MIN_NORMAL), _KEY_MIN_NORMAL, t_k)
        t_k = jnp.minimum(jnp.maximum(t_k, lo_k + 1), hi_k - 1)
        c = count_ge(_key_to_float(t_k))
        live = active(state)
        up = c >= top
        go_lo, go_hi = live & up, live & jnp.logical_not(up)
        same = jnp.where(up, 1, -1) == last_up
        return (it + 1,
                jnp.where(go_lo, t_k, lo_k), jnp.where(go_hi, t_k, hi_k),
                jnp.where(go_lo, c, clo), jnp.where(go_hi, c, chi),
                jnp.where(go_lo, 1, lo_real), jnp.where(go_hi, 1, hi_real),
                jnp.where(bisect, 0, jnp.where(same, run + 1, 1)),
                jnp.where(up, 1, -1))

    zeros = jnp.zeros(shape, jnp.int32)
    init = (jnp.int32(0), jnp.full(shape, _KEY_NEG_FLT_MAX, jnp.int32), jnp.full(shape, _KEY_POS_INF, jnp.int32),
            n_valid, jnp.zeros(shape, jnp.float32), zeros, zeros, zeros, zeros)
    out = lax.while_loop(cond, body, init)
    return _key_to_float(out[1]), out[3]


def _count_ge(ref, x, n_kt, kt):
    cols = ref.shape[1]

    def tile(j, acc):
        blk = ref[pl.ds(pl.multiple_of(j * kt, kt), kt), :]
        ones = jnp.where(blk >= x, 1.0, 0.0)
        return acc + ones.reshape(kt // 8, 8, cols).sum(axis=0)

    def pair(jj, acc):
        return tile(2 * jj + 1, tile(2 * jj, acc))

    acc = lax.fori_loop(0, n_kt // 2, pair, jnp.zeros((8, cols), jnp.float32))
    acc = lax.cond(n_kt % 2 == 1, lambda a: tile(n_kt - 1, a), lambda a: a, acc)
    return acc.sum(axis=0, keepdims=True)


def _attn_kernel(qi_ref, wit_ref, q_ref, kidx_ref, k_ref, vt_ref, o_ref,
                 s_ref, eq_ref, bias0_ref, bias1_ref, lg0_ref, lg1_ref, acc_ref, *, past, tq, kt, top):
    t0 = pl.program_id(1) * tq
    n_kt = (past + t0 + tq + kt - 1) // kt
    lane_q = lax.broadcasted_iota(jnp.int32, (1, tq), 1)
    limit = past + ((t0 + lane_q) // CHUNK + 1) * CHUNK
    w = wit_ref[...]

    def score_rows(off, rows, last):
        kx = kidx_ref[pl.ds(off, rows), :]
        acc = jnp.zeros((rows, tq), jnp.float32)
        for h in range(IDX_HEADS):
            s = _dot_nt(kx, qi_ref[:, h * IDX_DIM:(h + 1) * IDX_DIM])
            acc = acc + w[h:h + 1, :] * jnp.maximum(s, 0.0)
        if last:
            key = off + lax.broadcasted_iota(jnp.int32, (rows, tq), 0)
            acc = jnp.where(key < limit, acc, -jnp.inf)
        s_ref[pl.ds(off, rows), :] = acc

    def score_pair(jj, carry):
        score_rows(pl.multiple_of(jj * 2 * kt, 2 * kt), 2 * kt, False)
        return carry

    n_full = n_kt - 1
    lax.fori_loop(0, n_full // 2, score_pair, 0)

    @pl.when(n_full % 2 == 1)
    def _():
        score_rows(pl.multiple_of((n_full - 1) * kt, kt), kt, False)

    score_rows(pl.multiple_of(n_full * kt, kt), kt, True)

    def count_ge(thr):
        return _count_ge(s_ref, thr, n_kt, kt)

    thr, cnt = _kth_largest(s_ref, count_ge, limit.astype(jnp.float32), top, kt)

    idx_bits = max(1, (s_ref.shape[0] - 1).bit_length())

    def resolve_ties(cnt):
        def mark(j, carry):
            off = pl.multiple_of(j * kt, kt)
            key_idx = off + lax.broadcasted_iota(jnp.int32, (kt, tq), 0)
            eq_ref[pl.ds(off, kt), :] = jnp.where(s_ref[pl.ds(off, kt), :] == thr, key_idx, -1)
            return carry

        lax.fori_loop(0, n_kt, mark, 0)

        want = cnt - top + 1.0

        def descend_idx(p, x):
            cand = x | lax.shift_left(jnp.int32(1), idx_bits - 1 - p)
            return jnp.where(_count_ge(eq_ref, cand, n_kt, kt) >= want, cand, x)

        last = lax.fori_loop(0, idx_bits, descend_idx, jnp.zeros((1, tq), jnp.int32))

        def kill(j, carry):
            rows = pl.ds(pl.multiple_of(j * kt, kt), kt)
            s_ref[rows, :] = jnp.where(eq_ref[rows, :] > last, -jnp.inf, s_ref[rows, :])
            return carry

        lax.fori_loop(0, n_kt, kill, 0)
        return count_ge(thr)

    cnt = lax.cond(jnp.max(cnt) > top, resolve_ties, lambda c: c, cnt)

    def drop_lowest(cnt):
        over = cnt > top

        def tile(j):
            off = pl.multiple_of(j * kt, kt)
            return off, s_ref[pl.ds(off, kt), :]

        def min_body(j, vmin):
            _, blk = tile(j)
            cand = jnp.where(blk >= thr, blk, jnp.inf)
            return jnp.minimum(vmin, cand.reshape(kt // 8, 8, tq).min(axis=0))

        vmin = lax.fori_loop(0, n_kt, min_body, jnp.full((8, tq), jnp.inf, jnp.float32))
        vmin = vmin.min(axis=0, keepdims=True)

        def idx_body(j, imax):
            off, blk = tile(j)
            key_idx = off + lax.broadcasted_iota(jnp.int32, (kt, tq), 0)
            cand = jnp.where(blk == vmin, key_idx, -1)
            return jnp.maximum(imax, cand.reshape(kt // 8, 8, tq).max(axis=0))

        imax = lax.fori_loop(0, n_kt, idx_body, jnp.full((8, tq), -1, jnp.int32))
        kill = jnp.where(over, imax.max(axis=0, keepdims=True), -1)

        def kill_body(j, carry):
            off, blk = tile(j)
            key_idx = off + lax.broadcasted_iota(jnp.int32, (kt, tq), 0)
            s_ref[pl.ds(off, kt), :] = jnp.where(key_idx == kill, -jnp.inf, blk)
            return carry

        lax.fori_loop(0, n_kt, kill_body, 0)
        return jnp.where(over, cnt - 1.0, cnt)

    lax.while_loop(lambda cnt: jnp.max(cnt) > top, drop_lowest, cnt)

    acc_ref[...] = jnp.zeros(acc_ref.shape, jnp.float32)

    def logits_sweep(j, m, bias_buf, lg_buf):
        off = pl.multiple_of(jnp.minimum(j, n_kt - 1) * kt, kt)
        bias_buf[...] = jnp.where(s_ref[pl.ds(off, kt), :] >= jnp.where(j < n_kt, thr, jnp.inf), 0.0, _NEG)
        m_new = []
        for h in range(N_HEADS):
            c = h // GROUP
            kx = k_ref[pl.ds(off, kt), c * HEAD_DIM:(c + 1) * HEAD_DIM]
            lg = _dot_nt(kx, q_ref[:, h * HEAD_DIM:(h + 1) * HEAD_DIM]) + bias_buf[...]
            lg_buf[h] = lg
            m_new.append(jnp.maximum(m[h], lg.max(axis=0, keepdims=True)))
        return tuple(m_new)

    def softmax_sweep(j, m_old, m_new, l, lg_buf):
        off = pl.multiple_of(jnp.minimum(j, n_kt - 1) * kt, kt)
        l_new = []
        for h in range(N_HEADS):
            c = h // GROUP
            vx = vt_ref[c * HEAD_DIM:(c + 1) * HEAD_DIM, pl.ds(off, kt)]
            alpha = jnp.exp(m_old[h] - m_new[h])
            p = jnp.exp(lg_buf[h] - m_new[h])
            acc_ref[h] = alpha * acc_ref[h] + _dot(vx, p.astype(vx.dtype))
            l_new.append(alpha * l[h] + p.sum(axis=0, keepdims=True))
        return tuple(l_new)

    def attn_step(jj, carry):
        m_a, m_b, l = carry
        m_c = logits_sweep(2 * jj + 1, m_b, bias1_ref, lg1_ref)
        l = softmax_sweep(2 * jj, m_a, m_b, l, lg0_ref)
        m_d = logits_sweep(2 * jj + 2, m_c, bias0_ref, lg0_ref)
        l = softmax_sweep(2 * jj + 1, m_b, m_c, l, lg1_ref)
        return m_c, m_d, l

    m_init = tuple(jnp.full((1, tq), _NEG, jnp.float32) for _ in range(N_HEADS))
    l_init = tuple(jnp.zeros((1, tq), jnp.float32) for _ in range(N_HEADS))
    m_first = logits_sweep(0, m_init, bias0_ref, lg0_ref)
    if kt == s_ref.shape[0]:
        l = softmax_sweep(0, m_init, m_first, l_init, lg0_ref)
    else:
        _, _, l = lax.fori_loop(0, (n_kt + 1) // 2, attn_step, (m_init, m_first, l_init))
    if o_ref.shape[0] == tq:
        o = jnp.concatenate([acc_ref[h] / l[h] for h in range(N_HEADS)], axis=0)
        o_ref[...] = o.T.astype(o_ref.dtype)
    else:
        for h in range(N_HEADS):
            o_ref[h * HEAD_DIM:(h + 1) * HEAD_DIM, :] = (acc_ref[h] / l[h]).astype(o_ref.dtype)


def _attn(qi, wit, q, kidx, k, vt, *, batch, past, tq, kt):
    t = q.shape[0] // batch
    nq = t // tq
    if k.ndim == 2:
        lp = t
        key_specs = [
            pl.BlockSpec((lp, IDX_DIM), lambda bi, i: (bi, 0)),
            pl.BlockSpec((lp, KV_COLS), lambda bi, i: (bi, 0)),
            pl.BlockSpec((KV_COLS, lp), lambda bi, i: (0, bi)),
        ]
    else:
        lp = k.shape[1]
        key_specs = [
            pl.BlockSpec((None, lp, IDX_DIM), lambda bi, i: (bi, 0, 0)),
            pl.BlockSpec((None, lp, KV_COLS), lambda bi, i: (bi, 0, 0)),
            pl.BlockSpec((None, KV_COLS, lp), lambda bi, i: (bi, 0, 0)),
        ]
    n_keys = past + t
    assert t % tq == 0 and tq % CHUNK == 0 and lp % kt == 0 and lp >= n_keys
    assert t == tq or (past == 0 and kt == tq)
    top = min(TOPK_MAX, n_keys // 4)
    transpose_in_kernel = tq % LANES == 0
    if transpose_in_kernel:
        out_shape = jax.ShapeDtypeStruct((batch * t, Q_COLS), _MXU_DTYPE)
        out_spec = pl.BlockSpec((tq, Q_COLS), lambda bi, i: (bi * nq + i, 0))
    else:
        out_shape = jax.ShapeDtypeStruct((batch, Q_COLS, t), _MXU_DTYPE)
        out_spec = pl.BlockSpec((None, Q_COLS, tq), lambda bi, i: (bi, 0, i))
    if tq % LANES == 0:
        wit_spec = pl.BlockSpec((IDX_HEADS, tq), lambda bi, i: (0, bi * nq + i))
    else:
        wit = wit.reshape(IDX_HEADS, batch, t).transpose(1, 0, 2)
        wit_spec = pl.BlockSpec((None, IDX_HEADS, tq), lambda bi, i: (bi, 0, i))
    out = pl.pallas_call(
        functools.partial(_attn_kernel, past=past, tq=tq, kt=kt, top=top),
        grid=(batch, nq),
        in_specs=[
            pl.BlockSpec((tq, IDXQ_COLS), lambda bi, i: (bi * nq + i, 0)),
            wit_spec,
            pl.BlockSpec((tq, Q_COLS), lambda bi, i: (bi * nq + i, 0)),
        ] + key_specs,
        out_specs=out_spec,
        out_shape=out_shape,
        scratch_shapes=[
            pltpu.VMEM((lp, tq), jnp.float32),
            pltpu.VMEM((lp, tq), jnp.int32),
            pltpu.VMEM((kt, tq), jnp.float32),
            pltpu.VMEM((kt, tq), jnp.float32),
            pltpu.VMEM((N_HEADS, kt, tq), jnp.float32),
            pltpu.VMEM((N_HEADS, kt, tq), jnp.float32),
            pltpu.VMEM((N_HEADS, HEAD_DIM, tq), jnp.float32),
        ],
        compiler_params=_params(("parallel", "arbitrary")),
        name=f"attn_tq{tq}",
    )(qi, wit, q, kidx, k, vt)
    if transpose_in_kernel:
        return out
    return out.transpose(0, 2, 1).reshape(batch * t, Q_COLS)


def _merge_kernel(h_ref, c_ref, halo_ref, at_ref, ga_ref, gc_ref, cw_ref, cb_ref, lng_ref, lnb_ref,
                  wco_ref, wao_ref, wo_ref, o_ref, win_ref, *, seq_len):
    tm = c_ref.shape[0]
    n_chunk = tm // CHUNK
    lead = _HALO - (CONV_WIDTH - 1)
    dcs = []
    for j in range(n_chunk):
        if seq_len == CHUNK:
            history = halo_ref[j]
        elif j == 0:
            starts_sequence = (pl.program_id(0) * tm) % seq_len == 0
            history = jnp.where(starts_sequence, 0.0, halo_ref[...])
        else:
            history = c_ref[j * CHUNK - _HALO:j * CHUNK, :]
        win_ref[j, 0:_HALO, :] = history
        win_ref[j, _HALO:_HALO + CHUNK, :] = c_ref[j * CHUNK:(j + 1) * CHUNK, :]
        acc = jnp.zeros((CHUNK, CONV_CH), jnp.float32)
        for tap in range(CONV_WIDTH):
            acc = acc + win_ref[j, lead + tap:lead + tap + CHUNK, :] * cw_ref[tap:tap + 1, :]
        dcs.append(acc)
    dc = jnp.concatenate(dcs, axis=0) + cb_ref[...]
    mu = jnp.mean(dc, axis=-1, keepdims=True)
    var = jnp.mean(jnp.square(dc - mu), axis=-1, keepdims=True)
    y = (dc - mu) * lax.rsqrt(var + EPS) * lng_ref[...] + lnb_ref[...]
    conv_out = _dot((y * jax.nn.sigmoid(y)).astype(wco_ref.dtype), wco_ref[...])
    attn_out = _dot(at_ref[...], wao_ref[...])
    merged = ga_ref[...] * attn_out + gc_ref[...] * conv_out
    o_ref[...] = h_ref[...] + _dot(merged.astype(wo_ref.dtype), wo_ref[...])


def _merge(h, c, state, attn, ga, gc, weights, tm, seq_len):
    n = h.shape[0]
    cw, conv_b, ln_g, ln_b, wco, wao, wo = weights

    def row(width):
        return pl.BlockSpec((tm, width), lambda i: (i, 0))

    if seq_len == CHUNK:
        halo, halo_spec = state, pl.BlockSpec((tm // CHUNK, _HALO, CONV_CH), lambda i: (i, 0, 0))
    else:
        assert state is None and seq_len % tm == 0
        per_tile = tm // _HALO
        halo, halo_spec = c, pl.BlockSpec((_HALO, CONV_CH), lambda i: (jnp.maximum(i * per_tile - 1, 0), 0))
    vec = _resident((1, CONV_CH))
    return pl.pallas_call(
        functools.partial(_merge_kernel, seq_len=seq_len),
        grid=(n // tm,),
        in_specs=[
            row(D_MODEL), row(CONV_CH), halo_spec,
            row(Q_COLS), row(D_MODEL), row(D_MODEL),
            _resident(cw.shape), vec, vec, vec,
            _resident((CONV_CH, D_MODEL)), _resident((Q_COLS, D_MODEL)), _resident((D_MODEL, D_MODEL)),
        ],
        out_specs=row(D_MODEL),
        out_shape=jax.ShapeDtypeStruct((n, D_MODEL), jnp.float32),
        scratch_shapes=[pltpu.VMEM((tm // CHUNK, _HALO + CHUNK, CONV_CH), jnp.float32)],
        compiler_params=_params(("parallel",)),
        name=f"merge_n{n}",
    )(h, c, halo, attn, ga, gc, cw, conv_b, ln_g, ln_b, wco, wao, wo)


def _row_tile(n, candidates):
    for tm in candidates:
        if n % tm == 0:
            return tm
    raise ValueError(f"row count {n} is not a multiple of {candidates[-1]}")


def _rope_tables(pos, rows):
    inv = ROPE_THETA ** (-jnp.arange(0, HEAD_DIM, 2, dtype=jnp.float32) / HEAD_DIM)
    ang = pos.astype(jnp.float32)[:, None] * inv[None, :]
    cos, sin = jnp.cos(ang), jnp.sin(ang)
    reps = rows // pos.shape[0]
    return (jnp.tile(cos, (reps, 4)), jnp.tile(jnp.concatenate([-sin, sin], axis=1), (reps, 2)))


def _main_weight(w_in):
    sizes = (Q_COLS, KV_COLS, KV_COLS, IDXQ_COLS, IDX_DIM, IDX_HEADS, 2 * CONV_CH, 2 * D_MODEL)
    offs = [0]
    for s in sizes:
        offs.append(offs[-1] + s)
    wq, wk, wv, wqi, wki, wwi, wconv, wgate = (w_in[:, offs[i]:offs[i + 1]] for i in range(len(sizes)))
    pad = jnp.zeros((D_MODEL, LANES - IDX_DIM - IDX_HEADS), w_in.dtype)
    return jnp.concatenate([wq, wqi, wk, wki, wwi, pad, wv, wconv, wgate], axis=1).astype(_MXU_DTYPE)


def _layer(x, caches, weights, final_norm):
    (ffn1_norm, ffn1_w, mix_norm, w_main, b_gate, merge_w, ffn2_norm, ffn2_w) = weights
    b, t, _ = x.shape
    n = b * t
    assert t % CHUNK == 0
    tm = _row_tile(n, (512, 256, 128, 64))
    tm_merge = _row_tile(n, (256, 128, 64))
    past = 0 if caches is None else caches[0].shape[1]

    h = _ffn(x.reshape(n, D_MODEL), ffn1_norm, ffn1_w, None, tm)
    if t >= tm:
        assert t % tm == 0
    else:
        assert tm % t == 0
    cos, sin = _rope_tables(past + jnp.arange(t, dtype=jnp.int32), max(t, tm))
    q, qi, k, ki, v, c, ga, gc, kb, kib, vt, wit = _proj(h, mix_norm, w_main, b_gate, cos, sin, tm)

    if caches is None:
        tq = _row_tile(t, (256, 128, 64))
        attn = _attn(qi, wit, q, kib, kb, vt, batch=b, past=0, tq=tq, kt=tq)
        state = None
        conv_tail = c.reshape(b, t, CONV_CH)[:, t - (CONV_WIDTH - 1):]
    else:
        cache_k, cache_v, cache_idx_k, state_conv = caches
        n_keys = past + t
        pad_keys = -n_keys % LANES

        def with_cache(cache, new):
            cache = cache.reshape(b, past, -1).astype(_MXU_DTYPE)
            pad = jnp.zeros((b, pad_keys, cache.shape[2]), _MXU_DTYPE)
            return jnp.concatenate([cache, new.reshape(b, t, -1), pad], axis=1)

        vt_all = with_cache(cache_v, vt.reshape(KV_COLS, b, t).transpose(1, 2, 0)).transpose(0, 2, 1)
        attn = _attn(qi, wit, q, with_cache(cache_idx_k, kib), with_cache(cache_k, kb), vt_all,
                     batch=b, past=past, tq=t, kt=n_keys + pad_keys)
        lead = jnp.zeros((b, _HALO - (CONV_WIDTH - 1), CONV_CH), jnp.float32)
        state = jnp.concatenate([lead, state_conv], axis=1)
        conv_tail = jnp.concatenate([state_conv, c.reshape(b, t, CONV_CH)], axis=1)[:, -(CONV_WIDTH - 1):]
    if t != CHUNK:
        assert state is None, "a carried conv state is supported for 64-row sequences only"

    h2 = _merge(h, c, state, attn, ga, gc, merge_w, tm_merge, t)
    y = _ffn(h2, ffn2_norm, ffn2_w, final_norm, tm)
    return (y.reshape(b, t, D_MODEL),
            k.reshape(1, b, t, N_KV_HEADS, HEAD_DIM), v.reshape(1, b, t, N_KV_HEADS, HEAD_DIM),
            ki.reshape(1, b, t, IDX_DIM), conv_tail[None])


def kernel(x_prompt, x_sample, cache_k, cache_v, cache_idx_k, state_conv, ffn1_norm, ffn1_w_in, ffn1_w_out, mix_norm, w_in, b_gate, conv_w, conv_b, conv_ln_g, conv_ln_b, conv_w_out, attn_w_out, w_out, ffn2_norm, ffn2_w_in, ffn2_w_out, final_norm):
    assert ffn1_norm.shape[0] == 1, "one layer"
    assert x_prompt.shape[1] >= CONV_WIDTH - 1
    conv_taps = jnp.zeros((_HALO, CONV_CH), jnp.float32).at[:CONV_WIDTH].set(conv_w[0])
    merge_w = (conv_taps, conv_b[0].reshape(1, CONV_CH), conv_ln_g[0].reshape(1, CONV_CH),
               conv_ln_b[0].reshape(1, CONV_CH), conv_w_out[0].astype(_MXU_DTYPE),
               attn_w_out[0].astype(_MXU_DTYPE), w_out[0].astype(_MXU_DTYPE))
    weights = (ffn1_norm[0], _ffn_weights(ffn1_w_in[0], ffn1_w_out[0]), mix_norm[0],
               _main_weight(w_in[0]), b_gate[0], merge_w,
               ffn2_norm[0], _ffn_weights(ffn2_w_in[0], ffn2_w_out[0]))
    y_p, k_p, v_p, ki_p, conv_p = _layer(x_prompt, None, weights, final_norm)
    y_s, k_s, v_s, ki_s, conv_s = _layer(
        x_sample, (cache_k[0], cache_v[0], cache_idx_k[0], state_conv[0]), weights, final_norm)
    return (y_p, y_s, k_p, v_p, ki_p, conv_p, k_s, v_s, ki_s, conv_s)
```

```python
import functools

import jax
import jax.numpy as jnp
from jax import lax
from jax.experimental import pallas as pl
from jax.experimental.pallas import tpu as pltpu

D_MODEL = 1024
CHUNK = 64
N_HEADS = 8
N_KV_HEADS = 2
HEAD_DIM = 64
GROUP = N_HEADS // N_KV_HEADS
IDX_HEADS = 8
IDX_DIM = 64
TOPK_MAX = 256
CONV_CH = 512
CONV_WIDTH = 31
D_FF = 2816
ROPE_THETA = 10000.0
EPS = 1e-6

Q_COLS = N_HEADS * HEAD_DIM
KV_COLS = N_KV_HEADS * HEAD_DIM
IDXQ_COLS = IDX_HEADS * IDX_DIM

LANES = 128
SUBLANES = 8
_MXU_DTYPE = jnp.bfloat16
_VMEM_LIMIT = 56 * 1024 * 1024
_FF_CHUNK = 256
_HALO = 32
_NEG = -1e30

_C_Q = 0
_C_QI = _C_Q + Q_COLS
_C_K = _C_QI + IDXQ_COLS
_C_KW = _C_K + KV_COLS
_C_V = _C_KW + LANES
_C_CA = _C_V + KV_COLS
_C_CB = _C_CA + CONV_CH
_C_GA = _C_CB + CONV_CH
_C_GC = _C_GA + D_MODEL
_C_END = _C_GC + D_MODEL

_KEY_NEG_FLT_MAX = -2139095040
_KEY_POS_INF = 0x7F800000
_KEY_MIN_NORMAL = 0x00800000
_MAX_SEARCH_PASSES = 96


def _params(sem):
    return pltpu.CompilerParams(dimension_semantics=sem, vmem_limit_bytes=_VMEM_LIMIT)


def _resident(shape):
    nd = len(shape)
    return pl.BlockSpec(shape, lambda *_: (0,) * nd, pipeline_mode=pl.Buffered(1))


def _rms(x, g):
    return x * lax.rsqrt(jnp.mean(x * x, axis=-1, keepdims=True) + EPS) * g


def _dot(a, b):
    return jnp.dot(a, b, preferred_element_type=jnp.float32)


def _dot_nt(a, b):
    return lax.dot_general(a, b, (((1,), (1,)), ((), ())), preferred_element_type=jnp.float32)


def _ffn_kernel(*refs, final_norm):
    if final_norm:
        x_ref, g_ref, wa_ref, wb_ref, wo_ref, gf_ref, o_ref = refs
    else:
        x_ref, g_ref, wa_ref, wb_ref, wo_ref, o_ref = refs
    x = x_ref[...]
    u = _rms(x, g_ref[...]).astype(wa_ref.dtype)
    acc = jnp.zeros_like(x)
    for c in range(D_FF // _FF_CHUNK):
        sl = slice(c * _FF_CHUNK, (c + 1) * _FF_CHUNK)
        a = _dot(u, wa_ref[:, sl])
        b = _dot(u, wb_ref[:, sl])
        act = (a * jax.nn.sigmoid(a) * b).astype(wo_ref.dtype)
        acc = acc + _dot(act, wo_ref[sl, :])
    h = x + 0.5 * acc
    if final_norm:
        h = _rms(h, gf_ref[...])
    o_ref[...] = h


def _ffn_weights(w_in, w_out):
    return w_in[:, :D_FF].astype(_MXU_DTYPE), w_in[:, D_FF:].astype(_MXU_DTYPE), w_out.astype(_MXU_DTYPE)


def _ffn(x, g, weights, g_final, tm):
    n = x.shape[0]
    wa, wb, wo = weights
    row = pl.BlockSpec((tm, D_MODEL), lambda i: (i, 0))
    vec = _resident((1, D_MODEL))
    in_specs = [row, vec, _resident(wa.shape), _resident(wb.shape), _resident(wo.shape)]
    args = [x, g.reshape(1, D_MODEL), wa, wb, wo]
    if g_final is not None:
        in_specs.append(vec)
        args.append(g_final.reshape(1, D_MODEL))
    return pl.pallas_call(
        functools.partial(_ffn_kernel, final_norm=g_final is not None),
        grid=(n // tm,),
        in_specs=in_specs,
        out_specs=row,
        out_shape=jax.ShapeDtypeStruct((n, D_MODEL), jnp.float32),
        compiler_params=_params(("parallel",)),
        name=f"ffn_final_n{n}" if g_final is not None else f"ffn_n{n}",
    )(*args)


def _swap_halves(x):
    lane = lax.broadcasted_iota(jnp.int32, x.shape, 1)
    first = (lane & (HEAD_DIM - 1)) < HEAD_DIM // 2
    return jnp.where(first, pltpu.roll(x, LANES - HEAD_DIM // 2, 1), pltpu.roll(x, HEAD_DIM // 2, 1))


def _rope(x, cos, sin_signed):
    tiles = []
    for j in range(x.shape[1] // LANES):
        xt = x[:, j * LANES:(j + 1) * LANES]
        tiles.append(xt * cos + _swap_halves(xt) * sin_signed)
    return tiles[0] if len(tiles) == 1 else jnp.concatenate(tiles, axis=1)


def _proj_kernel(h_ref, g_ref, w_ref, bg_ref, cos_ref, sin_ref,
                 q_ref, qi_ref, k_ref, ki_ref, v_ref, c_ref, ga_ref, gc_ref,
                 kb_ref, kib_ref, vt_ref, wit_ref):
    u = _rms(h_ref[...], g_ref[...]).astype(w_ref.dtype)
    cos = cos_ref[...]
    sin = sin_ref[...]

    def proj(lo, hi):
        return _dot(u, w_ref[:, lo:hi])

    q_ref[...] = (_rope(proj(_C_Q, _C_QI), cos, sin) * (HEAD_DIM ** -0.5)).astype(q_ref.dtype)
    qi_ref[...] = _rope(proj(_C_QI, _C_K), cos, sin).astype(qi_ref.dtype)
    k = _rope(proj(_C_K, _C_KW), cos, sin)
    k_ref[...] = k
    kb_ref[...] = k.astype(kb_ref.dtype)
    zkw = proj(_C_KW, _C_V)
    ki = _rope(zkw, cos, sin)[:, :IDX_DIM]
    ki_ref[...] = ki
    kib_ref[...] = ki.astype(kib_ref.dtype)
    wit_ref[...] = (zkw * (IDX_HEADS ** -0.5)).T[IDX_DIM:IDX_DIM + IDX_HEADS, :]
    v = proj(_C_V, _C_CA)
    v_ref[...] = v
    vt_ref[...] = v.T.astype(vt_ref.dtype)
    c_ref[...] = proj(_C_CA, _C_CB) * jax.nn.sigmoid(proj(_C_CB, _C_GA))
    ga_ref[...] = jax.nn.sigmoid(proj(_C_GA, _C_GC) + bg_ref[:, :D_MODEL])
    gc_ref[...] = jax.nn.sigmoid(proj(_C_GC, _C_END) + bg_ref[:, D_MODEL:])


def _proj(h, g, w_main, b_gate, cos, sin, tm):
    n = h.shape[0]
    n_tab = cos.shape[0] // tm

    def row(width):
        return pl.BlockSpec((tm, width), lambda i: (i, 0))

    def col(height):
        return pl.BlockSpec((height, tm), lambda i: (0, i))

    table = pl.BlockSpec((tm, LANES), lambda i: (i % n_tab, 0))
    f32 = jnp.float32
    out_shape = (
        jax.ShapeDtypeStruct((n, Q_COLS), _MXU_DTYPE),
        jax.ShapeDtypeStruct((n, IDXQ_COLS), _MXU_DTYPE),
        jax.ShapeDtypeStruct((n, KV_COLS), f32),
        jax.ShapeDtypeStruct((n, IDX_DIM), f32),
        jax.ShapeDtypeStruct((n, KV_COLS), f32),
        jax.ShapeDtypeStruct((n, CONV_CH), f32),
        jax.ShapeDtypeStruct((n, D_MODEL), f32),
        jax.ShapeDtypeStruct((n, D_MODEL), f32),
        jax.ShapeDtypeStruct((n, KV_COLS), _MXU_DTYPE),
        jax.ShapeDtypeStruct((n, IDX_DIM), _MXU_DTYPE),
        jax.ShapeDtypeStruct((KV_COLS, n), _MXU_DTYPE),
        jax.ShapeDtypeStruct((IDX_HEADS, n), f32),
    )
    out_specs = tuple(row(s.shape[1]) for s in out_shape[:10]) + (col(KV_COLS), col(IDX_HEADS))
    return pl.pallas_call(
        _proj_kernel,
        grid=(n // tm,),
        in_specs=[row(D_MODEL), _resident((1, D_MODEL)), _resident(w_main.shape),
                  _resident((1, 2 * D_MODEL)), table, table],
        out_specs=out_specs,
        out_shape=out_shape,
        compiler_params=_params(("parallel",)),
        name=f"proj_n{n}",
    )(h, g.reshape(1, D_MODEL), w_main, b_gate.reshape(1, 2 * D_MODEL), cos, sin)


def _float_to_key(x):
    bits = lax.bitcast_convert_type(x, jnp.int32)
    return jnp.where(bits >= 0, bits, bits ^ 0x7FFFFFFF)


def _key_to_float(key):
    return lax.bitcast_convert_type(jnp.where(key >= 0, key, key ^ 0x7FFFFFFF), jnp.float32)


def _normal_quantile(p):
    tail = jnp.minimum(p, 1.0 - p)
    t = jnp.sqrt(-2.0 * jnp.log(tail))
    z = t - (2.515517 + t * (0.802853 + t * 0.010328)) / (1.0 + t * (1.432788 + t * (0.189269 + t * 0.001308)))
    return jnp.where(p < 0.5, z, -z)


def _kth_largest(s_ref, count_ge, n_valid, top, kt):
    cols = s_ref.shape[1]
    shape = (1, cols)
    head = s_ref[0:min(kt, 256), :]
    mean = head.mean(axis=0, keepdims=True)
    std = jnp.sqrt(jnp.maximum((head * head).mean(axis=0, keepdims=True) - mean * mean, 1e-30))
    guess = mean + std * _normal_quantile(jnp.minimum((top - 0.5) / jnp.maximum(n_valid, 1.0), 0.999))
    log_top = jnp.log(top - 0.5)

    def adjacent(lo_k, hi_k):
        return (hi_k <= lo_k + 1) | ((lo_k == 0) & (hi_k == _KEY_MIN_NORMAL))

    def active(state):
        _, lo_k, hi_k, clo = state[:4]
        return (clo > top) & jnp.logical_not(adjacent(lo_k, hi_k))

    def cond(state):
        return (state[0] < _MAX_SEARCH_PASSES) & (jnp.max(jnp.where(active(state), 1.0, 0.0)) > 0.0)

    def body(state):
        it, lo_k, hi_k, clo, chi, lo_real, hi_real, run, last_up = state
        lo_v, hi_v = _key_to_float(lo_k), _key_to_float(hi_k)
        both = (lo_real > 0) & (hi_real > 0)
        log_lo = jnp.log(clo)
        frac = (log_lo - log_top) / (log_lo - jnp.log(jnp.maximum(chi, 0.5)))
        frac = jnp.where(clo - chi > 16.0, frac, 0.5)
        step = std * 0.5 * lax.shift_left(jnp.int32(1), jnp.minimum(run, 20)).astype(jnp.float32)
        t = jnp.where(both, lo_v + (hi_v - lo_v) * frac, jnp.where(lo_real > 0, lo_v + step, hi_v - step))
        t = jnp.where(it == 0, guess, t)
        t_k = _float_to_key(t)
        mid_k = (lo_k >> 1) + (hi_k >> 1) + (lo_k & hi_k & 1)
        bisect = (both & (run >= 2)) | (it >= 24)
        t_k = jnp.where(bisect, mid_k, t_k)
        t_k = jnp.where((lo_k < 0) & (hi_k > _KEY_MIN_NORMAL) & (it >= 3), 0, t_k)
        t_k = jnp.where((lo_k == 0) & (hi_k > _KEY_MIN_NORMAL), _KEY_MIN_NORMAL, t_k)
        t_k = jnp.minimum(jnp.maximum(t_k, lo_k + 1), hi_k - 1)
        c = count_ge(_key_to_float(t_k))
        live = active(state)
        up = c >= top
        go_lo, go_hi = live & up, live & jnp.logical_not(up)
        same = jnp.where(up, 1, -1) == last_up
        return (it + 1,
                jnp.where(go_lo, t_k, lo_k), jnp.where(go_hi, t_k, hi_k),
                jnp.where(go_lo, c, clo), jnp.where(go_hi, c, chi),
                jnp.where(go_lo, 1, lo_real), jnp.where(go_hi, 1, hi_real),
                jnp.where(bisect, 0, jnp.where(same, run + 1, 1)),
                jnp.where(up, 1, -1))

    zeros = jnp.zeros(shape, jnp.int32)
    init = (jnp.int32(0), jnp.full(shape, _KEY_NEG_FLT_MAX, jnp.int32), jnp.full(shape, _KEY_POS_INF, jnp.int32),
            n_valid, jnp.zeros(shape, jnp.float32), zeros, zeros, zeros, zeros)
    out = lax.while_loop(cond, body, init)
    return _key_to_float(out[1]), out[3]


def _count_ge(ref, x, n_kt, kt):
    cols = ref.shape[1]

    def tile(j, acc):
        blk = ref[pl.ds(pl.multiple_of(j * kt, kt), kt), :]
        ones = jnp.where(blk >= x, 1.0, 0.0)
        return acc + ones.reshape(kt // 8, 8, cols).sum(axis=0)

    def pair(jj, acc):
        return tile(2 * jj + 1, tile(2 * jj, acc))

    acc = lax.fori_loop(0, n_kt // 2, pair, jnp.zeros((8, cols), jnp.float32))
    acc = lax.cond(n_kt % 2 == 1, lambda a: tile(n_kt - 1, a), lambda a: a, acc)
    return acc.sum(axis=0, keepdims=True)


def _attn_kernel(qi_ref, wit_ref, q_ref, kidx_ref, k_ref, vt_ref, o_ref,
                 s_ref, bias0_ref, bias1_ref, lg0_ref, lg1_ref, acc_ref, *, past, tq, kt, top):
    t0 = pl.program_id(1) * tq
    n_kt = (past + t0 + tq + kt - 1) // kt
    lane_q = lax.broadcasted_iota(jnp.int32, (1, tq), 1)
    limit = past + ((t0 + lane_q) // CHUNK + 1) * CHUNK
    w = wit_ref[...]

    def score_rows(off, rows, last):
        kx = kidx_ref[pl.ds(off, rows), :]
        acc = jnp.zeros((rows, tq), jnp.float32)
        for h in range(IDX_HEADS):
            s = _dot_nt(kx, qi_ref[:, h * IDX_DIM:(h + 1) * IDX_DIM])
            acc = acc + w[h:h + 1, :] * jnp.maximum(s, 0.0)
        if last:
            key = off + lax.broadcasted_iota(jnp.int32, (rows, tq), 0)
            acc = jnp.where(key < limit, acc, -jnp.inf)
        s_ref[pl.ds(off, rows), :] = acc

    def score_pair(jj, carry):
        score_rows(pl.multiple_of(jj * 2 * kt, 2 * kt), 2 * kt, False)
        return carry

    n_full = n_kt - 1
    lax.fori_loop(0, n_full // 2, score_pair, 0)

    @pl.when(n_full % 2 == 1)
    def _():
        score_rows(pl.multiple_of((n_full - 1) * kt, kt), kt, False)

    score_rows(pl.multiple_of(n_full * kt, kt), kt, True)

    def count_ge(thr):
        return _count_ge(s_ref, thr, n_kt, kt)

    thr, cnt = _kth_largest(s_ref, count_ge, limit.astype(jnp.float32), top, kt)

    def resolve_ties(cnt):
        excess = cnt - top
        row = lax.broadcasted_iota(jnp.int32, (kt, kt), 0)
        col = lax.broadcasted_iota(jnp.int32, (kt, kt), 1)
        at_or_after = jnp.where(col >= row, 1.0, 0.0).astype(_MXU_DTYPE)

        def tile(jr, after):
            rows = pl.ds(pl.multiple_of((n_kt - 1 - jr) * kt, kt), kt)
            blk = s_ref[rows, :]
            equal = blk == thr
            suffix = after + _dot(at_or_after, jnp.where(equal, 1.0, 0.0).astype(_MXU_DTYPE))
            s_ref[rows, :] = jnp.where(equal, jnp.where(suffix <= excess, -jnp.inf, blk), blk)
            return suffix[0:1, :]

        lax.fori_loop(0, n_kt, tile, jnp.zeros((1, tq), jnp.float32))
        return count_ge(thr)

    cnt = lax.cond(jnp.max(cnt) > top, resolve_ties, lambda c: c, cnt)

    def drop_lowest(cnt):
        over = cnt > top

        def tile(j):
            off = pl.multiple_of(j * kt, kt)
            return off, s_ref[pl.ds(off, kt), :]

        def min_body(j, vmin):
            _, blk = tile(j)
            cand = jnp.where(blk >= thr, blk, jnp.inf)
            return jnp.minimum(vmin, cand.reshape(kt // 8, 8, tq).min(axis=0))

        vmin = lax.fori_loop(0, n_kt, min_body, jnp.full((8, tq), jnp.inf, jnp.float32))
        vmin = vmin.min(axis=0, keepdims=True)

        def idx_body(j, imax):
            off, blk = tile(j)
            key_idx = off + lax.broadcasted_iota(jnp.int32, (kt, tq), 0)
            cand = jnp.where(blk == vmin, key_idx, -1)
            return jnp.maximum(imax, cand.reshape(kt // 8, 8, tq).max(axis=0))

        imax = lax.fori_loop(0, n_kt, idx_body, jnp.full((8, tq), -1, jnp.int32))
        kill = jnp.where(over, imax.max(axis=0, keepdims=True), -1)

        def kill_body(j, carry):
            off, blk = tile(j)
            key_idx = off + lax.broadcasted_iota(jnp.int32, (kt, tq), 0)
            s_ref[pl.ds(off, kt), :] = jnp.where(key_idx == kill, -jnp.inf, blk)
            return carry

        lax.fori_loop(0, n_kt, kill_body, 0)
        return jnp.where(over, cnt - 1.0, cnt)

    lax.while_loop(lambda cnt: jnp.max(cnt) > top, drop_lowest, cnt)

    acc_ref[...] = jnp.zeros(acc_ref.shape, jnp.float32)

    def logits_sweep(j, m, bias_buf, lg_buf):
        off = pl.multiple_of(jnp.minimum(j, n_kt - 1) * kt, kt)
        bias_buf[...] = jnp.where(s_ref[pl.ds(off, kt), :] >= jnp.where(j < n_kt, thr, jnp.inf), 0.0, _NEG)
        m_new = []
        for h in range(N_HEADS):
            c = h // GROUP
            kx = k_ref[pl.ds(off, kt), c * HEAD_DIM:(c + 1) * HEAD_DIM]
            lg = _dot_nt(kx, q_ref[:, h * HEAD_DIM:(h + 1) * HEAD_DIM]) + bias_buf[...]
            lg_buf[h] = lg
            m_new.append(jnp.maximum(m[h], lg.max(axis=0, keepdims=True)))
        return tuple(m_new)

    def softmax_sweep(j, m_old, m_new, l, lg_buf):
        off = pl.multiple_of(jnp.minimum(j, n_kt - 1) * kt, kt)
        l_new = []
        for h in range(N_HEADS):
            c = h // GROUP
            vx = vt_ref[c * HEAD_DIM:(c + 1) * HEAD_DIM, pl.ds(off, kt)]
            alpha = jnp.exp(m_old[h] - m_new[h])
            p = jnp.exp(lg_buf[h] - m_new[h])
            acc_ref[h] = alpha * acc_ref[h] + _dot(vx, p.astype(vx.dtype))
            l_new.append(alpha * l[h] + p.sum(axis=0, keepdims=True))
        return tuple(l_new)

    def attn_step(jj, carry):
        m_a, m_b, l = carry
        m_c = logits_sweep(2 * jj + 1, m_b, bias1_ref, lg1_ref)
        l = softmax_sweep(2 * jj, m_a, m_b, l, lg0_ref)
        m_d = logits_sweep(2 * jj + 2, m_c, bias0_ref, lg0_ref)
        l = softmax_sweep(2 * jj + 1, m_b, m_c, l, lg1_ref)
        return m_c, m_d, l

    m_init = tuple(jnp.full((1, tq), _NEG, jnp.float32) for _ in range(N_HEADS))
    l_init = tuple(jnp.zeros((1, tq), jnp.float32) for _ in range(N_HEADS))
    m_first = logits_sweep(0, m_init, bias0_ref, lg0_ref)
    if kt == s_ref.shape[0]:
        l = softmax_sweep(0, m_init, m_first, l_init, lg0_ref)
    else:
        _, _, l = lax.fori_loop(0, (n_kt + 1) // 2, attn_step, (m_init, m_first, l_init))
    if o_ref.shape[0] == tq:
        o = jnp.concatenate([acc_ref[h] / l[h] for h in range(N_HEADS)], axis=0)
        o_ref[...] = o.T.astype(o_ref.dtype)
    else:
        for h in range(N_HEADS):
            o_ref[h * HEAD_DIM:(h + 1) * HEAD_DIM, :] = (acc_ref[h] / l[h]).astype(o_ref.dtype)


def _attn(qi, wit, q, kidx, k, vt, *, batch, past, tq, kt):
    t = q.shape[0] // batch
    nq = t // tq
    if k.ndim == 2:
        lp = t
        key_specs = [
            pl.BlockSpec((lp, IDX_DIM), lambda bi, i: (bi, 0)),
            pl.BlockSpec((lp, KV_COLS), lambda bi, i: (bi, 0)),
            pl.BlockSpec((KV_COLS, lp), lambda bi, i: (0, bi)),
        ]
    else:
        lp = k.shape[1]
        key_specs = [
            pl.BlockSpec((None, lp, IDX_DIM), lambda bi, i: (bi, 0, 0)),
            pl.BlockSpec((None, lp, KV_COLS), lambda bi, i: (bi, 0, 0)),
            pl.BlockSpec((None, KV_COLS, lp), lambda bi, i: (bi, 0, 0)),
        ]
    n_keys = past + t
    assert t % tq == 0 and tq % CHUNK == 0 and lp % kt == 0 and lp >= n_keys
    assert t == tq or (past == 0 and kt == tq)
    top = min(TOPK_MAX, n_keys // 4)
    transpose_in_kernel = tq % LANES == 0
    if transpose_in_kernel:
        out_shape = jax.ShapeDtypeStruct((batch * t, Q_COLS), _MXU_DTYPE)
        out_spec = pl.BlockSpec((tq, Q_COLS), lambda bi, i: (bi * nq + i, 0))
    else:
        out_shape = jax.ShapeDtypeStruct((batch, Q_COLS, t), _MXU_DTYPE)
        out_spec = pl.BlockSpec((None, Q_COLS, tq), lambda bi, i: (bi, 0, i))
    if tq % LANES == 0:
        wit_spec = pl.BlockSpec((IDX_HEADS, tq), lambda bi, i: (0, bi * nq + i))
    else:
        wit = wit.reshape(IDX_HEADS, batch, t).transpose(1, 0, 2)
        wit_spec = pl.BlockSpec((None, IDX_HEADS, tq), lambda bi, i: (bi, 0, i))
    out = pl.pallas_call(
        functools.partial(_attn_kernel, past=past, tq=tq, kt=kt, top=top),
        grid=(batch, nq),
        in_specs=[
            pl.BlockSpec((tq, IDXQ_COLS), lambda bi, i: (bi * nq + i, 0)),
            wit_spec,
            pl.BlockSpec((tq, Q_COLS), lambda bi, i: (bi * nq + i, 0)),
        ] + key_specs,
        out_specs=out_spec,
        out_shape=out_shape,
        scratch_shapes=[
            pltpu.VMEM((lp, tq), jnp.float32),
            pltpu.VMEM((kt, tq), jnp.float32),
            pltpu.VMEM((kt, tq), jnp.float32),
            pltpu.VMEM((N_HEADS, kt, tq), jnp.float32),
            pltpu.VMEM((N_HEADS, kt, tq), jnp.float32),
            pltpu.VMEM((N_HEADS, HEAD_DIM, tq), jnp.float32),
        ],
        compiler_params=_params(("parallel", "arbitrary")),
        name=f"attn_tq{tq}",
    )(qi, wit, q, kidx, k, vt)
    if transpose_in_kernel:
        return out
    return out.transpose(0, 2, 1).reshape(batch * t, Q_COLS)


def _merge_kernel(h_ref, c_ref, halo_ref, at_ref, ga_ref, gc_ref, cw_ref, cb_ref, lng_ref, lnb_ref,
                  wco_ref, wao_ref, wo_ref, o_ref, win_ref, shift_ref, *, seq_len):
    tm = c_ref.shape[0]
    n_chunk = tm // CHUNK
    lead = _HALO - (CONV_WIDTH - 1)
    dcs = []
    for j in range(n_chunk):
        if seq_len == CHUNK:
            history = halo_ref[j]
        elif j == 0:
            starts_sequence = (pl.program_id(0) * tm) % seq_len == 0
            history = jnp.where(starts_sequence, 0.0, halo_ref[...])
        else:
            history = c_ref[j * CHUNK - _HALO:j * CHUNK, :]
        win_ref[j, 0:_HALO, :] = history
        win_ref[j, _HALO:_HALO + CHUNK, :] = c_ref[j * CHUNK:(j + 1) * CHUNK, :]
        for s in range(1, SUBLANES):
            shift_ref[j, s - 1] = win_ref[j, s:s + shift_ref.shape[2], :]
        acc = jnp.zeros((CHUNK, CONV_CH), jnp.float32)
        for tap in range(CONV_WIDTH):
            s, base = (lead + tap) % SUBLANES, (lead + tap) // SUBLANES * SUBLANES
            rows = win_ref[j, base:base + CHUNK, :] if s == 0 else shift_ref[j, s - 1, base:base + CHUNK, :]
            acc = acc + rows * cw_ref[tap:tap + 1, :]
        dcs.append(acc)
    dc = jnp.concatenate(dcs, axis=0) + cb_ref[...]
    mu = jnp.mean(dc, axis=-1, keepdims=True)
    var = jnp.mean(jnp.square(dc - mu), axis=-1, keepdims=True)
    y = (dc - mu) * lax.rsqrt(var + EPS) * lng_ref[...] + lnb_ref[...]
    conv_out = _dot((y * jax.nn.sigmoid(y)).astype(wco_ref.dtype), wco_ref[...])
    attn_out = _dot(at_ref[...], wao_ref[...])
    merged = ga_ref[...] * attn_out + gc_ref[...] * conv_out
    o_ref[...] = h_ref[...] + _dot(merged.astype(wo_ref.dtype), wo_ref[...])


def _merge(h, c, state, attn, ga, gc, weights, tm, seq_len):
    n = h.shape[0]
    cw, conv_b, ln_g, ln_b, wco, wao, wo = weights

    def row(width):
        return pl.BlockSpec((tm, width), lambda i: (i, 0))

    if seq_len == CHUNK:
        halo, halo_spec = state, pl.BlockSpec((tm // CHUNK, _HALO, CONV_CH), lambda i: (i, 0, 0))
    else:
        assert state is None and seq_len % tm == 0
        per_tile = tm // _HALO
        halo, halo_spec = c, pl.BlockSpec((_HALO, CONV_CH), lambda i: (jnp.maximum(i * per_tile - 1, 0), 0))
    vec = _resident((1, CONV_CH))
    return pl.pallas_call(
        functools.partial(_merge_kernel, seq_len=seq_len),
        grid=(n // tm,),
        in_specs=[
            row(D_MODEL), row(CONV_CH), halo_spec,
            row(Q_COLS), row(D_MODEL), row(D_MODEL),
            _resident(cw.shape), vec, vec, vec,
            _resident((CONV_CH, D_MODEL)), _resident((Q_COLS, D_MODEL)), _resident((D_MODEL, D_MODEL)),
        ],
        out_specs=row(D_MODEL),
        out_shape=jax.ShapeDtypeStruct((n, D_MODEL), jnp.float32),
        scratch_shapes=[pltpu.VMEM((tm // CHUNK, _HALO + CHUNK, CONV_CH), jnp.float32),
                        pltpu.VMEM((tm // CHUNK, SUBLANES - 1, _HALO + CHUNK - SUBLANES, CONV_CH), jnp.float32)],
        compiler_params=_params(("parallel",)),
        name=f"merge_n{n}",
    )(h, c, halo, attn, ga, gc, cw, conv_b, ln_g, ln_b, wco, wao, wo)


def _row_tile(n, candidates):
    for tm in candidates:
        if n % tm == 0:
            return tm
    raise ValueError(f"row count {n} is not a multiple of {candidates[-1]}")


def _rope_tables(pos, rows):
    inv = ROPE_THETA ** (-jnp.arange(0, HEAD_DIM, 2, dtype=jnp.float32) / HEAD_DIM)
    ang = pos.astype(jnp.float32)[:, None] * inv[None, :]
    cos, sin = jnp.cos(ang), jnp.sin(ang)
    reps = rows // pos.shape[0]
    return (jnp.tile(cos, (reps, 4)), jnp.tile(jnp.concatenate([-sin, sin], axis=1), (reps, 2)))


def _main_weight(w_in):
    sizes = (Q_COLS, KV_COLS, KV_COLS, IDXQ_COLS, IDX_DIM, IDX_HEADS, 2 * CONV_CH, 2 * D_MODEL)
    offs = [0]
    for s in sizes:
        offs.append(offs[-1] + s)
    wq, wk, wv, wqi, wki, wwi, wconv, wgate = (w_in[:, offs[i]:offs[i + 1]] for i in range(len(sizes)))
    pad = jnp.zeros((D_MODEL, LANES - IDX_DIM - IDX_HEADS), w_in.dtype)
    return jnp.concatenate([wq, wqi, wk, wki, wwi, pad, wv, wconv, wgate], axis=1).astype(_MXU_DTYPE)


def _layer(x, caches, weights, final_norm):
    (ffn1_norm, ffn1_w, mix_norm, w_main, b_gate, merge_w, ffn2_norm, ffn2_w) = weights
    b, t, _ = x.shape
    n = b * t
    assert t % CHUNK == 0
    tm = _row_tile(n, (512, 256, 128, 64))
    tm_merge = _row_tile(n, (256, 128, 64))
    past = 0 if caches is None else caches[0].shape[1]

    h = _ffn(x.reshape(n, D_MODEL), ffn1_norm, ffn1_w, None, tm)
    if t >= tm:
        assert t % tm == 0
    else:
        assert tm % t == 0
    cos, sin = _rope_tables(past + jnp.arange(t, dtype=jnp.int32), max(t, tm))
    q, qi, k, ki, v, c, ga, gc, kb, kib, vt, wit = _proj(h, mix_norm, w_main, b_gate, cos, sin, tm)

    if caches is None:
        tq = _row_tile(t, (256, 128, 64))
        attn = _attn(qi, wit, q, kib, kb, vt, batch=b, past=0, tq=tq, kt=tq)
        state = None
        conv_tail = c.reshape(b, t, CONV_CH)[:, t - (CONV_WIDTH - 1):]
    else:
        cache_k, cache_v, cache_idx_k, state_conv = caches
        n_keys = past + t
        pad_keys = -n_keys % LANES

        def with_cache(cache, new):
            cache = cache.reshape(b, past, -1).astype(_MXU_DTYPE)
            pad = jnp.zeros((b, pad_keys, cache.shape[2]), _MXU_DTYPE)
            return jnp.concatenate([cache, new.reshape(b, t, -1), pad], axis=1)

        vt_all = with_cache(cache_v, vt.reshape(KV_COLS, b, t).transpose(1, 2, 0)).transpose(0, 2, 1)
        attn = _attn(qi, wit, q, with_cache(cache_idx_k, kib), with_cache(cache_k, kb), vt_all,
                     batch=b, past=past, tq=t, kt=n_keys + pad_keys)
        lead = jnp.zeros((b, _HALO - (CONV_WIDTH - 1), CONV_CH), jnp.float32)
        state = jnp.concatenate([lead, state_conv], axis=1)
        conv_tail = jnp.concatenate([state_conv, c.reshape(b, t, CONV_CH)], axis=1)[:, -(CONV_WIDTH - 1):]
    if t != CHUNK:
        assert state is None, "a carried conv state is supported for 64-row sequences only"

    h2 = _merge(h, c, state, attn, ga, gc, merge_w, tm_merge, t)
    y = _ffn(h2, ffn2_norm, ffn2_w, final_norm, tm)
    return (y.reshape(b, t, D_MODEL),
            k.reshape(1, b, t, N_KV_HEADS, HEAD_DIM), v.reshape(1, b, t, N_KV_HEADS, HEAD_DIM),
            ki.reshape(1, b, t, IDX_DIM), conv_tail[None])


def kernel(x_prompt, x_sample, cache_k, cache_v, cache_idx_k, state_conv, ffn1_norm, ffn1_w_in, ffn1_w_out, mix_norm, w_in, b_gate, conv_w, conv_b, conv_ln_g, conv_ln_b, conv_w_out, attn_w_out, w_out, ffn2_norm, ffn2_w_in, ffn2_w_out, final_norm):
    assert ffn1_norm.shape[0] == 1, "one layer"
    assert x_prompt.shape[1] >= CONV_WIDTH - 1
    conv_taps = jnp.zeros((_HALO, CONV_CH), jnp.float32).at[:CONV_WIDTH].set(conv_w[0])
    merge_w = (conv_taps, conv_b[0].reshape(1, CONV_CH), conv_ln_g[0].reshape(1, CONV_CH),
               conv_ln_b[0].reshape(1, CONV_CH), conv_w_out[0].astype(_MXU_DTYPE),
               attn_w_out[0].astype(_MXU_DTYPE), w_out[0].astype(_MXU_DTYPE))
    weights = (ffn1_norm[0], _ffn_weights(ffn1_w_in[0], ffn1_w_out[0]), mix_norm[0],
               _main_weight(w_in[0]), b_gate[0], merge_w,
               ffn2_norm[0], _ffn_weights(ffn2_w_in[0], ffn2_w_out[0]))
    y_p, k_p, v_p, ki_p, conv_p = _layer(x_prompt, None, weights, final_norm)
    y_s, k_s, v_s, ki_s, conv_s = _layer(
        x_sample, (cache_k[0], cache_v[0], cache_idx_k[0], state_conv[0]), weights, final_norm)
    return (y_p, y_s, k_p, v_p, ki_p, conv_p, k_s, v_s, ki_s, conv_s)
```

```python
import functools
import math

import jax
import jax.numpy as jnp
from jax import lax
from jax.experimental import pallas as pl
from jax.experimental.pallas import tpu as pltpu

D_MODEL = 1024
CHUNK = 64
N_HEADS = 8
N_KV_HEADS = 2
HEAD_DIM = 64
GROUP = N_HEADS // N_KV_HEADS
IDX_HEADS = 8
IDX_DIM = 64
TOPK_MAX = 256
CONV_CH = 512
CONV_WIDTH = 31
D_FF = 2816
ROPE_THETA = 10000.0
EPS = 1e-6

Q_COLS = N_HEADS * HEAD_DIM
KV_COLS = N_KV_HEADS * HEAD_DIM
IDXQ_COLS = IDX_HEADS * IDX_DIM

LANES = 128
SUBLANES = 8
_MXU_DTYPE = jnp.bfloat16
_VMEM_LIMIT = 56 * 1024 * 1024
_FF_CHUNK = 256
_HALO = 32
_NEG = -1e30
_Q_SCALE = HEAD_DIM ** -0.5 * math.log2(math.e)

_C_Q = 0
_C_QI = _C_Q + Q_COLS
_C_K = _C_QI + IDXQ_COLS
_C_KW = _C_K + KV_COLS
_C_V = _C_KW + LANES
_C_CA = _C_V + KV_COLS
_C_CB = _C_CA + CONV_CH
_C_GA = _C_CB + CONV_CH
_C_GC = _C_GA + D_MODEL
_C_END = _C_GC + D_MODEL

_KEY_NEG_FLT_MAX = -2139095040
_KEY_POS_INF = 0x7F800000
_KEY_MIN_NORMAL = 0x00800000
_MAX_SEARCH_PASSES = 96


def _params(sem):
    return pltpu.CompilerParams(dimension_semantics=sem, vmem_limit_bytes=_VMEM_LIMIT)


def _resident(shape):
    nd = len(shape)
    return pl.BlockSpec(shape, lambda *_: (0,) * nd, pipeline_mode=pl.Buffered(1))


def _rms(x, g):
    return x * lax.rsqrt(jnp.mean(x * x, axis=-1, keepdims=True) + EPS) * g


def _dot(a, b):
    return jnp.dot(a, b, preferred_element_type=jnp.float32)


def _dot_nt(a, b):
    return lax.dot_general(a, b, (((1,), (1,)), ((), ())), preferred_element_type=jnp.float32)


def _ffn_kernel(*refs, final_norm):
    if final_norm:
        x_ref, g_ref, wa_ref, wb_ref, wo_ref, gf_ref, o_ref = refs
    else:
        x_ref, g_ref, wa_ref, wb_ref, wo_ref, o_ref = refs
    x = x_ref[...]
    u = _rms(x, g_ref[...]).astype(wa_ref.dtype)
    acc = jnp.zeros_like(x)
    for c in range(D_FF // _FF_CHUNK):
        sl = slice(c * _FF_CHUNK, (c + 1) * _FF_CHUNK)
        a = _dot(u, wa_ref[:, sl])
        b = _dot(u, wb_ref[:, sl])
        act = (a * jax.nn.sigmoid(a) * b).astype(wo_ref.dtype)
        acc = acc + _dot(act, wo_ref[sl, :])
    h = x + 0.5 * acc
    if final_norm:
        h = _rms(h, gf_ref[...])
    o_ref[...] = h


def _ffn_weights(w_in, w_out):
    return w_in[:, :D_FF].astype(_MXU_DTYPE), w_in[:, D_FF:].astype(_MXU_DTYPE), w_out.astype(_MXU_DTYPE)


def _ffn(x, g, weights, g_final, tm):
    n = x.shape[0]
    wa, wb, wo = weights
    row = pl.BlockSpec((tm, D_MODEL), lambda i: (i, 0))
    vec = _resident((1, D_MODEL))
    in_specs = [row, vec, _resident(wa.shape), _resident(wb.shape), _resident(wo.shape)]
    args = [x, g.reshape(1, D_MODEL), wa, wb, wo]
    if g_final is not None:
        in_specs.append(vec)
        args.append(g_final.reshape(1, D_MODEL))
    return pl.pallas_call(
        functools.partial(_ffn_kernel, final_norm=g_final is not None),
        grid=(n // tm,),
        in_specs=in_specs,
        out_specs=row,
        out_shape=jax.ShapeDtypeStruct((n, D_MODEL), jnp.float32),
        compiler_params=_params(("parallel",)),
        name=f"ffn_final_n{n}" if g_final is not None else f"ffn_n{n}",
    )(*args)


def _swap_halves(x):
    lane = lax.broadcasted_iota(jnp.int32, x.shape, 1)
    first = (lane & (HEAD_DIM - 1)) < HEAD_DIM // 2
    return jnp.where(first, pltpu.roll(x, LANES - HEAD_DIM // 2, 1), pltpu.roll(x, HEAD_DIM // 2, 1))


def _rope(x, cos, sin_signed):
    tiles = []
    for j in range(x.shape[1] // LANES):
        xt = x[:, j * LANES:(j + 1) * LANES]
        tiles.append(xt * cos + _swap_halves(xt) * sin_signed)
    return tiles[0] if len(tiles) == 1 else jnp.concatenate(tiles, axis=1)


def _proj_kernel(h_ref, g_ref, w_ref, bg_ref, cos_ref, sin_ref,
                 q_ref, qi_ref, k_ref, ki_ref, v_ref, c_ref, ga_ref, gc_ref,
                 kb_ref, kib_ref, vt_ref, wit_ref):
    u = _rms(h_ref[...], g_ref[...]).astype(w_ref.dtype)
    cos = cos_ref[...]
    sin = sin_ref[...]

    def proj(lo, hi):
        return _dot(u, w_ref[:, lo:hi])

    q_ref[...] = (_rope(proj(_C_Q, _C_QI), cos, sin) * _Q_SCALE).astype(q_ref.dtype)
    qi_ref[...] = _rope(proj(_C_QI, _C_K), cos, sin).astype(qi_ref.dtype)
    k = _rope(proj(_C_K, _C_KW), cos, sin)
    k_ref[...] = k
    kb_ref[...] = k.astype(kb_ref.dtype)
    zkw = proj(_C_KW, _C_V)
    ki = _rope(zkw, cos, sin)[:, :IDX_DIM]
    ki_ref[...] = ki
    kib_ref[...] = ki.astype(kib_ref.dtype)
    wit_ref[...] = (zkw * (IDX_HEADS ** -0.5)).T[IDX_DIM:IDX_DIM + IDX_HEADS, :]
    v = proj(_C_V, _C_CA)
    v_ref[...] = v
    vt_ref[...] = v.T.astype(vt_ref.dtype)
    c_ref[...] = proj(_C_CA, _C_CB) * jax.nn.sigmoid(proj(_C_CB, _C_GA))
    ga_ref[...] = jax.nn.sigmoid(proj(_C_GA, _C_GC) + bg_ref[:, :D_MODEL])
    gc_ref[...] = jax.nn.sigmoid(proj(_C_GC, _C_END) + bg_ref[:, D_MODEL:])


def _proj(h, g, w_main, b_gate, cos, sin, tm):
    n = h.shape[0]
    n_tab = cos.shape[0] // tm

    def row(width):
        return pl.BlockSpec((tm, width), lambda i: (i, 0))

    def col(height):
        return pl.BlockSpec((height, tm), lambda i: (0, i))

    table = pl.BlockSpec((tm, LANES), lambda i: (i % n_tab, 0))
    f32 = jnp.float32
    out_shape = (
        jax.ShapeDtypeStruct((n, Q_COLS), _MXU_DTYPE),
        jax.ShapeDtypeStruct((n, IDXQ_COLS), _MXU_DTYPE),
        jax.ShapeDtypeStruct((n, KV_COLS), f32),
        jax.ShapeDtypeStruct((n, IDX_DIM), f32),
        jax.ShapeDtypeStruct((n, KV_COLS), f32),
        jax.ShapeDtypeStruct((n, CONV_CH), f32),
        jax.ShapeDtypeStruct((n, D_MODEL), f32),
        jax.ShapeDtypeStruct((n, D_MODEL), f32),
        jax.ShapeDtypeStruct((n, KV_COLS), _MXU_DTYPE),
        jax.ShapeDtypeStruct((n, IDX_DIM), _MXU_DTYPE),
        jax.ShapeDtypeStruct((KV_COLS, n), _MXU_DTYPE),
        jax.ShapeDtypeStruct((IDX_HEADS, n), f32),
    )
    out_specs = tuple(row(s.shape[1]) for s in out_shape[:10]) + (col(KV_COLS), col(IDX_HEADS))
    return pl.pallas_call(
        _proj_kernel,
        grid=(n // tm,),
        in_specs=[row(D_MODEL), _resident((1, D_MODEL)), _resident(w_main.shape),
                  _resident((1, 2 * D_MODEL)), table, table],
        out_specs=out_specs,
        out_shape=out_shape,
        compiler_params=_params(("parallel",)),
        name=f"proj_n{n}",
    )(h, g.reshape(1, D_MODEL), w_main, b_gate.reshape(1, 2 * D_MODEL), cos, sin)


def _float_to_key(x):
    bits = lax.bitcast_convert_type(x, jnp.int32)
    return jnp.where(bits >= 0, bits, bits ^ 0x7FFFFFFF)


def _key_to_float(key):
    return lax.bitcast_convert_type(jnp.where(key >= 0, key, key ^ 0x7FFFFFFF), jnp.float32)


def _normal_quantile(p):
    tail = jnp.minimum(p, 1.0 - p)
    t = jnp.sqrt(-2.0 * jnp.log(tail))
    z = t - (2.515517 + t * (0.802853 + t * 0.010328)) / (1.0 + t * (1.432788 + t * (0.189269 + t * 0.001308)))
    return jnp.where(p < 0.5, z, -z)


def _kth_largest(s_ref, count_ge, n_valid, top, kt):
    cols = s_ref.shape[1]
    shape = (1, cols)
    head = s_ref[0:min(kt, 256), :]
    mean = head.mean(axis=0, keepdims=True)
    std = jnp.sqrt(jnp.maximum((head * head).mean(axis=0, keepdims=True) - mean * mean, 1e-30))
    guess = mean + std * _normal_quantile(jnp.minimum((top - 0.5) / jnp.maximum(n_valid, 1.0), 0.999))
    log_top = jnp.log(top - 0.5)

    def adjacent(lo_k, hi_k):
        return (hi_k <= lo_k + 1) | ((lo_k == 0) & (hi_k == _KEY_MIN_NORMAL))

    def active(state):
        _, lo_k, hi_k, clo = state[:4]
        return (clo > top) & jnp.logical_not(adjacent(lo_k, hi_k))

    def cond(state):
        return (state[0] < _MAX_SEARCH_PASSES) & (jnp.max(jnp.where(active(state), 1.0, 0.0)) > 0.0)

    def body(state):
        it, lo_k, hi_k, clo, chi, lo_real, hi_real, run, last_up = state
        lo_v, hi_v = _key_to_float(lo_k), _key_to_float(hi_k)
        both = (lo_real > 0) & (hi_real > 0)
        log_lo = jnp.log(clo)
        frac = (log_lo - log_top) / (log_lo - jnp.log(jnp.maximum(chi, 0.5)))
        frac = jnp.where(clo - chi > 16.0, frac, 0.5)
        step = std * 0.5 * lax.shift_left(jnp.int32(1), jnp.minimum(run, 20)).astype(jnp.float32)
        t = jnp.where(both, lo_v + (hi_v - lo_v) * frac, jnp.where(lo_real > 0, lo_v + step, hi_v - step))
        t = jnp.where(it == 0, guess, t)
        t_k = _float_to_key(t)
        mid_k = (lo_k >> 1) + (hi_k >> 1) + (lo_k & hi_k & 1)
        bisect = (both & (run >= 2)) | (it >= 24)
        t_k = jnp.where(bisect, mid_k, t_k)
        t_k = jnp.where((lo_k < 0) & (hi_k > _KEY_MIN_NORMAL) & (it >= 3), 0, t_k)
        t_k = jnp.where((lo_k == 0) & (hi_k > _KEY_MIN_NORMAL), _KEY_MIN_NORMAL, t_k)
        t_k = jnp.minimum(jnp.maximum(t_k, lo_k + 1), hi_k - 1)
        c = count_ge(_key_to_float(t_k))
        live = active(state)
        up = c >= top
        go_lo, go_hi = live & up, live & jnp.logical_not(up)
        same = jnp.where(up, 1, -1) == last_up
        return (it + 1,
                jnp.where(go_lo, t_k, lo_k), jnp.where(go_hi, t_k, hi_k),
                jnp.where(go_lo, c, clo), jnp.where(go_hi, c, chi),
                jnp.where(go_lo, 1, lo_real), jnp.where(go_hi, 1, hi_real),
                jnp.where(bisect, 0, jnp.where(same, run + 1, 1)),
                jnp.where(up, 1, -1))

    zeros = jnp.zeros(shape, jnp.int32)
    init = (jnp.int32(0), jnp.full(shape, _KEY_NEG_FLT_MAX, jnp.int32), jnp.full(shape, _KEY_POS_INF, jnp.int32),
            n_valid, jnp.zeros(shape, jnp.float32), zeros, zeros, zeros, zeros)
    out = lax.while_loop(cond, body, init)
    return _key_to_float(out[1]), out[3]


def _count_ge(ref, x, n_kt, kt):
    cols = ref.shape[1]

    def tile(j, acc):
        blk = ref[pl.ds(pl.multiple_of(j * kt, kt), kt), :]
        ones = jnp.where(blk >= x, 1.0, 0.0)
        return acc + ones.reshape(kt // 8, 8, cols).sum(axis=0)

    def pair(jj, acc):
        return tile(2 * jj + 1, tile(2 * jj, acc))

    acc = lax.fori_loop(0, n_kt // 2, pair, jnp.zeros((8, cols), jnp.float32))
    acc = lax.cond(n_kt % 2 == 1, lambda a: tile(n_kt - 1, a), lambda a: a, acc)
    return acc.sum(axis=0, keepdims=True)


def _attn_kernel(qi_ref, wit_ref, q_ref, kidx_ref, k_ref, vt_ref, o_ref,
                 s_ref, bias0_ref, bias1_ref, lg0_ref, lg1_ref, acc_ref, *, past, tq, kt, top, seq_q):
    n_seq = tq // seq_q
    t0 = pl.program_id(1) * seq_q
    n_kt = (past + t0 + seq_q + kt - 1) // kt
    lane_q = lax.broadcasted_iota(jnp.int32, (1, tq), 1)
    limit = past + ((t0 + lane_q % seq_q) // CHUNK + 1) * CHUNK
    w = wit_ref[...]

    def keys_times_queries(keys_ref, rows, lanes, queries_ref, cols):
        parts = [_dot_nt(keys_ref[g, rows, lanes], queries_ref[g * seq_q:(g + 1) * seq_q, cols])
                 for g in range(n_seq)]
        return parts[0] if n_seq == 1 else jnp.concatenate(parts, axis=1)

    def score_rows(off, rows, last):
        acc = jnp.zeros((rows, tq), jnp.float32)
        for h in range(IDX_HEADS):
            s = keys_times_queries(kidx_ref, pl.ds(off, rows), slice(None), qi_ref,
                                   slice(h * IDX_DIM, (h + 1) * IDX_DIM))
            acc = acc + w[h:h + 1, :] * jnp.maximum(s, 0.0)
        if last:
            key = off + lax.broadcasted_iota(jnp.int32, (rows, tq), 0)
            acc = jnp.where(key < limit, acc, -jnp.inf)
        s_ref[pl.ds(off, rows), :] = acc

    def score_pair(jj, carry):
        score_rows(pl.multiple_of(jj * 2 * kt, 2 * kt), 2 * kt, False)
        return carry

    n_full = n_kt - 1
    lax.fori_loop(0, n_full // 2, score_pair, 0)

    @pl.when(n_full % 2 == 1)
    def _():
        score_rows(pl.multiple_of((n_full - 1) * kt, kt), kt, False)

    score_rows(pl.multiple_of(n_full * kt, kt), kt, True)

    def count_ge(thr):
        return _count_ge(s_ref, thr, n_kt, kt)

    thr, cnt = _kth_largest(s_ref, count_ge, limit.astype(jnp.float32), top, kt)

    def resolve_ties(cnt):
        excess = cnt - top
        row = lax.broadcasted_iota(jnp.int32, (kt, kt), 0)
        col = lax.broadcasted_iota(jnp.int32, (kt, kt), 1)
        at_or_after = jnp.where(col >= row, 1.0, 0.0).astype(_MXU_DTYPE)

        def tile(jr, after):
            rows = pl.ds(pl.multiple_of((n_kt - 1 - jr) * kt, kt), kt)
            blk = s_ref[rows, :]
            equal = blk == thr
            suffix = after + _dot(at_or_after, jnp.where(equal, 1.0, 0.0).astype(_MXU_DTYPE))
            s_ref[rows, :] = jnp.where(equal, jnp.where(suffix <= excess, -jnp.inf, blk), blk)
            return suffix[0:1, :]

        lax.fori_loop(0, n_kt, tile, jnp.zeros((1, tq), jnp.float32))
        return count_ge(thr)

    cnt = lax.cond(jnp.max(cnt) > top, resolve_ties, lambda c: c, cnt)

    def drop_lowest(cnt):
        over = cnt > top

        def tile(j):
            off = pl.multiple_of(j * kt, kt)
            return off, s_ref[pl.ds(off, kt), :]

        def min_body(j, vmin):
            _, blk = tile(j)
            cand = jnp.where(blk >= thr, blk, jnp.inf)
            return jnp.minimum(vmin, cand.reshape(kt // 8, 8, tq).min(axis=0))

        vmin = lax.fori_loop(0, n_kt, min_body, jnp.full((8, tq), jnp.inf, jnp.float32))
        vmin = vmin.min(axis=0, keepdims=True)

        def idx_body(j, imax):
            off, blk = tile(j)
            key_idx = off + lax.broadcasted_iota(jnp.int32, (kt, tq), 0)
            cand = jnp.where(blk == vmin, key_idx, -1)
            return jnp.maximum(imax, cand.reshape(kt // 8, 8, tq).max(axis=0))

        imax = lax.fori_loop(0, n_kt, idx_body, jnp.full((8, tq), -1, jnp.int32))
        kill = jnp.where(over, imax.max(axis=0, keepdims=True), -1)

        def kill_body(j, carry):
            off, blk = tile(j)
            key_idx = off + lax.broadcasted_iota(jnp.int32, (kt, tq), 0)
            s_ref[pl.ds(off, kt), :] = jnp.where(key_idx == kill, -jnp.inf, blk)
            return carry

        lax.fori_loop(0, n_kt, kill_body, 0)
        return jnp.where(over, cnt - 1.0, cnt)

    lax.while_loop(lambda cnt: jnp.max(cnt) > top, drop_lowest, cnt)

    acc_ref[...] = jnp.zeros(acc_ref.shape, jnp.float32)

    def logits_sweep(j, m, bias_buf, lg_buf):
        off = pl.multiple_of(jnp.minimum(j, n_kt - 1) * kt, kt)
        bias_buf[...] = jnp.where(s_ref[pl.ds(off, kt), :] >= jnp.where(j < n_kt, thr, jnp.inf), 0.0, _NEG)
        m_new = []
        for h in range(N_HEADS):
            c = h // GROUP
            lg = keys_times_queries(k_ref, pl.ds(off, kt), slice(c * HEAD_DIM, (c + 1) * HEAD_DIM), q_ref,
                                    slice(h * HEAD_DIM, (h + 1) * HEAD_DIM)) + bias_buf[...]
            lg_buf[h] = lg
            m_new.append(jnp.maximum(m[h], lg.max(axis=0, keepdims=True)))
        return tuple(m_new)

    def softmax_sweep(j, m_old, m_new, l, lg_buf):
        off = pl.multiple_of(jnp.minimum(j, n_kt - 1) * kt, kt)
        l_new = []
        for h in range(N_HEADS):
            c = h // GROUP
            alpha = jnp.exp2(m_old[h] - m_new[h])
            p = jnp.exp2(lg_buf[h] - m_new[h])
            pv = [_dot(vt_ref[g, c * HEAD_DIM:(c + 1) * HEAD_DIM, pl.ds(off, kt)],
                       p[:, g * seq_q:(g + 1) * seq_q].astype(vt_ref.dtype)) for g in range(n_seq)]
            acc_ref[h] = alpha * acc_ref[h] + (pv[0] if n_seq == 1 else jnp.concatenate(pv, axis=1))
            l_new.append(alpha * l[h] + p.sum(axis=0, keepdims=True))
        return tuple(l_new)

    def attn_step(jj, carry):
        m_a, m_b, l = carry
        m_c = logits_sweep(2 * jj + 1, m_b, bias1_ref, lg1_ref)
        l = softmax_sweep(2 * jj, m_a, m_b, l, lg0_ref)
        m_d = logits_sweep(2 * jj + 2, m_c, bias0_ref, lg0_ref)
        l = softmax_sweep(2 * jj + 1, m_b, m_c, l, lg1_ref)
        return m_c, m_d, l

    m_init = tuple(jnp.full((1, tq), _NEG, jnp.float32) for _ in range(N_HEADS))
    l_init = tuple(jnp.zeros((1, tq), jnp.float32) for _ in range(N_HEADS))
    m_first = logits_sweep(0, m_init, bias0_ref, lg0_ref)
    if kt == s_ref.shape[0]:
        l = softmax_sweep(0, m_init, m_first, l_init, lg0_ref)
    else:
        _, _, l = lax.fori_loop(0, (n_kt + 1) // 2, attn_step, (m_init, m_first, l_init))
    if o_ref.shape[0] == tq:
        o = jnp.concatenate([acc_ref[h] / l[h] for h in range(N_HEADS)], axis=0)
        o_ref[...] = o.T.astype(o_ref.dtype)
    else:
        for h in range(N_HEADS):
            o_ref[h * HEAD_DIM:(h + 1) * HEAD_DIM, :] = (acc_ref[h] / l[h]).astype(o_ref.dtype)


def _attn(qi, wit, q, kidx, k, vt, *, seq_len, past, tq, kt):
    batch, lp = k.shape[0], k.shape[1]
    t = seq_len
    n_seq = max(1, tq // t)
    seq_q = tq // n_seq
    nq = t // seq_q
    n_keys = past + t
    assert batch % n_seq == 0 and t % seq_q == 0 and seq_q % CHUNK == 0 and lp % kt == 0 and lp >= n_keys
    assert t == seq_q or (past == 0 and kt == tq)
    top = min(TOPK_MAX, n_keys // 4)

    def per_sequence(width_last, width_mid):
        return pl.BlockSpec((n_seq, width_mid, width_last), lambda bi, i: (bi, 0, 0))

    if vt.ndim == 2:
        assert n_seq == 1 and lp == t
        vt = vt.reshape(1, KV_COLS, batch * t)
        vt_spec = pl.BlockSpec((1, KV_COLS, lp), lambda bi, i: (0, 0, bi))
    else:
        vt_spec = per_sequence(lp, KV_COLS)
    transpose_in_kernel = tq % LANES == 0
    if transpose_in_kernel:
        out_shape = jax.ShapeDtypeStruct((batch * t, Q_COLS), _MXU_DTYPE)
        out_spec = pl.BlockSpec((tq, Q_COLS), lambda bi, i: (bi * nq + i, 0))
        wit_spec = pl.BlockSpec((IDX_HEADS, tq), lambda bi, i: (0, bi * nq + i))
    else:
        assert n_seq == 1
        out_shape = jax.ShapeDtypeStruct((batch, Q_COLS, t), _MXU_DTYPE)
        out_spec = pl.BlockSpec((None, Q_COLS, tq), lambda bi, i: (bi, 0, i))
        wit = wit.reshape(IDX_HEADS, batch, t).transpose(1, 0, 2)
        wit_spec = pl.BlockSpec((None, IDX_HEADS, tq), lambda bi, i: (bi, 0, i))
    single_tile = kt == lp
    spare = SUBLANES if single_tile else kt
    out = pl.pallas_call(
        functools.partial(_attn_kernel, past=past, tq=tq, kt=kt, top=top, seq_q=seq_q),
        grid=(batch // n_seq, nq),
        in_specs=[
            pl.BlockSpec((tq, IDXQ_COLS), lambda bi, i: (bi * nq + i, 0)),
            wit_spec,
            pl.BlockSpec((tq, Q_COLS), lambda bi, i: (bi * nq + i, 0)),
            per_sequence(IDX_DIM, lp), per_sequence(KV_COLS, lp), vt_spec,
        ],
        out_specs=out_spec,
        out_shape=out_shape,
        scratch_shapes=[
            pltpu.VMEM((lp, tq), jnp.float32),
            pltpu.VMEM((kt, tq), jnp.float32),
            pltpu.VMEM((spare, tq), jnp.float32),
            pltpu.VMEM((N_HEADS, kt, tq), jnp.float32),
            pltpu.VMEM((N_HEADS, spare, tq), jnp.float32),
            pltpu.VMEM((N_HEADS, HEAD_DIM, tq), jnp.float32),
        ],
        compiler_params=_params(("parallel", "arbitrary")),
        name=f"attn_t{t}_tq{tq}",
    )(qi, wit, q, kidx, k, vt)
    if transpose_in_kernel:
        return out
    return out.transpose(0, 2, 1).reshape(batch * t, Q_COLS)


def _merge_kernel(h_ref, c_ref, halo_ref, at_ref, ga_ref, gc_ref, cw_ref, cb_ref, lng_ref, lnb_ref,
                  wco_ref, wao_ref, wo_ref, o_ref, win_ref, shift_ref, *, seq_len):
    tm = c_ref.shape[0]
    n_chunk = tm // CHUNK
    lead = _HALO - (CONV_WIDTH - 1)
    dcs = []
    for j in range(n_chunk):
        if seq_len == CHUNK:
            history = halo_ref[j]
        elif j == 0:
            starts_sequence = (pl.program_id(0) * tm) % seq_len == 0
            history = jnp.where(starts_sequence, 0.0, halo_ref[...])
        else:
            history = c_ref[j * CHUNK - _HALO:j * CHUNK, :]
        win_ref[j, 0:_HALO, :] = history
        win_ref[j, _HALO:_HALO + CHUNK, :] = c_ref[j * CHUNK:(j + 1) * CHUNK, :]
        for s in range(1, SUBLANES):
            shift_ref[j, s - 1] = win_ref[j, s:s + shift_ref.shape[2], :]
        acc = jnp.zeros((CHUNK, CONV_CH), jnp.float32)
        for tap in range(CONV_WIDTH):
            s, base = (lead + tap) % SUBLANES, (lead + tap) // SUBLANES * SUBLANES
            rows = win_ref[j, base:base + CHUNK, :] if s == 0 else shift_ref[j, s - 1, base:base + CHUNK, :]
            acc = acc + rows * cw_ref[tap:tap + 1, :]
        dcs.append(acc)
    dc = jnp.concatenate(dcs, axis=0) + cb_ref[...]
    mu = jnp.mean(dc, axis=-1, keepdims=True)
    var = jnp.mean(jnp.square(dc - mu), axis=-1, keepdims=True)
    y = (dc - mu) * lax.rsqrt(var + EPS) * lng_ref[...] + lnb_ref[...]
    conv_out = _dot((y * jax.nn.sigmoid(y)).astype(wco_ref.dtype), wco_ref[...])
    attn_out = _dot(at_ref[...], wao_ref[...])
    merged = ga_ref[...] * attn_out + gc_ref[...] * conv_out
    o_ref[...] = h_ref[...] + _dot(merged.astype(wo_ref.dtype), wo_ref[...])


def _merge(h, c, state, attn, ga, gc, weights, tm, seq_len):
    n = h.shape[0]
    cw, conv_b, ln_g, ln_b, wco, wao, wo = weights

    def row(width):
        return pl.BlockSpec((tm, width), lambda i: (i, 0))

    if seq_len == CHUNK:
        halo, halo_spec = state, pl.BlockSpec((tm // CHUNK, _HALO, CONV_CH), lambda i: (i, 0, 0))
    else:
        assert state is None and seq_len % tm == 0
        per_tile = tm // _HALO
        halo, halo_spec = c, pl.BlockSpec((_HALO, CONV_CH), lambda i: (jnp.maximum(i * per_tile - 1, 0), 0))
    vec = _resident((1, CONV_CH))
    return pl.pallas_call(
        functools.partial(_merge_kernel, seq_len=seq_len),
        grid=(n // tm,),
        in_specs=[
            row(D_MODEL), row(CONV_CH), halo_spec,
            row(Q_COLS), row(D_MODEL), row(D_MODEL),
            _resident(cw.shape), vec, vec, vec,
            _resident((CONV_CH, D_MODEL)), _resident((Q_COLS, D_MODEL)), _resident((D_MODEL, D_MODEL)),
        ],
        out_specs=row(D_MODEL),
        out_shape=jax.ShapeDtypeStruct((n, D_MODEL), jnp.float32),
        scratch_shapes=[pltpu.VMEM((tm // CHUNK, _HALO + CHUNK, CONV_CH), jnp.float32),
                        pltpu.VMEM((tm // CHUNK, SUBLANES - 1, _HALO + CHUNK - SUBLANES, CONV_CH), jnp.float32)],
        compiler_params=_params(("parallel",)),
        name=f"merge_n{n}",
    )(h, c, halo, attn, ga, gc, cw, conv_b, ln_g, ln_b, wco, wao, wo)


def _row_tile(n, candidates):
    for tm in candidates:
        if n % tm == 0:
            return tm
    raise ValueError(f"row count {n} is not a multiple of {candidates[-1]}")


def _rope_tables(pos, rows):
    inv = ROPE_THETA ** (-jnp.arange(0, HEAD_DIM, 2, dtype=jnp.float32) / HEAD_DIM)
    ang = pos.astype(jnp.float32)[:, None] * inv[None, :]
    cos, sin = jnp.cos(ang), jnp.sin(ang)
    reps = rows // pos.shape[0]
    return (jnp.tile(cos, (reps, 4)), jnp.tile(jnp.concatenate([-sin, sin], axis=1), (reps, 2)))


def _main_weight(w_in):
    sizes = (Q_COLS, KV_COLS, KV_COLS, IDXQ_COLS, IDX_DIM, IDX_HEADS, 2 * CONV_CH, 2 * D_MODEL)
    offs = [0]
    for s in sizes:
        offs.append(offs[-1] + s)
    wq, wk, wv, wqi, wki, wwi, wconv, wgate = (w_in[:, offs[i]:offs[i + 1]] for i in range(len(sizes)))
    pad = jnp.zeros((D_MODEL, LANES - IDX_DIM - IDX_HEADS), w_in.dtype)
    return jnp.concatenate([wq, wqi, wk, wki, wwi, pad, wv, wconv, wgate], axis=1).astype(_MXU_DTYPE)


def _layer(x, caches, weights, final_norm):
    (ffn1_norm, ffn1_w, mix_norm, w_main, b_gate, merge_w, ffn2_norm, ffn2_w) = weights
    b, t, _ = x.shape
    n = b * t
    assert t % CHUNK == 0
    tm = _row_tile(n, (512, 256, 128, 64))
    tm_merge = _row_tile(n, (256, 128, 64))
    past = 0 if caches is None else caches[0].shape[1]

    h = _ffn(x.reshape(n, D_MODEL), ffn1_norm, ffn1_w, None, tm)
    if t >= tm:
        assert t % tm == 0
    else:
        assert tm % t == 0
    cos, sin = _rope_tables(past + jnp.arange(t, dtype=jnp.int32), max(t, tm))
    q, qi, k, ki, v, c, ga, gc, kb, kib, vt, wit = _proj(h, mix_norm, w_main, b_gate, cos, sin, tm)

    if caches is None:
        tq = _row_tile(t, (256, 128, 64))
        attn = _attn(qi, wit, q, kib.reshape(b, t, IDX_DIM), kb.reshape(b, t, KV_COLS), vt,
                     seq_len=t, past=0, tq=tq, kt=tq)
        state = None
        conv_tail = c.reshape(b, t, CONV_CH)[:, t - (CONV_WIDTH - 1):]
    else:
        cache_k, cache_v, cache_idx_k, state_conv = caches
        n_keys = past + t
        pad_keys = -n_keys % LANES

        def with_cache(cache, new):
            cache = cache.reshape(b, past, -1).astype(_MXU_DTYPE)
            pad = jnp.zeros((b, pad_keys, cache.shape[2]), _MXU_DTYPE)
            return jnp.concatenate([cache, new.reshape(b, t, -1), pad], axis=1)

        vt_all = with_cache(cache_v, vt.reshape(KV_COLS, b, t).transpose(1, 2, 0)).transpose(0, 2, 1)
        side_by_side = next(g for g in (4, 2, 1) if b % g == 0)
        attn = _attn(qi, wit, q, with_cache(cache_idx_k, kib), with_cache(cache_k, kb), vt_all,
                     seq_len=t, past=past, tq=side_by_side * t, kt=n_keys + pad_keys)
        lead = jnp.zeros((b, _HALO - (CONV_WIDTH - 1), CONV_CH), jnp.float32)
        state = jnp.concatenate([lead, state_conv], axis=1)
        conv_tail = jnp.concatenate([state_conv, c.reshape(b, t, CONV_CH)], axis=1)[:, -(CONV_WIDTH - 1):]
    if t != CHUNK:
        assert state is None, "a carried conv state is supported for 64-row sequences only"

    h2 = _merge(h, c, state, attn, ga, gc, merge_w, tm_merge, t)
    y = _ffn(h2, ffn2_norm, ffn2_w, final_norm, tm)
    return (y.reshape(b, t, D_MODEL),
            k.reshape(1, b, t, N_KV_HEADS, HEAD_DIM), v.reshape(1, b, t, N_KV_HEADS, HEAD_DIM),
            ki.reshape(1, b, t, IDX_DIM), conv_tail[None])


def kernel(x_prompt, x_sample, cache_k, cache_v, cache_idx_k, state_conv, ffn1_norm, ffn1_w_in, ffn1_w_out, mix_norm, w_in, b_gate, conv_w, conv_b, conv_ln_g, conv_ln_b, conv_w_out, attn_w_out, w_out, ffn2_norm, ffn2_w_in, ffn2_w_out, final_norm):
    assert ffn1_norm.shape[0] == 1, "one layer"
    assert x_prompt.shape[1] >= CONV_WIDTH - 1
    conv_taps = jnp.zeros((_HALO, CONV_CH), jnp.float32).at[:CONV_WIDTH].set(conv_w[0])
    merge_w = (conv_taps, conv_b[0].reshape(1, CONV_CH), conv_ln_g[0].reshape(1, CONV_CH),
               conv_ln_b[0].reshape(1, CONV_CH), conv_w_out[0].astype(_MXU_DTYPE),
               attn_w_out[0].astype(_MXU_DTYPE), w_out[0].astype(_MXU_DTYPE))
    weights = (ffn1_norm[0], _ffn_weights(ffn1_w_in[0], ffn1_w_out[0]), mix_norm[0],
               _main_weight(w_in[0]), b_gate[0], merge_w,
               ffn2_norm[0], _ffn_weights(ffn2_w_in[0], ffn2_w_out[0]))
    y_p, k_p, v_p, ki_p, conv_p = _layer(x_prompt, None, weights, final_norm)
    y_s, k_s, v_s, ki_s, conv_s = _layer(
        x_sample, (cache_k[0], cache_v[0], cache_idx_k[0], state_conv[0]), weights, final_norm)
    return (y_p, y_s, k_p, v_p, ki_p, conv_p, k_s, v_s, ki_s, conv_s)
```

```python
import functools
import math

import jax
import jax.numpy as jnp
from jax import lax
from jax.experimental import pallas as pl
from jax.experimental.pallas import tpu as pltpu

D_MODEL = 1024
CHUNK = 64
N_HEADS = 8
N_KV_HEADS = 2
HEAD_DIM = 64
GROUP = N_HEADS // N_KV_HEADS
IDX_HEADS = 8
IDX_DIM = 64
TOPK_MAX = 256
CONV_CH = 512
CONV_WIDTH = 31
D_FF = 2816
ROPE_THETA = 10000.0
EPS = 1e-6

Q_COLS = N_HEADS * HEAD_DIM
KV_COLS = N_KV_HEADS * HEAD_DIM
IDXQ_COLS = IDX_HEADS * IDX_DIM

LANES = 128
SUBLANES = 8
_MXU_DTYPE = jnp.bfloat16
_VMEM_LIMIT = 56 * 1024 * 1024
_FF_CHUNK = 256
_HALO = 32
_NEG = -1e30
_Q_SCALE = HEAD_DIM ** -0.5 * math.log2(math.e)

_C_Q = 0
_C_QI = _C_Q + Q_COLS
_C_K = _C_QI + IDXQ_COLS
_C_KW = _C_K + KV_COLS
_C_V = _C_KW + LANES
_C_CA = _C_V + KV_COLS
_C_CB = _C_CA + CONV_CH
_C_GA = _C_CB + CONV_CH
_C_GC = _C_GA + D_MODEL
_C_END = _C_GC + D_MODEL

_KEY_NEG_FLT_MAX = -2139095040
_KEY_POS_INF = 0x7F800000
_KEY_MIN_NORMAL = 0x00800000
_MAX_SEARCH_PASSES = 96


def _params(sem):
    return pltpu.CompilerParams(dimension_semantics=sem, vmem_limit_bytes=_VMEM_LIMIT)


def _resident(shape):
    nd = len(shape)
    return pl.BlockSpec(shape, lambda *_: (0,) * nd, pipeline_mode=pl.Buffered(1))


def _rms(x, g):
    return x * lax.rsqrt(jnp.mean(x * x, axis=-1, keepdims=True) + EPS) * g


def _dot(a, b):
    return jnp.dot(a, b, preferred_element_type=jnp.float32)


def _dot_nt(a, b):
    return lax.dot_general(a, b, (((1,), (1,)), ((), ())), preferred_element_type=jnp.float32)


def _ffn_kernel(*refs, final_norm):
    if final_norm:
        x_ref, g_ref, wa_ref, wb_ref, wo_ref, gf_ref, o_ref = refs
    else:
        x_ref, g_ref, wa_ref, wb_ref, wo_ref, o_ref = refs
    x = x_ref[...]
    u = _rms(x, g_ref[...]).astype(wa_ref.dtype)
    acc = jnp.zeros_like(x)
    for c in range(D_FF // _FF_CHUNK):
        sl = slice(c * _FF_CHUNK, (c + 1) * _FF_CHUNK)
        a = _dot(u, wa_ref[:, sl])
        b = _dot(u, wb_ref[:, sl])
        act = (a * jax.nn.sigmoid(a) * b).astype(wo_ref.dtype)
        acc = acc + _dot(act, wo_ref[sl, :])
    h = x + 0.5 * acc
    if final_norm:
        h = _rms(h, gf_ref[...])
    o_ref[...] = h


def _ffn_weights(w_in, w_out):
    return w_in[:, :D_FF].astype(_MXU_DTYPE), w_in[:, D_FF:].astype(_MXU_DTYPE), w_out.astype(_MXU_DTYPE)


def _ffn(x, g, weights, g_final, tm):
    n = x.shape[0]
    wa, wb, wo = weights
    row = pl.BlockSpec((tm, D_MODEL), lambda i: (i, 0))
    vec = _resident((1, D_MODEL))
    in_specs = [row, vec, _resident(wa.shape), _resident(wb.shape), _resident(wo.shape)]
    args = [x, g.reshape(1, D_MODEL), wa, wb, wo]
    if g_final is not None:
        in_specs.append(vec)
        args.append(g_final.reshape(1, D_MODEL))
    return pl.pallas_call(
        functools.partial(_ffn_kernel, final_norm=g_final is not None),
        grid=(n // tm,),
        in_specs=in_specs,
        out_specs=row,
        out_shape=jax.ShapeDtypeStruct((n, D_MODEL), jnp.float32),
        compiler_params=_params(("parallel",)),
        name=f"ffn_final_n{n}" if g_final is not None else f"ffn_n{n}",
    )(*args)


def _swap_halves(x):
    lane = lax.broadcasted_iota(jnp.int32, x.shape, 1)
    first = (lane & (HEAD_DIM - 1)) < HEAD_DIM // 2
    return jnp.where(first, pltpu.roll(x, LANES - HEAD_DIM // 2, 1), pltpu.roll(x, HEAD_DIM // 2, 1))


def _rope(x, cos, sin_signed):
    tiles = []
    for j in range(x.shape[1] // LANES):
        xt = x[:, j * LANES:(j + 1) * LANES]
        tiles.append(xt * cos + _swap_halves(xt) * sin_signed)
    return tiles[0] if len(tiles) == 1 else jnp.concatenate(tiles, axis=1)


def _proj_kernel(h_ref, g_ref, w_ref, bg_ref, cos_ref, sin_ref,
                 q_ref, qi_ref, k_ref, ki_ref, v_ref, c_ref, ga_ref, gc_ref,
                 kb_ref, kib_ref, vt_ref, wit_ref):
    u = _rms(h_ref[...], g_ref[...]).astype(w_ref.dtype)
    cos = cos_ref[...]
    sin = sin_ref[...]

    def proj(lo, hi):
        return _dot(u, w_ref[:, lo:hi])

    q_ref[...] = (_rope(proj(_C_Q, _C_QI), cos, sin) * _Q_SCALE).astype(q_ref.dtype)
    qi_ref[...] = _rope(proj(_C_QI, _C_K), cos, sin).astype(qi_ref.dtype)
    k = _rope(proj(_C_K, _C_KW), cos, sin)
    k_ref[...] = k
    kb_ref[...] = k.astype(kb_ref.dtype)
    zkw = proj(_C_KW, _C_V)
    ki = _rope(zkw, cos, sin)[:, :IDX_DIM]
    ki_ref[...] = ki
    kib_ref[...] = ki.astype(kib_ref.dtype)
    wit_ref[...] = (zkw * (IDX_HEADS ** -0.5)).T[IDX_DIM:IDX_DIM + IDX_HEADS, :]
    v = proj(_C_V, _C_CA)
    v_ref[...] = v
    vt_ref[...] = v.T.astype(vt_ref.dtype)
    c_ref[...] = proj(_C_CA, _C_CB) * jax.nn.sigmoid(proj(_C_CB, _C_GA))
    ga_ref[...] = jax.nn.sigmoid(proj(_C_GA, _C_GC) + bg_ref[:, :D_MODEL])
    gc_ref[...] = jax.nn.sigmoid(proj(_C_GC, _C_END) + bg_ref[:, D_MODEL:])


def _proj(h, g, w_main, b_gate, cos, sin, tm):
    n = h.shape[0]
    n_tab = cos.shape[0] // tm

    def row(width):
        return pl.BlockSpec((tm, width), lambda i: (i, 0))

    def col(height):
        return pl.BlockSpec((height, tm), lambda i: (0, i))

    table = pl.BlockSpec((tm, LANES), lambda i: (i % n_tab, 0))
    f32 = jnp.float32
    out_shape = (
        jax.ShapeDtypeStruct((n, Q_COLS), _MXU_DTYPE),
        jax.ShapeDtypeStruct((n, IDXQ_COLS), _MXU_DTYPE),
        jax.ShapeDtypeStruct((n, KV_COLS), f32),
        jax.ShapeDtypeStruct((n, IDX_DIM), f32),
        jax.ShapeDtypeStruct((n, KV_COLS), f32),
        jax.ShapeDtypeStruct((n, CONV_CH), f32),
        jax.ShapeDtypeStruct((n, D_MODEL), f32),
        jax.ShapeDtypeStruct((n, D_MODEL), f32),
        jax.ShapeDtypeStruct((n, KV_COLS), _MXU_DTYPE),
        jax.ShapeDtypeStruct((n, IDX_DIM), _MXU_DTYPE),
        jax.ShapeDtypeStruct((KV_COLS, n), _MXU_DTYPE),
        jax.ShapeDtypeStruct((IDX_HEADS, n), f32),
    )
    out_specs = tuple(row(s.shape[1]) for s in out_shape[:10]) + (col(KV_COLS), col(IDX_HEADS))
    return pl.pallas_call(
        _proj_kernel,
        grid=(n // tm,),
        in_specs=[row(D_MODEL), _resident((1, D_MODEL)), _resident(w_main.shape),
                  _resident((1, 2 * D_MODEL)), table, table],
        out_specs=out_specs,
        out_shape=out_shape,
        compiler_params=_params(("parallel",)),
        name=f"proj_n{n}",
    )(h, g.reshape(1, D_MODEL), w_main, b_gate.reshape(1, 2 * D_MODEL), cos, sin)


def _float_to_key(x):
    bits = lax.bitcast_convert_type(x, jnp.int32)
    return jnp.where(bits >= 0, bits, bits ^ 0x7FFFFFFF)


def _key_to_float(key):
    return lax.bitcast_convert_type(jnp.where(key >= 0, key, key ^ 0x7FFFFFFF), jnp.float32)


def _normal_quantile(p):
    tail = jnp.minimum(p, 1.0 - p)
    t = jnp.sqrt(-2.0 * jnp.log(tail))
    z = t - (2.515517 + t * (0.802853 + t * 0.010328)) / (1.0 + t * (1.432788 + t * (0.189269 + t * 0.001308)))
    return jnp.where(p < 0.5, z, -z)


def _kth_largest(s_ref, count_ge, n_valid, top, kt):
    cols = s_ref.shape[1]
    shape = (1, cols)
    head = s_ref[0:min(kt, 256), :]
    mean = head.mean(axis=0, keepdims=True)
    std = jnp.sqrt(jnp.maximum((head * head).mean(axis=0, keepdims=True) - mean * mean, 1e-30))
    guess = mean + std * _normal_quantile(jnp.minimum((top - 0.5) / jnp.maximum(n_valid, 1.0), 0.999))
    log_top = jnp.log(top - 0.5)

    def adjacent(lo_k, hi_k):
        return (hi_k <= lo_k + 1) | ((lo_k == 0) & (hi_k == _KEY_MIN_NORMAL))

    def active(state):
        _, lo_k, hi_k, clo = state[:4]
        return (clo > top) & jnp.logical_not(adjacent(lo_k, hi_k))

    def cond(state):
        return (state[0] < _MAX_SEARCH_PASSES) & (jnp.max(jnp.where(active(state), 1.0, 0.0)) > 0.0)

    def body(state):
        it, lo_k, hi_k, clo, chi, lo_real, hi_real, run, last_up = state
        lo_v, hi_v = _key_to_float(lo_k), _key_to_float(hi_k)
        both = (lo_real > 0) & (hi_real > 0)
        log_lo = jnp.log(clo)
        frac = (log_lo - log_top) / (log_lo - jnp.log(jnp.maximum(chi, 0.5)))
        frac = jnp.where(clo - chi > 16.0, frac, 0.5)
        step = std * 0.5 * lax.shift_left(jnp.int32(1), jnp.minimum(run, 20)).astype(jnp.float32)
        t = jnp.where(both, lo_v + (hi_v - lo_v) * frac, jnp.where(lo_real > 0, lo_v + step, hi_v - step))
        t = jnp.where(it == 0, guess, t)
        t_k = _float_to_key(t)
        mid_k = (lo_k >> 1) + (hi_k >> 1) + (lo_k & hi_k & 1)
        bisect = (both & (run >= 2)) | (it >= 24)
        t_k = jnp.where(bisect, mid_k, t_k)
        t_k = jnp.where((lo_k < 0) & (hi_k > _KEY_MIN_NORMAL) & (it >= 3), 0, t_k)
        t_k = jnp.where((lo_k == 0) & (hi_k > _KEY_MIN_NORMAL), _KEY_MIN_NORMAL, t_k)
        t_k = jnp.minimum(jnp.maximum(t_k, lo_k + 1), hi_k - 1)
        c = count_ge(_key_to_float(t_k))
        live = active(state)
        up = c >= top
        go_lo, go_hi = live & up, live & jnp.logical_not(up)
        same = jnp.where(up, 1, -1) == last_up
        return (it + 1,
                jnp.where(go_lo, t_k, lo_k), jnp.where(go_hi, t_k, hi_k),
                jnp.where(go_lo, c, clo), jnp.where(go_hi, c, chi),
                jnp.where(go_lo, 1, lo_real), jnp.where(go_hi, 1, hi_real),
                jnp.where(bisect, 0, jnp.where(same, run + 1, 1)),
                jnp.where(up, 1, -1))

    zeros = jnp.zeros(shape, jnp.int32)
    init = (jnp.int32(0), jnp.full(shape, _KEY_NEG_FLT_MAX, jnp.int32), jnp.full(shape, _KEY_POS_INF, jnp.int32),
            n_valid, jnp.zeros(shape, jnp.float32), zeros, zeros, zeros, zeros)
    out = lax.while_loop(cond, body, init)
    return _key_to_float(out[1]), out[3]


def _count_ge(ref, x, n_kt, kt):
    cols = ref.shape[1]

    def tile(j, acc):
        blk = ref[pl.ds(pl.multiple_of(j * kt, kt), kt), :]
        ones = jnp.where(blk >= x, 1.0, 0.0)
        return acc + ones.reshape(kt // 8, 8, cols).sum(axis=0)

    def pair(jj, acc):
        return tile(2 * jj + 1, tile(2 * jj, acc))

    acc = lax.fori_loop(0, n_kt // 2, pair, jnp.zeros((8, cols), jnp.float32))
    acc = lax.cond(n_kt % 2 == 1, lambda a: tile(n_kt - 1, a), lambda a: a, acc)
    return acc.sum(axis=0, keepdims=True)


def _attn_kernel(qi_ref, wit_ref, q_ref, kidx_ref, k_ref, vt_ref, o_ref,
                 s_ref, bias0_ref, bias1_ref, lg0_ref, lg1_ref, acc_ref, *, past, tq, kt, top, seq_q):
    n_seq = tq // seq_q
    t0 = pl.program_id(1) * seq_q
    n_kt = (past + t0 + seq_q + kt - 1) // kt
    lane_q = lax.broadcasted_iota(jnp.int32, (1, tq), 1)
    limit = past + ((t0 + lane_q % seq_q) // CHUNK + 1) * CHUNK
    w = wit_ref[...]

    def keys_times_queries(keys_ref, rows, lanes, queries_ref, cols):
        parts = [_dot_nt(keys_ref[g, rows, lanes], queries_ref[g * seq_q:(g + 1) * seq_q, cols])
                 for g in range(n_seq)]
        return parts[0] if n_seq == 1 else jnp.concatenate(parts, axis=1)

    def score_rows(off, rows, last):
        acc = jnp.zeros((rows, tq), jnp.float32)
        for h in range(IDX_HEADS):
            s = keys_times_queries(kidx_ref, pl.ds(off, rows), slice(None), qi_ref,
                                   slice(h * IDX_DIM, (h + 1) * IDX_DIM))
            acc = acc + w[h:h + 1, :] * jnp.maximum(s, 0.0)
        if last:
            key = off + lax.broadcasted_iota(jnp.int32, (rows, tq), 0)
            acc = jnp.where(key < limit, acc, -jnp.inf)
        s_ref[pl.ds(off, rows), :] = acc

    def score_pair(jj, carry):
        score_rows(pl.multiple_of(jj * 2 * kt, 2 * kt), 2 * kt, False)
        return carry

    n_full = n_kt - 1
    lax.fori_loop(0, n_full // 2, score_pair, 0)

    @pl.when(n_full % 2 == 1)
    def _():
        score_rows(pl.multiple_of((n_full - 1) * kt, kt), kt, False)

    score_rows(pl.multiple_of(n_full * kt, kt), kt, True)

    def count_ge(thr):
        return _count_ge(s_ref, thr, n_kt, kt)

    thr, cnt = _kth_largest(s_ref, count_ge, limit.astype(jnp.float32), top, kt)

    def resolve_ties(cnt):
        excess = cnt - top
        row = lax.broadcasted_iota(jnp.int32, (kt, kt), 0)
        col = lax.broadcasted_iota(jnp.int32, (kt, kt), 1)
        at_or_after = jnp.where(col >= row, 1.0, 0.0).astype(_MXU_DTYPE)

        def tile(jr, after):
            rows = pl.ds(pl.multiple_of((n_kt - 1 - jr) * kt, kt), kt)
            blk = s_ref[rows, :]
            equal = blk == thr
            suffix = after + _dot(at_or_after, jnp.where(equal, 1.0, 0.0).astype(_MXU_DTYPE))
            s_ref[rows, :] = jnp.where(equal, jnp.where(suffix <= excess, -jnp.inf, blk), blk)
            return suffix[0:1, :]

        lax.fori_loop(0, n_kt, tile, jnp.zeros((1, tq), jnp.float32))
        return count_ge(thr)

    cnt = lax.cond(jnp.max(cnt) > top, resolve_ties, lambda c: c, cnt)

    def drop_lowest(cnt):
        over = cnt > top

        def tile(j):
            off = pl.multiple_of(j * kt, kt)
            return off, s_ref[pl.ds(off, kt), :]

        def min_body(j, vmin):
            _, blk = tile(j)
            cand = jnp.where(blk >= thr, blk, jnp.inf)
            return jnp.minimum(vmin, cand.reshape(kt // 8, 8, tq).min(axis=0))

        vmin = lax.fori_loop(0, n_kt, min_body, jnp.full((8, tq), jnp.inf, jnp.float32))
        vmin = vmin.min(axis=0, keepdims=True)

        def idx_body(j, imax):
            off, blk = tile(j)
            key_idx = off + lax.broadcasted_iota(jnp.int32, (kt, tq), 0)
            cand = jnp.where(blk == vmin, key_idx, -1)
            return jnp.maximum(imax, cand.reshape(kt // 8, 8, tq).max(axis=0))

        imax = lax.fori_loop(0, n_kt, idx_body, jnp.full((8, tq), -1, jnp.int32))
        kill = jnp.where(over, imax.max(axis=0, keepdims=True), -1)

        def kill_body(j, carry):
            off, blk = tile(j)
            key_idx = off + lax.broadcasted_iota(jnp.int32, (kt, tq), 0)
            s_ref[pl.ds(off, kt), :] = jnp.where(key_idx == kill, -jnp.inf, blk)
            return carry

        lax.fori_loop(0, n_kt, kill_body, 0)
        return jnp.where(over, cnt - 1.0, cnt)

    lax.while_loop(lambda cnt: jnp.max(cnt) > top, drop_lowest, cnt)

    acc_ref[...] = jnp.zeros(acc_ref.shape, jnp.float32)

    def logits_sweep(j, m, bias_buf, lg_buf):
        off = pl.multiple_of(jnp.minimum(j, n_kt - 1) * kt, kt)
        bias_buf[...] = jnp.where(s_ref[pl.ds(off, kt), :] >= jnp.where(j < n_kt, thr, jnp.inf), 0.0, _NEG)
        m_new = []
        for h in range(N_HEADS):
            c = h // GROUP
            lg = keys_times_queries(k_ref, pl.ds(off, kt), slice(c * HEAD_DIM, (c + 1) * HEAD_DIM), q_ref,
                                    slice(h * HEAD_DIM, (h + 1) * HEAD_DIM)) + bias_buf[...]
            lg_buf[h] = lg
            m_new.append(jnp.maximum(m[h], lg.max(axis=0, keepdims=True)))
        return tuple(m_new)

    def softmax_sweep(j, m_old, m_new, l, lg_buf):
        off = pl.multiple_of(jnp.minimum(j, n_kt - 1) * kt, kt)
        l_new = []
        for h in range(N_HEADS):
            c = h // GROUP
            alpha = jnp.exp2(m_old[h] - m_new[h])
            p = jnp.exp2(lg_buf[h] - m_new[h])
            pv = [_dot(vt_ref[g, c * HEAD_DIM:(c + 1) * HEAD_DIM, pl.ds(off, kt)],
                       p[:, g * seq_q:(g + 1) * seq_q].astype(vt_ref.dtype)) for g in range(n_seq)]
            acc_ref[h] = alpha * acc_ref[h] + (pv[0] if n_seq == 1 else jnp.concatenate(pv, axis=1))
            l_new.append(alpha * l[h] + p.sum(axis=0, keepdims=True))
        return tuple(l_new)

    def attn_step(jj, carry):
        m_a, m_b, l = carry
        m_c = logits_sweep(2 * jj + 1, m_b, bias1_ref, lg1_ref)
        l = softmax_sweep(2 * jj, m_a, m_b, l, lg0_ref)
        m_d = logits_sweep(2 * jj + 2, m_c, bias0_ref, lg0_ref)
        l = softmax_sweep(2 * jj + 1, m_b, m_c, l, lg1_ref)
        return m_c, m_d, l

    m_init = tuple(jnp.full((1, tq), _NEG, jnp.float32) for _ in range(N_HEADS))
    l_init = tuple(jnp.zeros((1, tq), jnp.float32) for _ in range(N_HEADS))
    m_first = logits_sweep(0, m_init, bias0_ref, lg0_ref)
    if kt == s_ref.shape[0]:
        l = softmax_sweep(0, m_init, m_first, l_init, lg0_ref)
    else:
        _, _, l = lax.fori_loop(0, (n_kt + 1) // 2, attn_step, (m_init, m_first, l_init))
    if o_ref.shape[0] == tq:
        o = jnp.concatenate([acc_ref[h] / l[h] for h in range(N_HEADS)], axis=0)
        o_ref[...] = o.T.astype(o_ref.dtype)
    else:
        for h in range(N_HEADS):
            o_ref[h * HEAD_DIM:(h + 1) * HEAD_DIM, :] = (acc_ref[h] / l[h]).astype(o_ref.dtype)


def _attn(qi, wit, q, kidx, k, vt, *, seq_len, past, tq, kt):
    batch, lp = k.shape[0], k.shape[1]
    t = seq_len
    n_seq = max(1, tq // t)
    seq_q = tq // n_seq
    nq = t // seq_q
    n_keys = past + t
    assert batch % n_seq == 0 and t % seq_q == 0 and seq_q % CHUNK == 0 and lp % kt == 0 and lp >= n_keys
    assert t == seq_q or (past == 0 and kt == tq)
    top = min(TOPK_MAX, n_keys // 4)

    def per_sequence(width_last, width_mid):
        return pl.BlockSpec((n_seq, width_mid, width_last), lambda bi, i: (bi, 0, 0))

    if vt.ndim == 2:
        assert n_seq == 1 and lp == t
        vt = vt.reshape(1, KV_COLS, batch * t)
        vt_spec = pl.BlockSpec((1, KV_COLS, lp), lambda bi, i: (0, 0, bi))
    else:
        vt_spec = per_sequence(lp, KV_COLS)
    transpose_in_kernel = tq % LANES == 0
    if transpose_in_kernel:
        out_shape = jax.ShapeDtypeStruct((batch * t, Q_COLS), _MXU_DTYPE)
        out_spec = pl.BlockSpec((tq, Q_COLS), lambda bi, i: (bi * nq + i, 0))
        wit_spec = pl.BlockSpec((IDX_HEADS, tq), lambda bi, i: (0, bi * nq + i))
    else:
        assert n_seq == 1
        out_shape = jax.ShapeDtypeStruct((batch, Q_COLS, t), _MXU_DTYPE)
        out_spec = pl.BlockSpec((None, Q_COLS, tq), lambda bi, i: (bi, 0, i))
        wit = wit.reshape(IDX_HEADS, batch, t).transpose(1, 0, 2)
        wit_spec = pl.BlockSpec((None, IDX_HEADS, tq), lambda bi, i: (bi, 0, i))
    single_tile = kt == lp
    spare = SUBLANES if single_tile else kt
    out = pl.pallas_call(
        functools.partial(_attn_kernel, past=past, tq=tq, kt=kt, top=top, seq_q=seq_q),
        grid=(batch // n_seq, nq),
        in_specs=[
            pl.BlockSpec((tq, IDXQ_COLS), lambda bi, i: (bi * nq + i, 0)),
            wit_spec,
            pl.BlockSpec((tq, Q_COLS), lambda bi, i: (bi * nq + i, 0)),
            per_sequence(IDX_DIM, lp), per_sequence(KV_COLS, lp), vt_spec,
        ],
        out_specs=out_spec,
        out_shape=out_shape,
        scratch_shapes=[
            pltpu.VMEM((lp, tq), jnp.float32),
            pltpu.VMEM((kt, tq), jnp.float32),
            pltpu.VMEM((spare, tq), jnp.float32),
            pltpu.VMEM((N_HEADS, kt, tq), jnp.float32),
            pltpu.VMEM((N_HEADS, spare, tq), jnp.float32),
            pltpu.VMEM((N_HEADS, HEAD_DIM, tq), jnp.float32),
        ],
        compiler_params=_params(("parallel", "arbitrary")),
        name=f"attn_t{t}_tq{tq}",
    )(qi, wit, q, kidx, k, vt)
    if transpose_in_kernel:
        return out
    return out.transpose(0, 2, 1).reshape(batch * t, Q_COLS)


def _merge_kernel(h_ref, c_ref, halo_ref, at_ref, ga_ref, gc_ref, cw_ref, cb_ref, lng_ref, lnb_ref,
                  wco_ref, wao_ref, wo_ref, o_ref, win_ref, shift_ref, *, seq_len):
    tm = c_ref.shape[0]
    n_chunk = tm // CHUNK
    lead = _HALO - (CONV_WIDTH - 1)
    dcs = []
    for j in range(n_chunk):
        if seq_len == CHUNK:
            history = halo_ref[j]
        elif j == 0:
            starts_sequence = (pl.program_id(0) * tm) % seq_len == 0
            history = jnp.where(starts_sequence, 0.0, halo_ref[...])
        else:
            history = c_ref[j * CHUNK - _HALO:j * CHUNK, :]
        win_ref[j, 0:_HALO, :] = history
        win_ref[j, _HALO:_HALO + CHUNK, :] = c_ref[j * CHUNK:(j + 1) * CHUNK, :]
        for s in range(1, SUBLANES):
            shift_ref[j, s - 1] = win_ref[j, s:s + shift_ref.shape[2], :]
        acc = jnp.zeros((CHUNK, CONV_CH), jnp.float32)
        for tap in range(CONV_WIDTH):
            s, base = (lead + tap) % SUBLANES, (lead + tap) // SUBLANES * SUBLANES
            rows = win_ref[j, base:base + CHUNK, :] if s == 0 else shift_ref[j, s - 1, base:base + CHUNK, :]
            acc = acc + rows * cw_ref[tap:tap + 1, :]
        dcs.append(acc)
    dc = jnp.concatenate(dcs, axis=0) + cb_ref[...]
    mu = jnp.mean(dc, axis=-1, keepdims=True)
    var = jnp.mean(jnp.square(dc - mu), axis=-1, keepdims=True)
    y = (dc - mu) * lax.rsqrt(var + EPS) * lng_ref[...] + lnb_ref[...]
    conv_out = _dot((y * jax.nn.sigmoid(y)).astype(wco_ref.dtype), wco_ref[...])
    attn_out = _dot(at_ref[...], wao_ref[...])
    merged = ga_ref[...] * attn_out + gc_ref[...] * conv_out
    o_ref[...] = h_ref[...] + _dot(merged.astype(wo_ref.dtype), wo_ref[...])


def _merge(h, c, state, attn, ga, gc, weights, tm, seq_len):
    n = h.shape[0]
    cw, conv_b, ln_g, ln_b, wco, wao, wo = weights

    def row(width):
        return pl.BlockSpec((tm, width), lambda i: (i, 0))

    if seq_len == CHUNK:
        halo, halo_spec = state, pl.BlockSpec((tm // CHUNK, _HALO, CONV_CH), lambda i: (i, 0, 0))
    else:
        assert state is None and seq_len % tm == 0
        per_tile = tm // _HALO
        halo, halo_spec = c, pl.BlockSpec((_HALO, CONV_CH), lambda i: (jnp.maximum(i * per_tile - 1, 0), 0))
    vec = _resident((1, CONV_CH))
    return pl.pallas_call(
        functools.partial(_merge_kernel, seq_len=seq_len),
        grid=(n // tm,),
        in_specs=[
            row(D_MODEL), row(CONV_CH), halo_spec,
            row(Q_COLS), row(D_MODEL), row(D_MODEL),
            _resident(cw.shape), vec, vec, vec,
            _resident((CONV_CH, D_MODEL)), _resident((Q_COLS, D_MODEL)), _resident((D_MODEL, D_MODEL)),
        ],
        out_specs=row(D_MODEL),
        out_shape=jax.ShapeDtypeStruct((n, D_MODEL), jnp.float32),
        scratch_shapes=[pltpu.VMEM((tm // CHUNK, _HALO + CHUNK, CONV_CH), jnp.float32),
                        pltpu.VMEM((tm // CHUNK, SUBLANES - 1, _HALO + CHUNK - SUBLANES, CONV_CH), jnp.float32)],
        compiler_params=_params(("parallel",)),
        name=f"merge_n{n}",
    )(h, c, halo, attn, ga, gc, cw, conv_b, ln_g, ln_b, wco, wao, wo)


def _row_tile(n, candidates):
    for tm in candidates:
        if n % tm == 0:
            return tm
    raise ValueError(f"row count {n} is not a multiple of {candidates[-1]}")


def _rope_tables(pos, rows):
    inv = ROPE_THETA ** (-jnp.arange(0, HEAD_DIM, 2, dtype=jnp.float32) / HEAD_DIM)
    ang = pos.astype(jnp.float32)[:, None] * inv[None, :]
    cos, sin = jnp.cos(ang), jnp.sin(ang)
    reps = rows // pos.shape[0]
    return (jnp.tile(cos, (reps, 4)), jnp.tile(jnp.concatenate([-sin, sin], axis=1), (reps, 2)))


def _main_weight(w_in):
    sizes = (Q_COLS, KV_COLS, KV_COLS, IDXQ_COLS, IDX_DIM, IDX_HEADS, 2 * CONV_CH, 2 * D_MODEL)
    offs = [0]
    for s in sizes:
        offs.append(offs[-1] + s)
    wq, wk, wv, wqi, wki, wwi, wconv, wgate = (w_in[:, offs[i]:offs[i + 1]] for i in range(len(sizes)))
    pad = jnp.zeros((D_MODEL, LANES - IDX_DIM - IDX_HEADS), w_in.dtype)
    return jnp.concatenate([wq, wqi, wk, wki, wwi, pad, wv, wconv, wgate], axis=1).astype(_MXU_DTYPE)


def _layer(x, caches, weights, final_norm):
    (ffn1_norm, ffn1_w, mix_norm, w_main, b_gate, merge_w, ffn2_norm, ffn2_w) = weights
    b, t, _ = x.shape
    n = b * t
    assert t % CHUNK == 0
    tm = _row_tile(n, (512, 256, 128, 64))
    tm_merge = _row_tile(n, (256, 128, 64))
    past = 0 if caches is None else caches[0].shape[1]

    h = _ffn(x.reshape(n, D_MODEL), ffn1_norm, ffn1_w, None, tm)
    if t >= tm:
        assert t % tm == 0
    else:
        assert tm % t == 0
    cos, sin = _rope_tables(past + jnp.arange(t, dtype=jnp.int32), max(t, tm))
    q, qi, k, ki, v, c, ga, gc, kb, kib, vt, wit = _proj(h, mix_norm, w_main, b_gate, cos, sin, tm)

    if caches is None:
        tq = _row_tile(t, (512, 256, 128, 64))
        attn = _attn(qi, wit, q, kib.reshape(b, t, IDX_DIM), kb.reshape(b, t, KV_COLS), vt,
                     seq_len=t, past=0, tq=tq, kt=tq)
        state = None
        conv_tail = c.reshape(b, t, CONV_CH)[:, t - (CONV_WIDTH - 1):]
    else:
        cache_k, cache_v, cache_idx_k, state_conv = caches
        n_keys = past + t
        pad_keys = -n_keys % LANES

        def with_cache(cache, new):
            cache = cache.reshape(b, past, -1).astype(_MXU_DTYPE)
            pad = jnp.zeros((b, pad_keys, cache.shape[2]), _MXU_DTYPE)
            return jnp.concatenate([cache, new.reshape(b, t, -1), pad], axis=1)

        vt_all = with_cache(cache_v, vt.reshape(KV_COLS, b, t).transpose(1, 2, 0)).transpose(0, 2, 1)
        side_by_side = next(g for g in (4, 2, 1) if b % g == 0)
        attn = _attn(qi, wit, q, with_cache(cache_idx_k, kib), with_cache(cache_k, kb), vt_all,
                     seq_len=t, past=past, tq=side_by_side * t, kt=n_keys + pad_keys)
        lead = jnp.zeros((b, _HALO - (CONV_WIDTH - 1), CONV_CH), jnp.float32)
        state = jnp.concatenate([lead, state_conv], axis=1)
        conv_tail = jnp.concatenate([state_conv, c.reshape(b, t, CONV_CH)], axis=1)[:, -(CONV_WIDTH - 1):]
    if t != CHUNK:
        assert state is None, "a carried conv state is supported for 64-row sequences only"

    h2 = _merge(h, c, state, attn, ga, gc, merge_w, tm_merge, t)
    y = _ffn(h2, ffn2_norm, ffn2_w, final_norm, tm)
    return (y.reshape(b, t, D_MODEL),
            k.reshape(1, b, t, N_KV_HEADS, HEAD_DIM), v.reshape(1, b, t, N_KV_HEADS, HEAD_DIM),
            ki.reshape(1, b, t, IDX_DIM), conv_tail[None])


def kernel(x_prompt, x_sample, cache_k, cache_v, cache_idx_k, state_conv, ffn1_norm, ffn1_w_in, ffn1_w_out, mix_norm, w_in, b_gate, conv_w, conv_b, conv_ln_g, conv_ln_b, conv_w_out, attn_w_out, w_out, ffn2_norm, ffn2_w_in, ffn2_w_out, final_norm):
    assert ffn1_norm.shape[0] == 1, "one layer"
    assert x_prompt.shape[1] >= CONV_WIDTH - 1
    conv_taps = jnp.zeros((_HALO, CONV_CH), jnp.float32).at[:CONV_WIDTH].set(conv_w[0])
    merge_w = (conv_taps, conv_b[0].reshape(1, CONV_CH), conv_ln_g[0].reshape(1, CONV_CH),
               conv_ln_b[0].reshape(1, CONV_CH), conv_w_out[0].astype(_MXU_DTYPE),
               attn_w_out[0].astype(_MXU_DTYPE), w_out[0].astype(_MXU_DTYPE))
    weights = (ffn1_norm[0], _ffn_weights(ffn1_w_in[0], ffn1_w_out[0]), mix_norm[0],
               _main_weight(w_in[0]), b_gate[0], merge_w,
               ffn2_norm[0], _ffn_weights(ffn2_w_in[0], ffn2_w_out[0]))
    y_p, k_p, v_p, ki_p, conv_p = _layer(x_prompt, None, weights, final_norm)
    y_s, k_s, v_s, ki_s, conv_s = _layer(
        x_sample, (cache_k[0], cache_v[0], cache_idx_k[0], state_conv[0]), weights, final_norm)
    return (y_p, y_s, k_p, v_p, ki_p, conv_p, k_s, v_s, ki_s, conv_s)
```

```python
import functools
import math

import jax
import jax.numpy as jnp
from jax import lax
from jax.experimental import pallas as pl
from jax.experimental.pallas import tpu as pltpu

D_MODEL = 1024
CHUNK = 64
N_HEADS = 8
N_KV_HEADS = 2
HEAD_DIM = 64
GROUP = N_HEADS // N_KV_HEADS
IDX_HEADS = 8
IDX_DIM = 64
TOPK_MAX = 256
CONV_CH = 512
CONV_WIDTH = 31
D_FF = 2816
ROPE_THETA = 10000.0
EPS = 1e-6

Q_COLS = N_HEADS * HEAD_DIM
KV_COLS = N_KV_HEADS * HEAD_DIM
IDXQ_COLS = IDX_HEADS * IDX_DIM

LANES = 128
SUBLANES = 8
_MXU_DTYPE = jnp.bfloat16
_VMEM_LIMIT = 56 * 1024 * 1024
_FF_CHUNK = 256
_HALO = 32
_NEG = -1e30
_Q_SCALE = HEAD_DIM ** -0.5 * math.log2(math.e)

_C_Q = 0
_C_QI = _C_Q + Q_COLS
_C_K = _C_QI + IDXQ_COLS
_C_KW = _C_K + KV_COLS
_C_V = _C_KW + LANES
_C_CA = _C_V + KV_COLS
_C_CB = _C_CA + CONV_CH
_C_GA = _C_CB + CONV_CH
_C_GC = _C_GA + D_MODEL
_C_END = _C_GC + D_MODEL

_KEY_NEG_FLT_MAX = -2139095040
_KEY_POS_INF = 0x7F800000
_KEY_MIN_NORMAL = 0x00800000
_MAX_SEARCH_PASSES = 96


def _params(sem):
    return pltpu.CompilerParams(dimension_semantics=sem, vmem_limit_bytes=_VMEM_LIMIT)


def _resident(shape):
    nd = len(shape)
    return pl.BlockSpec(shape, lambda *_: (0,) * nd, pipeline_mode=pl.Buffered(1))


def _rms(x, g):
    return x * lax.rsqrt(jnp.mean(x * x, axis=-1, keepdims=True) + EPS) * g


def _dot(a, b):
    return jnp.dot(a, b, preferred_element_type=jnp.float32)


def _dot_nt(a, b):
    return lax.dot_general(a, b, (((1,), (1,)), ((), ())), preferred_element_type=jnp.float32)


def _ffn_kernel(*refs, final_norm):
    if final_norm:
        x_ref, g_ref, wa_ref, wb_ref, wo_ref, gf_ref, o_ref = refs
    else:
        x_ref, g_ref, wa_ref, wb_ref, wo_ref, o_ref = refs
    x = x_ref[...]
    u = _rms(x, g_ref[...]).astype(wa_ref.dtype)
    acc = jnp.zeros_like(x)
    for c in range(D_FF // _FF_CHUNK):
        sl = slice(c * _FF_CHUNK, (c + 1) * _FF_CHUNK)
        a = _dot(u, wa_ref[:, sl])
        b = _dot(u, wb_ref[:, sl])
        act = (a * jax.nn.sigmoid(a) * b).astype(wo_ref.dtype)
        acc = acc + _dot(act, wo_ref[sl, :])
    h = x + 0.5 * acc
    if final_norm:
        h = _rms(h, gf_ref[...])
    o_ref[...] = h


def _ffn_weights(w_in, w_out):
    return w_in[:, :D_FF].astype(_MXU_DTYPE), w_in[:, D_FF:].astype(_MXU_DTYPE), w_out.astype(_MXU_DTYPE)


def _ffn(x, g, weights, g_final, tm):
    n = x.shape[0]
    wa, wb, wo = weights
    row = pl.BlockSpec((tm, D_MODEL), lambda i: (i, 0))
    vec = _resident((1, D_MODEL))
    in_specs = [row, vec, _resident(wa.shape), _resident(wb.shape), _resident(wo.shape)]
    args = [x, g.reshape(1, D_MODEL), wa, wb, wo]
    if g_final is not None:
        in_specs.append(vec)
        args.append(g_final.reshape(1, D_MODEL))
    return pl.pallas_call(
        functools.partial(_ffn_kernel, final_norm=g_final is not None),
        grid=(n // tm,),
        in_specs=in_specs,
        out_specs=row,
        out_shape=jax.ShapeDtypeStruct((n, D_MODEL), jnp.float32),
        compiler_params=_params(("parallel",)),
        name=f"ffn_final_n{n}" if g_final is not None else f"ffn_n{n}",
    )(*args)


def _swap_halves(x):
    lane = lax.broadcasted_iota(jnp.int32, x.shape, 1)
    first = (lane & (HEAD_DIM - 1)) < HEAD_DIM // 2
    return jnp.where(first, pltpu.roll(x, LANES - HEAD_DIM // 2, 1), pltpu.roll(x, HEAD_DIM // 2, 1))


def _rope(x, cos, sin_signed):
    tiles = []
    for j in range(x.shape[1] // LANES):
        xt = x[:, j * LANES:(j + 1) * LANES]
        tiles.append(xt * cos + _swap_halves(xt) * sin_signed)
    return tiles[0] if len(tiles) == 1 else jnp.concatenate(tiles, axis=1)


def _proj_kernel(h_ref, g_ref, w_ref, bg_ref, cos_ref, sin_ref,
                 q_ref, qi_ref, k_ref, ki_ref, v_ref, c_ref, ga_ref, gc_ref,
                 kb_ref, kib_ref, vt_ref, wit_ref):
    u = _rms(h_ref[...], g_ref[...]).astype(w_ref.dtype)
    cos = cos_ref[...]
    sin = sin_ref[...]

    def proj(lo, hi):
        return _dot(u, w_ref[:, lo:hi])

    q_ref[...] = (_rope(proj(_C_Q, _C_QI), cos, sin) * _Q_SCALE).astype(q_ref.dtype)
    qi_ref[...] = _rope(proj(_C_QI, _C_K), cos, sin).astype(qi_ref.dtype)
    k = _rope(proj(_C_K, _C_KW), cos, sin)
    k_ref[...] = k
    kb_ref[...] = k.astype(kb_ref.dtype)
    zkw = proj(_C_KW, _C_V)
    ki = _rope(zkw, cos, sin)[:, :IDX_DIM]
    ki_ref[...] = ki
    kib_ref[...] = ki.astype(kib_ref.dtype)
    wit_ref[...] = (zkw * (IDX_HEADS ** -0.5)).T[IDX_DIM:IDX_DIM + IDX_HEADS, :]
    v = proj(_C_V, _C_CA)
    v_ref[...] = v
    vt_ref[...] = v.T.astype(vt_ref.dtype)
    c_ref[...] = proj(_C_CA, _C_CB) * jax.nn.sigmoid(proj(_C_CB, _C_GA))
    ga_ref[...] = jax.nn.sigmoid(proj(_C_GA, _C_GC) + bg_ref[:, :D_MODEL])
    gc_ref[...] = jax.nn.sigmoid(proj(_C_GC, _C_END) + bg_ref[:, D_MODEL:])


def _proj(h, g, w_main, b_gate, cos, sin, tm):
    n = h.shape[0]
    n_tab = cos.shape[0] // tm

    def row(width):
        return pl.BlockSpec((tm, width), lambda i: (i, 0))

    def col(height):
        return pl.BlockSpec((height, tm), lambda i: (0, i))

    table = pl.BlockSpec((tm, LANES), lambda i: (i % n_tab, 0))
    f32 = jnp.float32
    out_shape = (
        jax.ShapeDtypeStruct((n, Q_COLS), _MXU_DTYPE),
        jax.ShapeDtypeStruct((n, IDXQ_COLS), _MXU_DTYPE),
        jax.ShapeDtypeStruct((n, KV_COLS), f32),
        jax.ShapeDtypeStruct((n, IDX_DIM), f32),
        jax.ShapeDtypeStruct((n, KV_COLS), f32),
        jax.ShapeDtypeStruct((n, CONV_CH), f32),
        jax.ShapeDtypeStruct((n, D_MODEL), f32),
        jax.ShapeDtypeStruct((n, D_MODEL), f32),
        jax.ShapeDtypeStruct((n, KV_COLS), _MXU_DTYPE),
        jax.ShapeDtypeStruct((n, IDX_DIM), _MXU_DTYPE),
        jax.ShapeDtypeStruct((KV_COLS, n), _MXU_DTYPE),
        jax.ShapeDtypeStruct((IDX_HEADS, n), f32),
    )
    out_specs = tuple(row(s.shape[1]) for s in out_shape[:10]) + (col(KV_COLS), col(IDX_HEADS))
    return pl.pallas_call(
        _proj_kernel,
        grid=(n // tm,),
        in_specs=[row(D_MODEL), _resident((1, D_MODEL)), _resident(w_main.shape),
                  _resident((1, 2 * D_MODEL)), table, table],
        out_specs=out_specs,
        out_shape=out_shape,
        compiler_params=_params(("parallel",)),
        name=f"proj_n{n}",
    )(h, g.reshape(1, D_MODEL), w_main, b_gate.reshape(1, 2 * D_MODEL), cos, sin)


def _float_to_key(x):
    bits = lax.bitcast_convert_type(x, jnp.int32)
    return jnp.where(bits >= 0, bits, bits ^ 0x7FFFFFFF)


def _key_to_float(key):
    return lax.bitcast_convert_type(jnp.where(key >= 0, key, key ^ 0x7FFFFFFF), jnp.float32)


def _normal_quantile(p):
    tail = jnp.minimum(p, 1.0 - p)
    t = jnp.sqrt(-2.0 * jnp.log(tail))
    z = t - (2.515517 + t * (0.802853 + t * 0.010328)) / (1.0 + t * (1.432788 + t * (0.189269 + t * 0.001308)))
    return jnp.where(p < 0.5, z, -z)


def _kth_largest(s_ref, count_ge, n_valid, top, kt):
    cols = s_ref.shape[1]
    shape = (1, cols)
    head = s_ref[0:min(kt, 256), :]
    mean = head.mean(axis=0, keepdims=True)
    std = jnp.sqrt(jnp.maximum((head * head).mean(axis=0, keepdims=True) - mean * mean, 1e-30))
    guess = mean + std * _normal_quantile(jnp.minimum((top - 0.5) / jnp.maximum(n_valid, 1.0), 0.999))
    log_top = jnp.log(top - 0.5)

    def adjacent(lo_k, hi_k):
        return (hi_k <= lo_k + 1) | ((lo_k == 0) & (hi_k == _KEY_MIN_NORMAL))

    def active(state):
        _, lo_k, hi_k, clo = state[:4]
        return (clo > top) & jnp.logical_not(adjacent(lo_k, hi_k))

    def cond(state):
        return (state[0] < _MAX_SEARCH_PASSES) & (jnp.max(jnp.where(active(state), 1.0, 0.0)) > 0.0)

    def body(state):
        it, lo_k, hi_k, clo, chi, lo_real, hi_real, run, last_up = state
        lo_v, hi_v = _key_to_float(lo_k), _key_to_float(hi_k)
        both = (lo_real > 0) & (hi_real > 0)
        log_lo = jnp.log(clo)
        frac = (log_lo - log_top) / (log_lo - jnp.log(jnp.maximum(chi, 0.5)))
        frac = jnp.where(clo - chi > 16.0, frac, 0.5)
        step = std * 0.5 * lax.shift_left(jnp.int32(1), jnp.minimum(run, 20)).astype(jnp.float32)
        t = jnp.where(both, lo_v + (hi_v - lo_v) * frac, jnp.where(lo_real > 0, lo_v + step, hi_v - step))
        t = jnp.where(it == 0, guess, t)
        t_k = _float_to_key(t)
        mid_k = (lo_k >> 1) + (hi_k >> 1) + (lo_k & hi_k & 1)
        bisect = (both & (run >= 2)) | (it >= 24)
        t_k = jnp.where(bisect, mid_k, t_k)
        t_k = jnp.where((lo_k < 0) & (hi_k > _KEY_MIN_NORMAL) & (it >= 3), 0, t_k)
        t_k = jnp.where((lo_k == 0) & (hi_k > _KEY_MIN_NORMAL), _KEY_MIN_NORMAL, t_k)
        t_k = jnp.minimum(jnp.maximum(t_k, lo_k + 1), hi_k - 1)
        c = count_ge(_key_to_float(t_k))
        live = active(state)
        up = c >= top
        go_lo, go_hi = live & up, live & jnp.logical_not(up)
        same = jnp.where(up, 1, -1) == last_up
        return (it + 1,
                jnp.where(go_lo, t_k, lo_k), jnp.where(go_hi, t_k, hi_k),
                jnp.where(go_lo, c, clo), jnp.where(go_hi, c, chi),
                jnp.where(go_lo, 1, lo_real), jnp.where(go_hi, 1, hi_real),
                jnp.where(bisect, 0, jnp.where(same, run + 1, 1)),
                jnp.where(up, 1, -1))

    zeros = jnp.zeros(shape, jnp.int32)
    init = (jnp.int32(0), jnp.full(shape, _KEY_NEG_FLT_MAX, jnp.int32), jnp.full(shape, _KEY_POS_INF, jnp.int32),
            n_valid, jnp.zeros(shape, jnp.float32), zeros, zeros, zeros, zeros)
    out = lax.while_loop(cond, body, init)
    return _key_to_float(out[1]), out[3]


def _count_ge(ref, x, n_pairs, rows):
    cols = ref.shape[1]

    def tile(j, acc):
        blk = ref[pl.ds(pl.multiple_of(j * rows, rows), rows), :]
        ones = jnp.where(blk >= x, 1.0, 0.0)
        return acc + ones.reshape(rows // 8, 8, cols).sum(axis=0)

    def pair(jj, acc):
        return tile(2 * jj + 1, tile(2 * jj, acc))

    acc = lax.fori_loop(0, n_pairs, pair, jnp.zeros((8, cols), jnp.float32))
    return acc.sum(axis=0, keepdims=True)


def _attn_kernel(qi_ref, wit_ref, q_ref, kidx_ref, k_ref, vt_ref, o_ref,
                 s_ref, bias0_ref, bias1_ref, lg0_ref, lg1_ref, acc_ref, *, past, tq, kt, top, seq_q):
    n_seq = tq // seq_q
    t0 = pl.program_id(1) * seq_q
    n_kt = (past + t0 + seq_q + kt - 1) // kt
    lane_q = lax.broadcasted_iota(jnp.int32, (1, tq), 1)
    limit = past + ((t0 + lane_q % seq_q) // CHUNK + 1) * CHUNK
    w = wit_ref[...]

    def keys_times_queries(keys_ref, rows, lanes, queries_ref, cols):
        parts = [_dot_nt(keys_ref[g, rows, lanes], queries_ref[g * seq_q:(g + 1) * seq_q, cols])
                 for g in range(n_seq)]
        return parts[0] if n_seq == 1 else jnp.concatenate(parts, axis=1)

    def score_rows(off, rows, last):
        acc = jnp.zeros((rows, tq), jnp.float32)
        for h in range(IDX_HEADS):
            s = keys_times_queries(kidx_ref, pl.ds(off, rows), slice(None), qi_ref,
                                   slice(h * IDX_DIM, (h + 1) * IDX_DIM))
            acc = acc + w[h:h + 1, :] * jnp.maximum(s, 0.0)
        if last:
            key = off + lax.broadcasted_iota(jnp.int32, (rows, tq), 0)
            acc = jnp.where(key < limit, acc, -jnp.inf)
        s_ref[pl.ds(off, rows), :] = acc

    def score_pair(jj, carry):
        score_rows(pl.multiple_of(jj * 2 * kt, 2 * kt), 2 * kt, False)
        return carry

    n_full = n_kt - 1
    lax.fori_loop(0, n_full // 2, score_pair, 0)

    @pl.when(n_full % 2 == 1)
    def _():
        score_rows(pl.multiple_of((n_full - 1) * kt, kt), kt, False)

    score_rows(pl.multiple_of(n_full * kt, kt), kt, True)

    def count_ge(thr):
        return _count_ge(s_ref, thr, n_kt, kt // 2)

    thr, cnt = _kth_largest(s_ref, count_ge, limit.astype(jnp.float32), top, kt)

    def resolve_ties(cnt):
        excess = cnt - top
        row = lax.broadcasted_iota(jnp.int32, (kt, kt), 0)
        col = lax.broadcasted_iota(jnp.int32, (kt, kt), 1)
        at_or_after = jnp.where(col >= row, 1.0, 0.0).astype(_MXU_DTYPE)

        def tile(jr, after):
            rows = pl.ds(pl.multiple_of((n_kt - 1 - jr) * kt, kt), kt)
            blk = s_ref[rows, :]
            equal = blk == thr
            suffix = after + _dot(at_or_after, jnp.where(equal, 1.0, 0.0).astype(_MXU_DTYPE))
            s_ref[rows, :] = jnp.where(equal, jnp.where(suffix <= excess, -jnp.inf, blk), blk)
            return suffix[0:1, :]

        lax.fori_loop(0, n_kt, tile, jnp.zeros((1, tq), jnp.float32))
        return count_ge(thr)

    cnt = lax.cond(jnp.max(cnt) > top, resolve_ties, lambda c: c, cnt)

    def drop_lowest(cnt):
        over = cnt > top

        def tile(j):
            off = pl.multiple_of(j * kt, kt)
            return off, s_ref[pl.ds(off, kt), :]

        def min_body(j, vmin):
            _, blk = tile(j)
            cand = jnp.where(blk >= thr, blk, jnp.inf)
            return jnp.minimum(vmin, cand.reshape(kt // 8, 8, tq).min(axis=0))

        vmin = lax.fori_loop(0, n_kt, min_body, jnp.full((8, tq), jnp.inf, jnp.float32))
        vmin = vmin.min(axis=0, keepdims=True)

        def idx_body(j, imax):
            off, blk = tile(j)
            key_idx = off + lax.broadcasted_iota(jnp.int32, (kt, tq), 0)
            cand = jnp.where(blk == vmin, key_idx, -1)
            return jnp.maximum(imax, cand.reshape(kt // 8, 8, tq).max(axis=0))

        imax = lax.fori_loop(0, n_kt, idx_body, jnp.full((8, tq), -1, jnp.int32))
        kill = jnp.where(over, imax.max(axis=0, keepdims=True), -1)

        def kill_body(j, carry):
            off, blk = tile(j)
            key_idx = off + lax.broadcasted_iota(jnp.int32, (kt, tq), 0)
            s_ref[pl.ds(off, kt), :] = jnp.where(key_idx == kill, -jnp.inf, blk)
            return carry

        lax.fori_loop(0, n_kt, kill_body, 0)
        return jnp.where(over, cnt - 1.0, cnt)

    lax.while_loop(lambda cnt: jnp.max(cnt) > top, drop_lowest, cnt)

    acc_ref[...] = jnp.zeros(acc_ref.shape, jnp.float32)

    def logits_sweep(j, m, bias_buf, lg_buf):
        off = pl.multiple_of(jnp.minimum(j, n_kt - 1) * kt, kt)
        bias_buf[...] = jnp.where(s_ref[pl.ds(off, kt), :] >= jnp.where(j < n_kt, thr, jnp.inf), 0.0, _NEG)
        m_new = []
        for h in range(N_HEADS):
            c = h // GROUP
            lg = keys_times_queries(k_ref, pl.ds(off, kt), slice(c * HEAD_DIM, (c + 1) * HEAD_DIM), q_ref,
                                    slice(h * HEAD_DIM, (h + 1) * HEAD_DIM)) + bias_buf[...]
            lg_buf[h] = lg
            m_new.append(jnp.maximum(m[h], lg.max(axis=0, keepdims=True)))
        return tuple(m_new)

    def softmax_sweep(j, m_old, m_new, l, lg_buf):
        off = pl.multiple_of(jnp.minimum(j, n_kt - 1) * kt, kt)
        l_new = []
        for h in range(N_HEADS):
            c = h // GROUP
            alpha = jnp.exp2(m_old[h] - m_new[h])
            p = jnp.exp2(lg_buf[h] - m_new[h])
            pv = [_dot(vt_ref[g, c * HEAD_DIM:(c + 1) * HEAD_DIM, pl.ds(off, kt)],
                       p[:, g * seq_q:(g + 1) * seq_q].astype(vt_ref.dtype)) for g in range(n_seq)]
            acc_ref[h] = alpha * acc_ref[h] + (pv[0] if n_seq == 1 else jnp.concatenate(pv, axis=1))
            l_new.append(alpha * l[h] + p.sum(axis=0, keepdims=True))
        return tuple(l_new)

    def attn_step(jj, carry):
        m_a, m_b, l = carry
        m_c = logits_sweep(2 * jj + 1, m_b, bias1_ref, lg1_ref)
        l = softmax_sweep(2 * jj, m_a, m_b, l, lg0_ref)
        m_d = logits_sweep(2 * jj + 2, m_c, bias0_ref, lg0_ref)
        l = softmax_sweep(2 * jj + 1, m_b, m_c, l, lg1_ref)
        return m_c, m_d, l

    m_init = tuple(jnp.full((1, tq), _NEG, jnp.float32) for _ in range(N_HEADS))
    l_init = tuple(jnp.zeros((1, tq), jnp.float32) for _ in range(N_HEADS))
    m_first = logits_sweep(0, m_init, bias0_ref, lg0_ref)
    if kt == s_ref.shape[0]:
        l = softmax_sweep(0, m_init, m_first, l_init, lg0_ref)
    else:
        _, _, l = lax.fori_loop(0, (n_kt + 1) // 2, attn_step, (m_init, m_first, l_init))
    if o_ref.shape[0] == tq:
        o = jnp.concatenate([acc_ref[h] / l[h] for h in range(N_HEADS)], axis=0)
        o_ref[...] = o.T.astype(o_ref.dtype)
    else:
        for h in range(N_HEADS):
            o_ref[h * HEAD_DIM:(h + 1) * HEAD_DIM, :] = (acc_ref[h] / l[h]).astype(o_ref.dtype)


def _attn(qi, wit, q, kidx, k, vt, *, seq_len, past, tq, kt):
    batch, lp = k.shape[0], k.shape[1]
    t = seq_len
    n_seq = max(1, tq // t)
    seq_q = tq // n_seq
    nq = t // seq_q
    n_keys = past + t
    assert batch % n_seq == 0 and t % seq_q == 0 and seq_q % CHUNK == 0 and lp % kt == 0 and lp >= n_keys
    assert t == seq_q or (past == 0 and kt == tq)
    top = min(TOPK_MAX, n_keys // 4)

    def per_sequence(width_last, width_mid):
        return pl.BlockSpec((n_seq, width_mid, width_last), lambda bi, i: (bi, 0, 0))

    if vt.ndim == 2:
        assert n_seq == 1 and lp == t
        vt = vt.reshape(1, KV_COLS, batch * t)
        vt_spec = pl.BlockSpec((1, KV_COLS, lp), lambda bi, i: (0, 0, bi))
    else:
        vt_spec = per_sequence(lp, KV_COLS)
    transpose_in_kernel = tq % LANES == 0
    if transpose_in_kernel:
        out_shape = jax.ShapeDtypeStruct((batch * t, Q_COLS), _MXU_DTYPE)
        out_spec = pl.BlockSpec((tq, Q_COLS), lambda bi, i: (bi * nq + i, 0))
        wit_spec = pl.BlockSpec((IDX_HEADS, tq), lambda bi, i: (0, bi * nq + i))
    else:
        assert n_seq == 1
        out_shape = jax.ShapeDtypeStruct((batch, Q_COLS, t), _MXU_DTYPE)
        out_spec = pl.BlockSpec((None, Q_COLS, tq), lambda bi, i: (bi, 0, i))
        wit = wit.reshape(IDX_HEADS, batch, t).transpose(1, 0, 2)
        wit_spec = pl.BlockSpec((None, IDX_HEADS, tq), lambda bi, i: (bi, 0, i))
    single_tile = kt == lp
    spare = SUBLANES if single_tile else kt
    out = pl.pallas_call(
        functools.partial(_attn_kernel, past=past, tq=tq, kt=kt, top=top, seq_q=seq_q),
        grid=(batch // n_seq, nq),
        in_specs=[
            pl.BlockSpec((tq, IDXQ_COLS), lambda bi, i: (bi * nq + i, 0)),
            wit_spec,
            pl.BlockSpec((tq, Q_COLS), lambda bi, i: (bi * nq + i, 0)),
            per_sequence(IDX_DIM, lp), per_sequence(KV_COLS, lp), vt_spec,
        ],
        out_specs=out_spec,
        out_shape=out_shape,
        scratch_shapes=[
            pltpu.VMEM((lp, tq), jnp.float32),
            pltpu.VMEM((kt, tq), jnp.float32),
            pltpu.VMEM((spare, tq), jnp.float32),
            pltpu.VMEM((N_HEADS, kt, tq), jnp.float32),
            pltpu.VMEM((N_HEADS, spare, tq), jnp.float32),
            pltpu.VMEM((N_HEADS, HEAD_DIM, tq), jnp.float32),
        ],
        compiler_params=_params(("parallel", "arbitrary")),
        name=f"attn_t{t}_tq{tq}",
    )(qi, wit, q, kidx, k, vt)
    if transpose_in_kernel:
        return out
    return out.transpose(0, 2, 1).reshape(batch * t, Q_COLS)


def _merge_kernel(h_ref, c_ref, halo_ref, at_ref, ga_ref, gc_ref, cw_ref, cb_ref, lng_ref, lnb_ref,
                  wco_ref, wao_ref, wo_ref, o_ref, win_ref, shift_ref, *, seq_len):
    tm = c_ref.shape[0]
    n_chunk = tm // CHUNK
    lead = _HALO - (CONV_WIDTH - 1)
    dcs = []
    for j in range(n_chunk):
        if seq_len == CHUNK:
            history = halo_ref[j]
        elif j == 0:
            starts_sequence = (pl.program_id(0) * tm) % seq_len == 0
            history = jnp.where(starts_sequence, 0.0, halo_ref[...])
        else:
            history = c_ref[j * CHUNK - _HALO:j * CHUNK, :]
        win_ref[j, 0:_HALO, :] = history
        win_ref[j, _HALO:_HALO + CHUNK, :] = c_ref[j * CHUNK:(j + 1) * CHUNK, :]
        for s in range(1, SUBLANES):
            shift_ref[j, s - 1] = win_ref[j, s:s + shift_ref.shape[2], :]
        acc = jnp.zeros((CHUNK, CONV_CH), jnp.float32)
        for tap in range(CONV_WIDTH):
            s, base = (lead + tap) % SUBLANES, (lead + tap) // SUBLANES * SUBLANES
            rows = win_ref[j, base:base + CHUNK, :] if s == 0 else shift_ref[j, s - 1, base:base + CHUNK, :]
            acc = acc + rows * cw_ref[tap:tap + 1, :]
        dcs.append(acc)
    dc = jnp.concatenate(dcs, axis=0) + cb_ref[...]
    mu = jnp.mean(dc, axis=-1, keepdims=True)
    var = jnp.mean(jnp.square(dc - mu), axis=-1, keepdims=True)
    y = (dc - mu) * lax.rsqrt(var + EPS) * lng_ref[...] + lnb_ref[...]
    conv_out = _dot((y * jax.nn.sigmoid(y)).astype(wco_ref.dtype), wco_ref[...])
    attn_out = _dot(at_ref[...], wao_ref[...])
    merged = ga_ref[...] * attn_out + gc_ref[...] * conv_out
    o_ref[...] = h_ref[...] + _dot(merged.astype(wo_ref.dtype), wo_ref[...])


def _merge(h, c, state, attn, ga, gc, weights, tm, seq_len):
    n = h.shape[0]
    cw, conv_b, ln_g, ln_b, wco, wao, wo = weights

    def row(width):
        return pl.BlockSpec((tm, width), lambda i: (i, 0))

    if seq_len == CHUNK:
        halo, halo_spec = state, pl.BlockSpec((tm // CHUNK, _HALO, CONV_CH), lambda i: (i, 0, 0))
    else:
        assert state is None and seq_len % tm == 0
        per_tile = tm // _HALO
        halo, halo_spec = c, pl.BlockSpec((_HALO, CONV_CH), lambda i: (jnp.maximum(i * per_tile - 1, 0), 0))
    vec = _resident((1, CONV_CH))
    return pl.pallas_call(
        functools.partial(_merge_kernel, seq_len=seq_len),
        grid=(n // tm,),
        in_specs=[
            row(D_MODEL), row(CONV_CH), halo_spec,
            row(Q_COLS), row(D_MODEL), row(D_MODEL),
            _resident(cw.shape), vec, vec, vec,
            _resident((CONV_CH, D_MODEL)), _resident((Q_COLS, D_MODEL)), _resident((D_MODEL, D_MODEL)),
        ],
        out_specs=row(D_MODEL),
        out_shape=jax.ShapeDtypeStruct((n, D_MODEL), jnp.float32),
        scratch_shapes=[pltpu.VMEM((tm // CHUNK, _HALO + CHUNK, CONV_CH), jnp.float32),
                        pltpu.VMEM((tm // CHUNK, SUBLANES - 1, _HALO + CHUNK - SUBLANES, CONV_CH), jnp.float32)],
        compiler_params=_params(("parallel",)),
        name=f"merge_n{n}",
    )(h, c, halo, attn, ga, gc, cw, conv_b, ln_g, ln_b, wco, wao, wo)


def _row_tile(n, candidates):
    for tm in candidates:
        if n % tm == 0:
            return tm
    raise ValueError(f"row count {n} is not a multiple of {candidates[-1]}")


def _rope_tables(pos, rows):
    inv = ROPE_THETA ** (-jnp.arange(0, HEAD_DIM, 2, dtype=jnp.float32) / HEAD_DIM)
    ang = pos.astype(jnp.float32)[:, None] * inv[None, :]
    cos, sin = jnp.cos(ang), jnp.sin(ang)
    reps = rows // pos.shape[0]
    return (jnp.tile(cos, (reps, 4)), jnp.tile(jnp.concatenate([-sin, sin], axis=1), (reps, 2)))


def _main_weight(w_in):
    sizes = (Q_COLS, KV_COLS, KV_COLS, IDXQ_COLS, IDX_DIM, IDX_HEADS, 2 * CONV_CH, 2 * D_MODEL)
    offs = [0]
    for s in sizes:
        offs.append(offs[-1] + s)
    wq, wk, wv, wqi, wki, wwi, wconv, wgate = (w_in[:, offs[i]:offs[i + 1]] for i in range(len(sizes)))
    pad = jnp.zeros((D_MODEL, LANES - IDX_DIM - IDX_HEADS), w_in.dtype)
    return jnp.concatenate([wq, wqi, wk, wki, wwi, pad, wv, wconv, wgate], axis=1).astype(_MXU_DTYPE)


def _layer(x, caches, weights, final_norm):
    (ffn1_norm, ffn1_w, mix_norm, w_main, b_gate, merge_w, ffn2_norm, ffn2_w) = weights
    b, t, _ = x.shape
    n = b * t
    assert t % CHUNK == 0
    tm = _row_tile(n, (512, 256, 128, 64))
    tm_merge = _row_tile(n, (256, 128, 64))
    past = 0 if caches is None else caches[0].shape[1]

    h = _ffn(x.reshape(n, D_MODEL), ffn1_norm, ffn1_w, None, tm)
    if t >= tm:
        assert t % tm == 0
    else:
        assert tm % t == 0
    cos, sin = _rope_tables(past + jnp.arange(t, dtype=jnp.int32), max(t, tm))
    q, qi, k, ki, v, c, ga, gc, kb, kib, vt, wit = _proj(h, mix_norm, w_main, b_gate, cos, sin, tm)

    if caches is None:
        tq = _row_tile(t, (512, 256, 128, 64))
        attn = _attn(qi, wit, q, kib.reshape(b, t, IDX_DIM), kb.reshape(b, t, KV_COLS), vt,
                     seq_len=t, past=0, tq=tq, kt=tq)
        state = None
        conv_tail = c.reshape(b, t, CONV_CH)[:, t - (CONV_WIDTH - 1):]
    else:
        cache_k, cache_v, cache_idx_k, state_conv = caches
        n_keys = past + t
        pad_keys = -n_keys % LANES

        def with_cache(cache, new):
            cache = cache.reshape(b, past, -1).astype(_MXU_DTYPE)
            pad = jnp.zeros((b, pad_keys, cache.shape[2]), _MXU_DTYPE)
            return jnp.concatenate([cache, new.reshape(b, t, -1), pad], axis=1)

        vt_all = with_cache(cache_v, vt.reshape(KV_COLS, b, t).transpose(1, 2, 0)).transpose(0, 2, 1)
        side_by_side = next(g for g in (4, 2, 1) if b % g == 0)
        attn = _attn(qi, wit, q, with_cache(cache_idx_k, kib), with_cache(cache_k, kb), vt_all,
                     seq_len=t, past=past, tq=side_by_side * t, kt=n_keys + pad_keys)
        lead = jnp.zeros((b, _HALO - (CONV_WIDTH - 1), CONV_CH), jnp.float32)
        state = jnp.concatenate([lead, state_conv], axis=1)
        conv_tail = jnp.concatenate([state_conv, c.reshape(b, t, CONV_CH)], axis=1)[:, -(CONV_WIDTH - 1):]
    if t != CHUNK:
        assert state is None, "a carried conv state is supported for 64-row sequences only"

    h2 = _merge(h, c, state, attn, ga, gc, merge_w, tm_merge, t)
    y = _ffn(h2, ffn2_norm, ffn2_w, final_norm, tm)
    return (y.reshape(b, t, D_MODEL),
            k.reshape(1, b, t, N_KV_HEADS, HEAD_DIM), v.reshape(1, b, t, N_KV_HEADS, HEAD_DIM),
            ki.reshape(1, b, t, IDX_DIM), conv_tail[None])


def kernel(x_prompt, x_sample, cache_k, cache_v, cache_idx_k, state_conv, ffn1_norm, ffn1_w_in, ffn1_w_out, mix_norm, w_in, b_gate, conv_w, conv_b, conv_ln_g, conv_ln_b, conv_w_out, attn_w_out, w_out, ffn2_norm, ffn2_w_in, ffn2_w_out, final_norm):
    assert ffn1_norm.shape[0] == 1, "one layer"
    assert x_prompt.shape[1] >= CONV_WIDTH - 1
    conv_taps = jnp.zeros((_HALO, CONV_CH), jnp.float32).at[:CONV_WIDTH].set(conv_w[0])
    merge_w = (conv_taps, conv_b[0].reshape(1, CONV_CH), conv_ln_g[0].reshape(1, CONV_CH),
               conv_ln_b[0].reshape(1, CONV_CH), conv_w_out[0].astype(_MXU_DTYPE),
               attn_w_out[0].astype(_MXU_DTYPE), w_out[0].astype(_MXU_DTYPE))
    weights = (ffn1_norm[0], _ffn_weights(ffn1_w_in[0], ffn1_w_out[0]), mix_norm[0],
               _main_weight(w_in[0]), b_gate[0], merge_w,
               ffn2_norm[0], _ffn_weights(ffn2_w_in[0], ffn2_w_out[0]))
    y_p, k_p, v_p, ki_p, conv_p = _layer(x_prompt, None, weights, final_norm)
    y_s, k_s, v_s, ki_s, conv_s = _layer(
        x_sample, (cache_k[0], cache_v[0], cache_idx_k[0], state_conv[0]), weights, final_norm)
    return (y_p, y_s, k_p, v_p, ki_p, conv_p, k_s, v_s, ki_s, conv_s)
```

```python
import functools
import math

import jax
import jax.numpy as jnp
from jax import lax
from jax.experimental import pallas as pl
from jax.experimental.pallas import tpu as pltpu

D_MODEL = 1024
CHUNK = 64
N_HEADS = 8
N_KV_HEADS = 2
HEAD_DIM = 64
GROUP = N_HEADS // N_KV_HEADS
IDX_HEADS = 8
IDX_DIM = 64
TOPK_MAX = 256
CONV_CH = 512
CONV_WIDTH = 31
D_FF = 2816
ROPE_THETA = 10000.0
EPS = 1e-6

Q_COLS = N_HEADS * HEAD_DIM
KV_COLS = N_KV_HEADS * HEAD_DIM
IDXQ_COLS = IDX_HEADS * IDX_DIM

LANES = 128
SUBLANES = 8
_MXU_DTYPE = jnp.bfloat16
_VMEM_LIMIT = 56 * 1024 * 1024
_FF_CHUNK = 256
_HALO = 32
_NEG = -1e30
_Q_SCALE = HEAD_DIM ** -0.5 * math.log2(math.e)

_C_Q = 0
_C_QI = _C_Q + Q_COLS
_C_K = _C_QI + IDXQ_COLS
_C_KW = _C_K + KV_COLS
_C_V = _C_KW + LANES
_C_CA = _C_V + KV_COLS
_C_CB = _C_CA + CONV_CH
_C_GA = _C_CB + CONV_CH
_C_GC = _C_GA + D_MODEL
_C_END = _C_GC + D_MODEL

_KEY_NEG_FLT_MAX = -2139095040
_KEY_POS_INF = 0x7F800000
_KEY_MIN_NORMAL = 0x00800000
_MAX_SEARCH_PASSES = 96
_PROBE_MARGIN = 0.1


def _params(sem):
    return pltpu.CompilerParams(dimension_semantics=sem, vmem_limit_bytes=_VMEM_LIMIT)


def _resident(shape):
    nd = len(shape)
    return pl.BlockSpec(shape, lambda *_: (0,) * nd, pipeline_mode=pl.Buffered(1))


def _rms(x, g):
    return x * lax.rsqrt(jnp.mean(x * x, axis=-1, keepdims=True) + EPS) * g


def _dot(a, b):
    return jnp.dot(a, b, preferred_element_type=jnp.float32)


def _dot_nt(a, b):
    return lax.dot_general(a, b, (((1,), (1,)), ((), ())), preferred_element_type=jnp.float32)


def _ffn_kernel(*refs, final_norm):
    if final_norm:
        x_ref, g_ref, wa_ref, wb_ref, wo_ref, gf_ref, o_ref = refs
    else:
        x_ref, g_ref, wa_ref, wb_ref, wo_ref, o_ref = refs
    x = x_ref[...]
    u = _rms(x, g_ref[...]).astype(wa_ref.dtype)
    acc = jnp.zeros_like(x)
    for c in range(D_FF // _FF_CHUNK):
        sl = slice(c * _FF_CHUNK, (c + 1) * _FF_CHUNK)
        a = _dot(u, wa_ref[:, sl])
        b = _dot(u, wb_ref[:, sl])
        act = (a * jax.nn.sigmoid(a) * b).astype(wo_ref.dtype)
        acc = acc + _dot(act, wo_ref[sl, :])
    h = x + 0.5 * acc
    if final_norm:
        h = _rms(h, gf_ref[...])
    o_ref[...] = h


def _ffn_weights(w_in, w_out):
    return w_in[:, :D_FF].astype(_MXU_DTYPE), w_in[:, D_FF:].astype(_MXU_DTYPE), w_out.astype(_MXU_DTYPE)


def _ffn(x, g, weights, g_final, tm):
    n = x.shape[0]
    wa, wb, wo = weights
    row = pl.BlockSpec((tm, D_MODEL), lambda i: (i, 0))
    vec = _resident((1, D_MODEL))
    in_specs = [row, vec, _resident(wa.shape), _resident(wb.shape), _resident(wo.shape)]
    args = [x, g.reshape(1, D_MODEL), wa, wb, wo]
    if g_final is not None:
        in_specs.append(vec)
        args.append(g_final.reshape(1, D_MODEL))
    return pl.pallas_call(
        functools.partial(_ffn_kernel, final_norm=g_final is not None),
        grid=(n // tm,),
        in_specs=in_specs,
        out_specs=row,
        out_shape=jax.ShapeDtypeStruct((n, D_MODEL), jnp.float32),
        compiler_params=_params(("parallel",)),
        name=f"ffn_final_n{n}" if g_final is not None else f"ffn_n{n}",
    )(*args)


def _swap_halves(x):
    lane = lax.broadcasted_iota(jnp.int32, x.shape, 1)
    first = (lane & (HEAD_DIM - 1)) < HEAD_DIM // 2
    return jnp.where(first, pltpu.roll(x, LANES - HEAD_DIM // 2, 1), pltpu.roll(x, HEAD_DIM // 2, 1))


def _rope(x, cos, sin_signed):
    tiles = []
    for j in range(x.shape[1] // LANES):
        xt = x[:, j * LANES:(j + 1) * LANES]
        tiles.append(xt * cos + _swap_halves(xt) * sin_signed)
    return tiles[0] if len(tiles) == 1 else jnp.concatenate(tiles, axis=1)


def _proj_kernel(h_ref, g_ref, w_ref, bg_ref, cos_ref, sin_ref,
                 q_ref, qi_ref, k_ref, ki_ref, v_ref, c_ref, ga_ref, gc_ref,
                 kb_ref, kib_ref, vt_ref, wit_ref):
    u = _rms(h_ref[...], g_ref[...]).astype(w_ref.dtype)
    cos = cos_ref[...]
    sin = sin_ref[...]

    def proj(lo, hi):
        return _dot(u, w_ref[:, lo:hi])

    q_ref[...] = (_rope(proj(_C_Q, _C_QI), cos, sin) * _Q_SCALE).astype(q_ref.dtype)
    qi_ref[...] = _rope(proj(_C_QI, _C_K), cos, sin).astype(qi_ref.dtype)
    k = _rope(proj(_C_K, _C_KW), cos, sin)
    k_ref[...] = k
    kb_ref[...] = k.astype(kb_ref.dtype)
    zkw = proj(_C_KW, _C_V)
    ki = _rope(zkw, cos, sin)[:, :IDX_DIM]
    ki_ref[...] = ki
    kib_ref[...] = ki.astype(kib_ref.dtype)
    wit_ref[...] = (zkw * (IDX_HEADS ** -0.5)).T[IDX_DIM:IDX_DIM + IDX_HEADS, :]
    v = proj(_C_V, _C_CA)
    v_ref[...] = v
    vt_ref[...] = v.T.astype(vt_ref.dtype)
    c_ref[...] = proj(_C_CA, _C_CB) * jax.nn.sigmoid(proj(_C_CB, _C_GA))
    ga_ref[...] = jax.nn.sigmoid(proj(_C_GA, _C_GC) + bg_ref[:, :D_MODEL])
    gc_ref[...] = jax.nn.sigmoid(proj(_C_GC, _C_END) + bg_ref[:, D_MODEL:])


def _proj(h, g, w_main, b_gate, cos, sin, tm):
    n = h.shape[0]
    n_tab = cos.shape[0] // tm

    def row(width):
        return pl.BlockSpec((tm, width), lambda i: (i, 0))

    def col(height):
        return pl.BlockSpec((height, tm), lambda i: (0, i))

    table = pl.BlockSpec((tm, LANES), lambda i: (i % n_tab, 0))
    f32 = jnp.float32
    out_shape = (
        jax.ShapeDtypeStruct((n, Q_COLS), _MXU_DTYPE),
        jax.ShapeDtypeStruct((n, IDXQ_COLS), _MXU_DTYPE),
        jax.ShapeDtypeStruct((n, KV_COLS), f32),
        jax.ShapeDtypeStruct((n, IDX_DIM), f32),
        jax.ShapeDtypeStruct((n, KV_COLS), f32),
        jax.ShapeDtypeStruct((n, CONV_CH), f32),
        jax.ShapeDtypeStruct((n, D_MODEL), f32),
        jax.ShapeDtypeStruct((n, D_MODEL), f32),
        jax.ShapeDtypeStruct((n, KV_COLS), _MXU_DTYPE),
        jax.ShapeDtypeStruct((n, IDX_DIM), _MXU_DTYPE),
        jax.ShapeDtypeStruct((KV_COLS, n), _MXU_DTYPE),
        jax.ShapeDtypeStruct((IDX_HEADS, n), f32),
    )
    out_specs = tuple(row(s.shape[1]) for s in out_shape[:10]) + (col(KV_COLS), col(IDX_HEADS))
    return pl.pallas_call(
        _proj_kernel,
        grid=(n // tm,),
        in_specs=[row(D_MODEL), _resident((1, D_MODEL)), _resident(w_main.shape),
                  _resident((1, 2 * D_MODEL)), table, table],
        out_specs=out_specs,
        out_shape=out_shape,
        compiler_params=_params(("parallel",)),
        name=f"proj_n{n}",
    )(h, g.reshape(1, D_MODEL), w_main, b_gate.reshape(1, 2 * D_MODEL), cos, sin)


def _float_to_key(x):
    bits = lax.bitcast_convert_type(x, jnp.int32)
    return jnp.where(bits >= 0, bits, bits ^ 0x7FFFFFFF)


def _key_to_float(key):
    return lax.bitcast_convert_type(jnp.where(key >= 0, key, key ^ 0x7FFFFFFF), jnp.float32)


def _normal_quantile(p):
    tail = jnp.minimum(p, 1.0 - p)
    t = jnp.sqrt(-2.0 * jnp.log(tail))
    z = t - (2.515517 + t * (0.802853 + t * 0.010328)) / (1.0 + t * (1.432788 + t * (0.189269 + t * 0.001308)))
    return jnp.where(p < 0.5, z, -z)


def _kth_largest(s_ref, count_ge, n_valid, top, kt):
    cols = s_ref.shape[1]
    shape = (1, cols)
    head = s_ref[0:min(kt, 256), :]
    mean = head.mean(axis=0, keepdims=True)
    std = jnp.sqrt(jnp.maximum((head * head).mean(axis=0, keepdims=True) - mean * mean, 1e-30))
    guess = mean + std * _normal_quantile(jnp.minimum((top - 0.5) / jnp.maximum(n_valid, 1.0), 0.999))
    log_top = jnp.log(top - 0.5)

    def adjacent(lo_k, hi_k):
        return (hi_k <= lo_k + 1) | ((lo_k == 0) & (hi_k == _KEY_MIN_NORMAL))

    def active(state):
        _, lo_k, hi_k, clo = state[:4]
        return (clo > top) & jnp.logical_not(adjacent(lo_k, hi_k))

    def cond(state):
        return (state[0] < _MAX_SEARCH_PASSES) & (jnp.max(jnp.where(active(state), 1.0, 0.0)) > 0.0)

    def body(state):
        it, lo_k, hi_k, clo, chi, lo_real, hi_real, run, last_up = state
        lo_v, hi_v = _key_to_float(lo_k), _key_to_float(hi_k)
        both = (lo_real > 0) & (hi_real > 0)
        log_lo = jnp.log(clo)
        frac = (log_lo - log_top) / (log_lo - jnp.log(jnp.maximum(chi, 0.5)))
        frac = jnp.where(clo - chi > 16.0, jnp.clip(frac, _PROBE_MARGIN, 1.0 - _PROBE_MARGIN), 0.5)
        step = std * 0.25 * lax.shift_left(jnp.int32(1), jnp.minimum(run, 20)).astype(jnp.float32)
        t = jnp.where(both, lo_v + (hi_v - lo_v) * frac, jnp.where(lo_real > 0, lo_v + step, hi_v - step))
        t = jnp.where(it == 0, guess, t)
        t_k = _float_to_key(t)
        mid_k = (lo_k >> 1) + (hi_k >> 1) + (lo_k & hi_k & 1)
        bisect = (both & (run >= 2)) | (it >= 24)
        t_k = jnp.where(bisect, mid_k, t_k)
        t_k = jnp.where((lo_k < 0) & (hi_k > _KEY_MIN_NORMAL) & (it >= 3), 0, t_k)
        t_k = jnp.where((lo_k == 0) & (hi_k > _KEY_MIN_NORMAL), _KEY_MIN_NORMAL, t_k)
        t_k = jnp.minimum(jnp.maximum(t_k, lo_k + 1), hi_k - 1)
        c = count_ge(_key_to_float(t_k))
        live = active(state)
        up = c >= top
        go_lo, go_hi = live & up, live & jnp.logical_not(up)
        same = jnp.where(up, 1, -1) == last_up
        return (it + 1,
                jnp.where(go_lo, t_k, lo_k), jnp.where(go_hi, t_k, hi_k),
                jnp.where(go_lo, c, clo), jnp.where(go_hi, c, chi),
                jnp.where(go_lo, 1, lo_real), jnp.where(go_hi, 1, hi_real),
                jnp.where(bisect, 0, jnp.where(same, run + 1, 1)),
                jnp.where(up, 1, -1))

    zeros = jnp.zeros(shape, jnp.int32)
    init = (jnp.int32(0), jnp.full(shape, _KEY_NEG_FLT_MAX, jnp.int32), jnp.full(shape, _KEY_POS_INF, jnp.int32),
            n_valid, jnp.zeros(shape, jnp.float32), zeros, zeros, zeros, zeros)
    out = lax.while_loop(cond, body, init)
    return _key_to_float(out[1]), out[3]


def _count_ge(ref, x, n_pairs, rows):
    cols = ref.shape[1]

    def tile(j, acc):
        blk = ref[pl.ds(pl.multiple_of(j * rows, rows), rows), :]
        ones = jnp.where(blk >= x, 1.0, 0.0)
        return acc + ones.reshape(rows // 8, 8, cols).sum(axis=0)

    def pair(jj, acc):
        return tile(2 * jj + 1, tile(2 * jj, acc))

    acc = lax.fori_loop(0, n_pairs, pair, jnp.zeros((8, cols), jnp.float32))
    return acc.sum(axis=0, keepdims=True)


def _attn_kernel(qi_ref, wit_ref, q_ref, kidx_ref, k_ref, vt_ref, o_ref,
                 s_ref, bias0_ref, bias1_ref, lg0_ref, lg1_ref, acc_ref, *, past, tq, kt, top, seq_q):
    n_seq = tq // seq_q
    t0 = pl.program_id(1) * seq_q
    n_kt = (past + t0 + seq_q + kt - 1) // kt
    lane_q = lax.broadcasted_iota(jnp.int32, (1, tq), 1)
    limit = past + ((t0 + lane_q % seq_q) // CHUNK + 1) * CHUNK
    w = wit_ref[...]

    def keys_times_queries(keys_ref, rows, lanes, queries_ref, cols):
        parts = [_dot_nt(keys_ref[g, rows, lanes], queries_ref[g * seq_q:(g + 1) * seq_q, cols])
                 for g in range(n_seq)]
        return parts[0] if n_seq == 1 else jnp.concatenate(parts, axis=1)

    def score_rows(off, rows, last):
        acc = jnp.zeros((rows, tq), jnp.float32)
        for h in range(IDX_HEADS):
            s = keys_times_queries(kidx_ref, pl.ds(off, rows), slice(None), qi_ref,
                                   slice(h * IDX_DIM, (h + 1) * IDX_DIM))
            acc = acc + w[h:h + 1, :] * jnp.maximum(s, 0.0)
        if last:
            key = off + lax.broadcasted_iota(jnp.int32, (rows, tq), 0)
            acc = jnp.where(key < limit, acc, -jnp.inf)
        s_ref[pl.ds(off, rows), :] = acc

    def score_pair(jj, carry):
        score_rows(pl.multiple_of(jj * 2 * kt, 2 * kt), 2 * kt, False)
        return carry

    n_full = n_kt - 1
    lax.fori_loop(0, n_full // 2, score_pair, 0)

    @pl.when(n_full % 2 == 1)
    def _():
        score_rows(pl.multiple_of((n_full - 1) * kt, kt), kt, False)

    score_rows(pl.multiple_of(n_full * kt, kt), kt, True)

    def count_ge(thr):
        return _count_ge(s_ref, thr, n_kt, kt // 2)

    thr, cnt = _kth_largest(s_ref, count_ge, limit.astype(jnp.float32), top, kt)

    def resolve_ties(cnt):
        excess = cnt - top
        row = lax.broadcasted_iota(jnp.int32, (kt, kt), 0)
        col = lax.broadcasted_iota(jnp.int32, (kt, kt), 1)
        at_or_after = jnp.where(col >= row, 1.0, 0.0).astype(_MXU_DTYPE)

        def tile(jr, after):
            rows = pl.ds(pl.multiple_of((n_kt - 1 - jr) * kt, kt), kt)
            blk = s_ref[rows, :]
            equal = blk == thr
            suffix = after + _dot(at_or_after, jnp.where(equal, 1.0, 0.0).astype(_MXU_DTYPE))
            s_ref[rows, :] = jnp.where(equal, jnp.where(suffix <= excess, -jnp.inf, blk), blk)
            return suffix[0:1, :]

        lax.fori_loop(0, n_kt, tile, jnp.zeros((1, tq), jnp.float32))
        return count_ge(thr)

    cnt = lax.cond(jnp.max(cnt) > top, resolve_ties, lambda c: c, cnt)

    def drop_lowest(cnt):
        over = cnt > top

        def tile(j):
            off = pl.multiple_of(j * kt, kt)
            return off, s_ref[pl.ds(off, kt), :]

        def min_body(j, vmin):
            _, blk = tile(j)
            cand = jnp.where(blk >= thr, blk, jnp.inf)
            return jnp.minimum(vmin, cand.reshape(kt // 8, 8, tq).min(axis=0))

        vmin = lax.fori_loop(0, n_kt, min_body, jnp.full((8, tq), jnp.inf, jnp.float32))
        vmin = vmin.min(axis=0, keepdims=True)

        def idx_body(j, imax):
            off, blk = tile(j)
            key_idx = off + lax.broadcasted_iota(jnp.int32, (kt, tq), 0)
            cand = jnp.where(blk == vmin, key_idx, -1)
            return jnp.maximum(imax, cand.reshape(kt // 8, 8, tq).max(axis=0))

        imax = lax.fori_loop(0, n_kt, idx_body, jnp.full((8, tq), -1, jnp.int32))
        kill = jnp.where(over, imax.max(axis=0, keepdims=True), -1)

        def kill_body(j, carry):
            off, blk = tile(j)
            key_idx = off + lax.broadcasted_iota(jnp.int32, (kt, tq), 0)
            s_ref[pl.ds(off, kt), :] = jnp.where(key_idx == kill, -jnp.inf, blk)
            return carry

        lax.fori_loop(0, n_kt, kill_body, 0)
        return jnp.where(over, cnt - 1.0, cnt)

    lax.while_loop(lambda cnt: jnp.max(cnt) > top, drop_lowest, cnt)

    acc_ref[...] = jnp.zeros(acc_ref.shape, jnp.float32)

    def logits_sweep(j, m, bias_buf, lg_buf):
        off = pl.multiple_of(jnp.minimum(j, n_kt - 1) * kt, kt)
        bias_buf[...] = jnp.where(s_ref[pl.ds(off, kt), :] >= jnp.where(j < n_kt, thr, jnp.inf), 0.0, _NEG)
        m_new = []
        for h in range(N_HEADS):
            c = h // GROUP
            lg = keys_times_queries(k_ref, pl.ds(off, kt), slice(c * HEAD_DIM, (c + 1) * HEAD_DIM), q_ref,
                                    slice(h * HEAD_DIM, (h + 1) * HEAD_DIM)) + bias_buf[...]
            lg_buf[h] = lg
            m_new.append(jnp.maximum(m[h], lg.max(axis=0, keepdims=True)))
        return tuple(m_new)

    def softmax_sweep(j, m_old, m_new, l, lg_buf):
        off = pl.multiple_of(jnp.minimum(j, n_kt - 1) * kt, kt)
        l_new = []
        for h in range(N_HEADS):
            c = h // GROUP
            alpha = jnp.exp2(m_old[h] - m_new[h])
            p = jnp.exp2(lg_buf[h] - m_new[h])
            pv = [_dot(vt_ref[g, c * HEAD_DIM:(c + 1) * HEAD_DIM, pl.ds(off, kt)],
                       p[:, g * seq_q:(g + 1) * seq_q].astype(vt_ref.dtype)) for g in range(n_seq)]
            acc_ref[h] = alpha * acc_ref[h] + (pv[0] if n_seq == 1 else jnp.concatenate(pv, axis=1))
            l_new.append(alpha * l[h] + p.sum(axis=0, keepdims=True))
        return tuple(l_new)

    def attn_step(jj, carry):
        m_a, m_b, l = carry
        m_c = logits_sweep(2 * jj + 1, m_b, bias1_ref, lg1_ref)
        l = softmax_sweep(2 * jj, m_a, m_b, l, lg0_ref)
        m_d = logits_sweep(2 * jj + 2, m_c, bias0_ref, lg0_ref)
        l = softmax_sweep(2 * jj + 1, m_b, m_c, l, lg1_ref)
        return m_c, m_d, l

    m_init = tuple(jnp.full((1, tq), _NEG, jnp.float32) for _ in range(N_HEADS))
    l_init = tuple(jnp.zeros((1, tq), jnp.float32) for _ in range(N_HEADS))
    m_first = logits_sweep(0, m_init, bias0_ref, lg0_ref)
    if kt == s_ref.shape[0]:
        l = softmax_sweep(0, m_init, m_first, l_init, lg0_ref)
    else:
        _, _, l = lax.fori_loop(0, (n_kt + 1) // 2, attn_step, (m_init, m_first, l_init))
    if o_ref.shape[0] == tq:
        o = jnp.concatenate([acc_ref[h] / l[h] for h in range(N_HEADS)], axis=0)
        o_ref[...] = o.T.astype(o_ref.dtype)
    else:
        for h in range(N_HEADS):
            o_ref[h * HEAD_DIM:(h + 1) * HEAD_DIM, :] = (acc_ref[h] / l[h]).astype(o_ref.dtype)


def _attn(qi, wit, q, kidx, k, vt, *, seq_len, past, tq, kt):
    batch, lp = k.shape[0], k.shape[1]
    t = seq_len
    n_seq = max(1, tq // t)
    seq_q = tq // n_seq
    nq = t // seq_q
    n_keys = past + t
    assert batch % n_seq == 0 and t % seq_q == 0 and seq_q % CHUNK == 0 and lp % kt == 0 and lp >= n_keys
    assert t == seq_q or (past == 0 and kt == tq)
    top = min(TOPK_MAX, n_keys // 4)

    def per_sequence(width_last, width_mid):
        return pl.BlockSpec((n_seq, width_mid, width_last), lambda bi, i: (bi, 0, 0))

    if vt.ndim == 2:
        assert n_seq == 1 and lp == t
        vt = vt.reshape(1, KV_COLS, batch * t)
        vt_spec = pl.BlockSpec((1, KV_COLS, lp), lambda bi, i: (0, 0, bi))
    else:
        vt_spec = per_sequence(lp, KV_COLS)
    transpose_in_kernel = tq % LANES == 0
    if transpose_in_kernel:
        out_shape = jax.ShapeDtypeStruct((batch * t, Q_COLS), _MXU_DTYPE)
        out_spec = pl.BlockSpec((tq, Q_COLS), lambda bi, i: (bi * nq + i, 0))
        wit_spec = pl.BlockSpec((IDX_HEADS, tq), lambda bi, i: (0, bi * nq + i))
    else:
        assert n_seq == 1
        out_shape = jax.ShapeDtypeStruct((batch, Q_COLS, t), _MXU_DTYPE)
        out_spec = pl.BlockSpec((None, Q_COLS, tq), lambda bi, i: (bi, 0, i))
        wit = wit.reshape(IDX_HEADS, batch, t).transpose(1, 0, 2)
        wit_spec = pl.BlockSpec((None, IDX_HEADS, tq), lambda bi, i: (bi, 0, i))
    single_tile = kt == lp
    spare = SUBLANES if single_tile else kt
    out = pl.pallas_call(
        functools.partial(_attn_kernel, past=past, tq=tq, kt=kt, top=top, seq_q=seq_q),
        grid=(batch // n_seq, nq),
        in_specs=[
            pl.BlockSpec((tq, IDXQ_COLS), lambda bi, i: (bi * nq + i, 0)),
            wit_spec,
            pl.BlockSpec((tq, Q_COLS), lambda bi, i: (bi * nq + i, 0)),
            per_sequence(IDX_DIM, lp), per_sequence(KV_COLS, lp), vt_spec,
        ],
        out_specs=out_spec,
        out_shape=out_shape,
        scratch_shapes=[
            pltpu.VMEM((lp, tq), jnp.float32),
            pltpu.VMEM((kt, tq), jnp.float32),
            pltpu.VMEM((spare, tq), jnp.float32),
            pltpu.VMEM((N_HEADS, kt, tq), jnp.float32),
            pltpu.VMEM((N_HEADS, spare, tq), jnp.float32),
            pltpu.VMEM((N_HEADS, HEAD_DIM, tq), jnp.float32),
        ],
        compiler_params=_params(("parallel", "arbitrary")),
        name=f"attn_t{t}_tq{tq}",
    )(qi, wit, q, kidx, k, vt)
    if transpose_in_kernel:
        return out
    return out.transpose(0, 2, 1).reshape(batch * t, Q_COLS)


def _merge_kernel(h_ref, c_ref, halo_ref, at_ref, ga_ref, gc_ref, cw_ref, cb_ref, lng_ref, lnb_ref,
                  wco_ref, wao_ref, wo_ref, o_ref, win_ref, shift_ref, *, seq_len):
    tm = c_ref.shape[0]
    n_chunk = tm // CHUNK
    lead = _HALO - (CONV_WIDTH - 1)
    dcs = []
    for j in range(n_chunk):
        if seq_len == CHUNK:
            history = halo_ref[j]
        elif j == 0:
            starts_sequence = (pl.program_id(0) * tm) % seq_len == 0
            history = jnp.where(starts_sequence, 0.0, halo_ref[...])
        else:
            history = c_ref[j * CHUNK - _HALO:j * CHUNK, :]
        win_ref[j, 0:_HALO, :] = history
        win_ref[j, _HALO:_HALO + CHUNK, :] = c_ref[j * CHUNK:(j + 1) * CHUNK, :]
        for s in range(1, SUBLANES):
            shift_ref[j, s - 1] = win_ref[j, s:s + shift_ref.shape[2], :]
        acc = jnp.zeros((CHUNK, CONV_CH), jnp.float32)
        for tap in range(CONV_WIDTH):
            s, base = (lead + tap) % SUBLANES, (lead + tap) // SUBLANES * SUBLANES
            rows = win_ref[j, base:base + CHUNK, :] if s == 0 else shift_ref[j, s - 1, base:base + CHUNK, :]
            acc = acc + rows * cw_ref[tap:tap + 1, :]
        dcs.append(acc)
    dc = jnp.concatenate(dcs, axis=0) + cb_ref[...]
    mu = jnp.mean(dc, axis=-1, keepdims=True)
    var = jnp.mean(jnp.square(dc - mu), axis=-1, keepdims=True)
    y = (dc - mu) * lax.rsqrt(var + EPS) * lng_ref[...] + lnb_ref[...]
    conv_out = _dot((y * jax.nn.sigmoid(y)).astype(wco_ref.dtype), wco_ref[...])
    attn_out = _dot(at_ref[...], wao_ref[...])
    merged = ga_ref[...] * attn_out + gc_ref[...] * conv_out
    o_ref[...] = h_ref[...] + _dot(merged.astype(wo_ref.dtype), wo_ref[...])


def _merge(h, c, state, attn, ga, gc, weights, tm, seq_len):
    n = h.shape[0]
    cw, conv_b, ln_g, ln_b, wco, wao, wo = weights

    def row(width):
        return pl.BlockSpec((tm, width), lambda i: (i, 0))

    if seq_len == CHUNK:
        halo, halo_spec = state, pl.BlockSpec((tm // CHUNK, _HALO, CONV_CH), lambda i: (i, 0, 0))
    else:
        assert state is None and seq_len % tm == 0
        per_tile = tm // _HALO
        halo, halo_spec = c, pl.BlockSpec((_HALO, CONV_CH), lambda i: (jnp.maximum(i * per_tile - 1, 0), 0))
    vec = _resident((1, CONV_CH))
    return pl.pallas_call(
        functools.partial(_merge_kernel, seq_len=seq_len),
        grid=(n // tm,),
        in_specs=[
            row(D_MODEL), row(CONV_CH), halo_spec,
            row(Q_COLS), row(D_MODEL), row(D_MODEL),
            _resident(cw.shape), vec, vec, vec,
            _resident((CONV_CH, D_MODEL)), _resident((Q_COLS, D_MODEL)), _resident((D_MODEL, D_MODEL)),
        ],
        out_specs=row(D_MODEL),
        out_shape=jax.ShapeDtypeStruct((n, D_MODEL), jnp.float32),
        scratch_shapes=[pltpu.VMEM((tm // CHUNK, _HALO + CHUNK, CONV_CH), jnp.float32),
                        pltpu.VMEM((tm // CHUNK, SUBLANES - 1, _HALO + CHUNK - SUBLANES, CONV_CH), jnp.float32)],
        compiler_params=_params(("parallel",)),
        name=f"merge_n{n}",
    )(h, c, halo, attn, ga, gc, cw, conv_b, ln_g, ln_b, wco, wao, wo)


def _row_tile(n, candidates):
    for tm in candidates:
        if n % tm == 0:
            return tm
    raise ValueError(f"row count {n} is not a multiple of {candidates[-1]}")


def _rope_tables(pos, rows):
    inv = ROPE_THETA ** (-jnp.arange(0, HEAD_DIM, 2, dtype=jnp.float32) / HEAD_DIM)
    ang = pos.astype(jnp.float32)[:, None] * inv[None, :]
    cos, sin = jnp.cos(ang), jnp.sin(ang)
    reps = rows // pos.shape[0]
    return (jnp.tile(cos, (reps, 4)), jnp.tile(jnp.concatenate([-sin, sin], axis=1), (reps, 2)))


def _main_weight(w_in):
    sizes = (Q_COLS, KV_COLS, KV_COLS, IDXQ_COLS, IDX_DIM, IDX_HEADS, 2 * CONV_CH, 2 * D_MODEL)
    offs = [0]
    for s in sizes:
        offs.append(offs[-1] + s)
    wq, wk, wv, wqi, wki, wwi, wconv, wgate = (w_in[:, offs[i]:offs[i + 1]] for i in range(len(sizes)))
    pad = jnp.zeros((D_MODEL, LANES - IDX_DIM - IDX_HEADS), w_in.dtype)
    return jnp.concatenate([wq, wqi, wk, wki, wwi, pad, wv, wconv, wgate], axis=1).astype(_MXU_DTYPE)


def _layer(x, caches, weights, final_norm):
    (ffn1_norm, ffn1_w, mix_norm, w_main, b_gate, merge_w, ffn2_norm, ffn2_w) = weights
    b, t, _ = x.shape
    n = b * t
    assert t % CHUNK == 0
    tm = _row_tile(n, (512, 256, 128, 64))
    tm_merge = _row_tile(n, (256, 128, 64))
    past = 0 if caches is None else caches[0].shape[1]

    h = _ffn(x.reshape(n, D_MODEL), ffn1_norm, ffn1_w, None, tm)
    if t >= tm:
        assert t % tm == 0
    else:
        assert tm % t == 0
    cos, sin = _rope_tables(past + jnp.arange(t, dtype=jnp.int32), max(t, tm))
    q, qi, k, ki, v, c, ga, gc, kb, kib, vt, wit = _proj(h, mix_norm, w_main, b_gate, cos, sin, tm)

    if caches is None:
        tq = _row_tile(t, (512, 256, 128, 64))
        attn = _attn(qi, wit, q, kib.reshape(b, t, IDX_DIM), kb.reshape(b, t, KV_COLS), vt,
                     seq_len=t, past=0, tq=tq, kt=tq)
        state = None
        conv_tail = c.reshape(b, t, CONV_CH)[:, t - (CONV_WIDTH - 1):]
    else:
        cache_k, cache_v, cache_idx_k, state_conv = caches
        n_keys = past + t
        pad_keys = -n_keys % LANES

        def with_cache(cache, new):
            cache = cache.reshape(b, past, -1).astype(_MXU_DTYPE)
            pad = jnp.zeros((b, pad_keys, cache.shape[2]), _MXU_DTYPE)
            return jnp.concatenate([cache, new.reshape(b, t, -1), pad], axis=1)

        vt_all = with_cache(cache_v, vt.reshape(KV_COLS, b, t).transpose(1, 2, 0)).transpose(0, 2, 1)
        side_by_side = next(g for g in (4, 2, 1) if b % g == 0)
        attn = _attn(qi, wit, q, with_cache(cache_idx_k, kib), with_cache(cache_k, kb), vt_all,
                     seq_len=t, past=past, tq=side_by_side * t, kt=n_keys + pad_keys)
        lead = jnp.zeros((b, _HALO - (CONV_WIDTH - 1), CONV_CH), jnp.float32)
        state = jnp.concatenate([lead, state_conv], axis=1)
        conv_tail = jnp.concatenate([state_conv, c.reshape(b, t, CONV_CH)], axis=1)[:, -(CONV_WIDTH - 1):]
    if t != CHUNK:
        assert state is None, "a carried conv state is supported for 64-row sequences only"

    h2 = _merge(h, c, state, attn, ga, gc, merge_w, tm_merge, t)
    y = _ffn(h2, ffn2_norm, ffn2_w, final_norm, tm)
    return (y.reshape(b, t, D_MODEL),
            k.reshape(1, b, t, N_KV_HEADS, HEAD_DIM), v.reshape(1, b, t, N_KV_HEADS, HEAD_DIM),
            ki.reshape(1, b, t, IDX_DIM), conv_tail[None])


def kernel(x_prompt, x_sample, cache_k, cache_v, cache_idx_k, state_conv, ffn1_norm, ffn1_w_in, ffn1_w_out, mix_norm, w_in, b_gate, conv_w, conv_b, conv_ln_g, conv_ln_b, conv_w_out, attn_w_out, w_out, ffn2_norm, ffn2_w_in, ffn2_w_out, final_norm):
    assert ffn1_norm.shape[0] == 1, "one layer"
    assert x_prompt.shape[1] >= CONV_WIDTH - 1
    conv_taps = jnp.zeros((_HALO, CONV_CH), jnp.float32).at[:CONV_WIDTH].set(conv_w[0])
    merge_w = (conv_taps, conv_b[0].reshape(1, CONV_CH), conv_ln_g[0].reshape(1, CONV_CH),
               conv_ln_b[0].reshape(1, CONV_CH), conv_w_out[0].astype(_MXU_DTYPE),
               attn_w_out[0].astype(_MXU_DTYPE), w_out[0].astype(_MXU_DTYPE))
    weights = (ffn1_norm[0], _ffn_weights(ffn1_w_in[0], ffn1_w_out[0]), mix_norm[0],
               _main_weight(w_in[0]), b_gate[0], merge_w,
               ffn2_norm[0], _ffn_weights(ffn2_w_in[0], ffn2_w_out[0]))
    y_p, k_p, v_p, ki_p, conv_p = _layer(x_prompt, None, weights, final_norm)
    y_s, k_s, v_s, ki_s, conv_s = _layer(
        x_sample, (cache_k[0], cache_v[0], cache_idx_k[0], state_conv[0]), weights, final_norm)
    return (y_p, y_s, k_p, v_p, ki_p, conv_p, k_s, v_s, ki_s, conv_s)
```

```python
import functools
import math

import jax
import jax.numpy as jnp
from jax import lax
from jax.experimental import pallas as pl
from jax.experimental.pallas import tpu as pltpu

D_MODEL = 1024
CHUNK = 64
N_HEADS = 8
N_KV_HEADS = 2
HEAD_DIM = 64
GROUP = N_HEADS // N_KV_HEADS
IDX_HEADS = 8
IDX_DIM = 64
TOPK_MAX = 256
CONV_CH = 512
CONV_WIDTH = 31
D_FF = 2816
ROPE_THETA = 10000.0
EPS = 1e-6

Q_COLS = N_HEADS * HEAD_DIM
KV_COLS = N_KV_HEADS * HEAD_DIM
IDXQ_COLS = IDX_HEADS * IDX_DIM

LANES = 128
SUBLANES = 8
_MXU_DTYPE = jnp.bfloat16
_VMEM_LIMIT = 56 * 1024 * 1024
_FF_CHUNK = 256
_HALO = 32
_NEG = -1e30
_Q_SCALE = HEAD_DIM ** -0.5 * math.log2(math.e)

_C_Q = 0
_C_QI = _C_Q + Q_COLS
_C_K = _C_QI + IDXQ_COLS
_C_KW = _C_K + KV_COLS
_C_V = _C_KW + LANES
_C_CA = _C_V + KV_COLS
_C_CB = _C_CA + CONV_CH
_C_GA = _C_CB + CONV_CH
_C_GC = _C_GA + D_MODEL
_C_END = _C_GC + D_MODEL

_KEY_NEG_FLT_MAX = -2139095040
_KEY_POS_INF = 0x7F800000
_KEY_MIN_NORMAL = 0x00800000
_MAX_SEARCH_PASSES = 96
_PROBE_MARGIN = 0.1


def _params(sem):
    return pltpu.CompilerParams(dimension_semantics=sem, vmem_limit_bytes=_VMEM_LIMIT)


def _resident(shape):
    nd = len(shape)
    return pl.BlockSpec(shape, lambda *_: (0,) * nd, pipeline_mode=pl.Buffered(1))


def _rms(x, g):
    return x * lax.rsqrt(jnp.mean(x * x, axis=-1, keepdims=True) + EPS) * g


def _dot(a, b):
    return jnp.dot(a, b, preferred_element_type=jnp.float32)


def _dot_nt(a, b):
    return lax.dot_general(a, b, (((1,), (1,)), ((), ())), preferred_element_type=jnp.float32)


def _ffn_kernel(*refs, final_norm):
    if final_norm:
        x_ref, g_ref, wa_ref, wb_ref, wo_ref, gf_ref, o_ref = refs
    else:
        x_ref, g_ref, wa_ref, wb_ref, wo_ref, o_ref = refs
    x = x_ref[...]
    u = _rms(x, g_ref[...]).astype(wa_ref.dtype)
    acc = jnp.zeros_like(x)
    for c in range(D_FF // _FF_CHUNK):
        sl = slice(c * _FF_CHUNK, (c + 1) * _FF_CHUNK)
        a = _dot(u, wa_ref[:, sl])
        b = _dot(u, wb_ref[:, sl])
        act = (a * jax.nn.sigmoid(a) * b).astype(wo_ref.dtype)
        acc = acc + _dot(act, wo_ref[sl, :])
    h = x + 0.5 * acc
    if final_norm:
        h = _rms(h, gf_ref[...])
    o_ref[...] = h


def _ffn_weights(w_in, w_out):
    return w_in[:, :D_FF].astype(_MXU_DTYPE), w_in[:, D_FF:].astype(_MXU_DTYPE), w_out.astype(_MXU_DTYPE)


def _ffn(x, g, weights, g_final, tm):
    n = x.shape[0]
    wa, wb, wo = weights
    row = pl.BlockSpec((tm, D_MODEL), lambda i: (i, 0))
    vec = _resident((1, D_MODEL))
    in_specs = [row, vec, _resident(wa.shape), _resident(wb.shape), _resident(wo.shape)]
    args = [x, g.reshape(1, D_MODEL), wa, wb, wo]
    if g_final is not None:
        in_specs.append(vec)
        args.append(g_final.reshape(1, D_MODEL))
    return pl.pallas_call(
        functools.partial(_ffn_kernel, final_norm=g_final is not None),
        grid=(n // tm,),
        in_specs=in_specs,
        out_specs=row,
        out_shape=jax.ShapeDtypeStruct((n, D_MODEL), jnp.float32),
        compiler_params=_params(("parallel",)),
        name=f"ffn_final_n{n}" if g_final is not None else f"ffn_n{n}",
    )(*args)


def _swap_halves(x):
    lane = lax.broadcasted_iota(jnp.int32, x.shape, 1)
    first = (lane & (HEAD_DIM - 1)) < HEAD_DIM // 2
    return jnp.where(first, pltpu.roll(x, LANES - HEAD_DIM // 2, 1), pltpu.roll(x, HEAD_DIM // 2, 1))


def _rope(x, cos, sin_signed):
    tiles = []
    for j in range(x.shape[1] // LANES):
        xt = x[:, j * LANES:(j + 1) * LANES]
        tiles.append(xt * cos + _swap_halves(xt) * sin_signed)
    return tiles[0] if len(tiles) == 1 else jnp.concatenate(tiles, axis=1)


def _proj_kernel(h_ref, g_ref, w_ref, bg_ref, cos_ref, sin_ref,
                 q_ref, qi_ref, k_ref, ki_ref, v_ref, c_ref, ga_ref, gc_ref,
                 kb_ref, kib_ref, vt_ref, wit_ref):
    u = _rms(h_ref[...], g_ref[...]).astype(w_ref.dtype)
    cos = cos_ref[...]
    sin = sin_ref[...]

    def proj(lo, hi):
        return _dot(u, w_ref[:, lo:hi])

    q_ref[...] = (_rope(proj(_C_Q, _C_QI), cos, sin) * _Q_SCALE).astype(q_ref.dtype)
    qi_ref[...] = _rope(proj(_C_QI, _C_K), cos, sin).astype(qi_ref.dtype)
    k = _rope(proj(_C_K, _C_KW), cos, sin)
    k_ref[...] = k
    kb_ref[...] = k.astype(kb_ref.dtype)
    zkw = proj(_C_KW, _C_V)
    ki = _rope(zkw, cos, sin)[:, :IDX_DIM]
    ki_ref[...] = ki
    kib_ref[...] = ki.astype(kib_ref.dtype)
    wit_ref[...] = (zkw * (IDX_HEADS ** -0.5)).T[IDX_DIM:IDX_DIM + IDX_HEADS, :]
    v = proj(_C_V, _C_CA)
    v_ref[...] = v
    vt_ref[...] = v.T.astype(vt_ref.dtype)
    c_ref[...] = proj(_C_CA, _C_CB) * jax.nn.sigmoid(proj(_C_CB, _C_GA))
    ga_ref[...] = jax.nn.sigmoid(proj(_C_GA, _C_GC) + bg_ref[:, :D_MODEL])
    gc_ref[...] = jax.nn.sigmoid(proj(_C_GC, _C_END) + bg_ref[:, D_MODEL:])


def _proj(h, g, w_main, b_gate, cos, sin, tm):
    n = h.shape[0]
    n_tab = cos.shape[0] // tm

    def row(width):
        return pl.BlockSpec((tm, width), lambda i: (i, 0))

    def col(height):
        return pl.BlockSpec((height, tm), lambda i: (0, i))

    table = pl.BlockSpec((tm, LANES), lambda i: (i % n_tab, 0))
    f32 = jnp.float32
    out_shape = (
        jax.ShapeDtypeStruct((n, Q_COLS), _MXU_DTYPE),
        jax.ShapeDtypeStruct((n, IDXQ_COLS), _MXU_DTYPE),
        jax.ShapeDtypeStruct((n, KV_COLS), f32),
        jax.ShapeDtypeStruct((n, IDX_DIM), f32),
        jax.ShapeDtypeStruct((n, KV_COLS), f32),
        jax.ShapeDtypeStruct((n, CONV_CH), f32),
        jax.ShapeDtypeStruct((n, D_MODEL), f32),
        jax.ShapeDtypeStruct((n, D_MODEL), f32),
        jax.ShapeDtypeStruct((n, KV_COLS), _MXU_DTYPE),
        jax.ShapeDtypeStruct((n, IDX_DIM), _MXU_DTYPE),
        jax.ShapeDtypeStruct((KV_COLS, n), _MXU_DTYPE),
        jax.ShapeDtypeStruct((IDX_HEADS, n), f32),
    )
    out_specs = tuple(row(s.shape[1]) for s in out_shape[:10]) + (col(KV_COLS), col(IDX_HEADS))
    return pl.pallas_call(
        _proj_kernel,
        grid=(n // tm,),
        in_specs=[row(D_MODEL), _resident((1, D_MODEL)), _resident(w_main.shape),
                  _resident((1, 2 * D_MODEL)), table, table],
        out_specs=out_specs,
        out_shape=out_shape,
        compiler_params=_params(("parallel",)),
        name=f"proj_n{n}",
    )(h, g.reshape(1, D_MODEL), w_main, b_gate.reshape(1, 2 * D_MODEL), cos, sin)


def _float_to_key(x):
    bits = lax.bitcast_convert_type(x, jnp.int32)
    return jnp.where(bits >= 0, bits, bits ^ 0x7FFFFFFF)


def _key_to_float(key):
    return lax.bitcast_convert_type(jnp.where(key >= 0, key, key ^ 0x7FFFFFFF), jnp.float32)


def _normal_quantile(p):
    tail = jnp.minimum(p, 1.0 - p)
    t = jnp.sqrt(-2.0 * jnp.log(tail))
    z = t - (2.515517 + t * (0.802853 + t * 0.010328)) / (1.0 + t * (1.432788 + t * (0.189269 + t * 0.001308)))
    return jnp.where(p < 0.5, z, -z)


def _kth_largest(s_ref, count_ge, n_valid, top, kt):
    cols = s_ref.shape[1]
    shape = (1, cols)
    head = s_ref[0:min(kt, 256), :]
    mean = head.mean(axis=0, keepdims=True)
    std = jnp.sqrt(jnp.maximum((head * head).mean(axis=0, keepdims=True) - mean * mean, 1e-30))
    guess = mean + std * _normal_quantile(jnp.minimum((top - 0.5) / jnp.maximum(n_valid, 1.0), 0.999))
    log_top = jnp.log(top - 0.5)

    def adjacent(lo_k, hi_k):
        return (hi_k <= lo_k + 1) | ((lo_k == 0) & (hi_k == _KEY_MIN_NORMAL))

    def active(state):
        _, lo_k, hi_k, clo = state[:4]
        return (clo > top) & jnp.logical_not(adjacent(lo_k, hi_k))

    def cond(state):
        return (state[0] < _MAX_SEARCH_PASSES) & (jnp.max(jnp.where(active(state), 1.0, 0.0)) > 0.0)

    def body(state):
        it, lo_k, hi_k, clo, chi, lo_real, hi_real, run, last_up = state
        lo_v, hi_v = _key_to_float(lo_k), _key_to_float(hi_k)
        both = (lo_real > 0) & (hi_real > 0)
        log_lo = jnp.log(clo)
        frac = (log_lo - log_top) / (log_lo - jnp.log(jnp.maximum(chi, 0.5)))
        frac = jnp.where(clo - chi > 16.0, jnp.clip(frac, _PROBE_MARGIN, 1.0 - _PROBE_MARGIN), 0.5)
        step = std * 0.25 * lax.shift_left(jnp.int32(1), jnp.minimum(run, 20)).astype(jnp.float32)
        t = jnp.where(both, lo_v + (hi_v - lo_v) * frac, jnp.where(lo_real > 0, lo_v + step, hi_v - step))
        t = jnp.where(it == 0, guess, t)
        t_k = _float_to_key(t)
        mid_k = (lo_k >> 1) + (hi_k >> 1) + (lo_k & hi_k & 1)
        bisect = (both & (run >= 2)) | (it >= 24)
        t_k = jnp.where(bisect, mid_k, t_k)
        t_k = jnp.where((lo_k < 0) & (hi_k > _KEY_MIN_NORMAL) & (it >= 3), 0, t_k)
        t_k = jnp.where((lo_k == 0) & (hi_k > _KEY_MIN_NORMAL), _KEY_MIN_NORMAL, t_k)
        t_k = jnp.minimum(jnp.maximum(t_k, lo_k + 1), hi_k - 1)
        c = count_ge(_key_to_float(t_k))
        live = active(state)
        up = c >= top
        go_lo, go_hi = live & up, live & jnp.logical_not(up)
        same = jnp.where(up, 1, -1) == last_up
        return (it + 1,
                jnp.where(go_lo, t_k, lo_k), jnp.where(go_hi, t_k, hi_k),
                jnp.where(go_lo, c, clo), jnp.where(go_hi, c, chi),
                jnp.where(go_lo, 1, lo_real), jnp.where(go_hi, 1, hi_real),
                jnp.where(bisect, 0, jnp.where(same, run + 1, 1)),
                jnp.where(up, 1, -1))

    zeros = jnp.zeros(shape, jnp.int32)
    init = (jnp.int32(0), jnp.full(shape, _KEY_NEG_FLT_MAX, jnp.int32), jnp.full(shape, _KEY_POS_INF, jnp.int32),
            n_valid, jnp.zeros(shape, jnp.float32), zeros, zeros, zeros, zeros)
    out = lax.while_loop(cond, body, init)
    return _key_to_float(out[1]), out[3]


def _count_ge(ref, x, n_pairs, rows):
    cols = ref.shape[1]

    def tile(j, acc):
        blk = ref[pl.ds(pl.multiple_of(j * rows, rows), rows), :]
        ones = jnp.where(blk >= x, 1.0, 0.0)
        return acc + ones.reshape(rows // 8, 8, cols).sum(axis=0)

    def pair(jj, acc):
        return tile(2 * jj + 1, tile(2 * jj, acc))

    acc = lax.fori_loop(0, n_pairs, pair, jnp.zeros((8, cols), jnp.float32))
    return acc.sum(axis=0, keepdims=True)


def _attn_kernel(qi_ref, wit_ref, q_ref, kidx_ref, k_ref, vt_ref, o_ref,
                 s_ref, bias0_ref, bias1_ref, lg0_ref, lg1_ref, acc_ref, *, past, tq, kt, top, seq_q):
    n_seq = tq // seq_q
    t0 = pl.program_id(1) * seq_q
    n_kt = (past + t0 + seq_q + kt - 1) // kt
    lane_q = lax.broadcasted_iota(jnp.int32, (1, tq), 1)
    limit = past + ((t0 + lane_q % seq_q) // CHUNK + 1) * CHUNK
    w = wit_ref[...]

    own = (lax.broadcasted_iota(jnp.int32, (tq, n_seq * HEAD_DIM), 0) // seq_q
           == lax.broadcasted_iota(jnp.int32, (tq, n_seq * HEAD_DIM), 1) // HEAD_DIM)

    def keys_times_queries(keys, queries_ref, cols):
        qh = queries_ref[:, cols]
        if n_seq > 1:
            qh = jnp.where(own, jnp.concatenate([qh] * n_seq, axis=1), jnp.zeros((), qh.dtype))
        return _dot_nt(keys, qh)

    def score_rows(off, rows, last):
        acc = jnp.zeros((rows, tq), jnp.float32)
        for h in range(IDX_HEADS):
            s = keys_times_queries(kidx_ref[0, pl.ds(off, rows), :], qi_ref, slice(h * IDX_DIM, (h + 1) * IDX_DIM))
            acc = acc + w[h:h + 1, :] * jnp.maximum(s, 0.0)
        if last:
            key = off + lax.broadcasted_iota(jnp.int32, (rows, tq), 0)
            acc = jnp.where(key < limit, acc, -jnp.inf)
        s_ref[pl.ds(off, rows), :] = acc

    def score_pair(jj, carry):
        score_rows(pl.multiple_of(jj * 2 * kt, 2 * kt), 2 * kt, False)
        return carry

    n_full = n_kt - 1
    lax.fori_loop(0, n_full // 2, score_pair, 0)

    @pl.when(n_full % 2 == 1)
    def _():
        score_rows(pl.multiple_of((n_full - 1) * kt, kt), kt, False)

    score_rows(pl.multiple_of(n_full * kt, kt), kt, True)

    def count_ge(thr):
        return _count_ge(s_ref, thr, n_kt, kt // 2)

    thr, cnt = _kth_largest(s_ref, count_ge, limit.astype(jnp.float32), top, kt)

    def resolve_ties(cnt):
        excess = cnt - top
        row = lax.broadcasted_iota(jnp.int32, (kt, kt), 0)
        col = lax.broadcasted_iota(jnp.int32, (kt, kt), 1)
        at_or_after = jnp.where(col >= row, 1.0, 0.0).astype(_MXU_DTYPE)

        def tile(jr, after):
            rows = pl.ds(pl.multiple_of((n_kt - 1 - jr) * kt, kt), kt)
            blk = s_ref[rows, :]
            equal = blk == thr
            suffix = after + _dot(at_or_after, jnp.where(equal, 1.0, 0.0).astype(_MXU_DTYPE))
            s_ref[rows, :] = jnp.where(equal, jnp.where(suffix <= excess, -jnp.inf, blk), blk)
            return suffix[0:1, :]

        lax.fori_loop(0, n_kt, tile, jnp.zeros((1, tq), jnp.float32))
        return count_ge(thr)

    cnt = lax.cond(jnp.max(cnt) > top, resolve_ties, lambda c: c, cnt)

    def drop_lowest(cnt):
        over = cnt > top

        def tile(j):
            off = pl.multiple_of(j * kt, kt)
            return off, s_ref[pl.ds(off, kt), :]

        def min_body(j, vmin):
            _, blk = tile(j)
            cand = jnp.where(blk >= thr, blk, jnp.inf)
            return jnp.minimum(vmin, cand.reshape(kt // 8, 8, tq).min(axis=0))

        vmin = lax.fori_loop(0, n_kt, min_body, jnp.full((8, tq), jnp.inf, jnp.float32))
        vmin = vmin.min(axis=0, keepdims=True)

        def idx_body(j, imax):
            off, blk = tile(j)
            key_idx = off + lax.broadcasted_iota(jnp.int32, (kt, tq), 0)
            cand = jnp.where(blk == vmin, key_idx, -1)
            return jnp.maximum(imax, cand.reshape(kt // 8, 8, tq).max(axis=0))

        imax = lax.fori_loop(0, n_kt, idx_body, jnp.full((8, tq), -1, jnp.int32))
        kill = jnp.where(over, imax.max(axis=0, keepdims=True), -1)

        def kill_body(j, carry):
            off, blk = tile(j)
            key_idx = off + lax.broadcasted_iota(jnp.int32, (kt, tq), 0)
            s_ref[pl.ds(off, kt), :] = jnp.where(key_idx == kill, -jnp.inf, blk)
            return carry

        lax.fori_loop(0, n_kt, kill_body, 0)
        return jnp.where(over, cnt - 1.0, cnt)

    lax.while_loop(lambda cnt: jnp.max(cnt) > top, drop_lowest, cnt)

    acc_ref[...] = jnp.zeros(acc_ref.shape, jnp.float32)

    def logits_sweep(j, m, bias_buf, lg_buf):
        off = pl.multiple_of(jnp.minimum(j, n_kt - 1) * kt, kt)
        bias_buf[...] = jnp.where(s_ref[pl.ds(off, kt), :] >= jnp.where(j < n_kt, thr, jnp.inf), 0.0, _NEG)
        m_new = []
        for h in range(N_HEADS):
            c = h // GROUP
            kx = (k_ref[0, c, pl.ds(off, kt), :] if n_seq > 1
                  else k_ref[0, pl.ds(off, kt), c * HEAD_DIM:(c + 1) * HEAD_DIM])
            lg = keys_times_queries(kx, q_ref, slice(h * HEAD_DIM, (h + 1) * HEAD_DIM)) + bias_buf[...]
            lg_buf[h] = lg
            m_new.append(jnp.maximum(m[h], lg.max(axis=0, keepdims=True)))
        return tuple(m_new)

    def softmax_sweep(j, m_old, m_new, l, lg_buf):
        off = pl.multiple_of(jnp.minimum(j, n_kt - 1) * kt, kt)
        l_new = []
        for h in range(N_HEADS):
            c = h // GROUP
            alpha = jnp.exp2(m_old[h] - m_new[h])
            p = jnp.exp2(lg_buf[h] - m_new[h])
            if n_seq == 1:
                pv = _dot(vt_ref[0, c * HEAD_DIM:(c + 1) * HEAD_DIM, pl.ds(off, kt)], p.astype(vt_ref.dtype))
            else:
                full = _dot(vt_ref[0, c, :, pl.ds(off, kt)], p.astype(vt_ref.dtype))
                lane_seq = lax.broadcasted_iota(jnp.int32, (HEAD_DIM, tq), 1) // seq_q
                pv = sum(jnp.where(lane_seq == g, full[g * HEAD_DIM:(g + 1) * HEAD_DIM, :], 0.0)
                         for g in range(n_seq))
            acc_ref[h] = alpha * acc_ref[h] + pv
            l_new.append(alpha * l[h] + p.sum(axis=0, keepdims=True))
        return tuple(l_new)

    def attn_step(jj, carry):
        m_a, m_b, l = carry
        m_c = logits_sweep(2 * jj + 1, m_b, bias1_ref, lg1_ref)
        l = softmax_sweep(2 * jj, m_a, m_b, l, lg0_ref)
        m_d = logits_sweep(2 * jj + 2, m_c, bias0_ref, lg0_ref)
        l = softmax_sweep(2 * jj + 1, m_b, m_c, l, lg1_ref)
        return m_c, m_d, l

    m_init = tuple(jnp.full((1, tq), _NEG, jnp.float32) for _ in range(N_HEADS))
    l_init = tuple(jnp.zeros((1, tq), jnp.float32) for _ in range(N_HEADS))
    m_first = logits_sweep(0, m_init, bias0_ref, lg0_ref)
    if kt == s_ref.shape[0]:
        l = softmax_sweep(0, m_init, m_first, l_init, lg0_ref)
    else:
        _, _, l = lax.fori_loop(0, (n_kt + 1) // 2, attn_step, (m_init, m_first, l_init))
    if o_ref.shape[0] == tq:
        o = jnp.concatenate([acc_ref[h] / l[h] for h in range(N_HEADS)], axis=0)
        o_ref[...] = o.T.astype(o_ref.dtype)
    else:
        for h in range(N_HEADS):
            o_ref[h * HEAD_DIM:(h + 1) * HEAD_DIM, :] = (acc_ref[h] / l[h]).astype(o_ref.dtype)


def _attn(qi, wit, q, kidx, k, vt, *, seq_len, past, tq, kt):
    batch, lp = k.shape[0], k.shape[1]
    t = seq_len
    n_seq = max(1, tq // t)
    seq_q = tq // n_seq
    nq = t // seq_q
    n_keys = past + t
    assert batch % n_seq == 0 and t % seq_q == 0 and seq_q % CHUNK == 0 and lp % kt == 0 and lp >= n_keys
    assert t == seq_q or (past == 0 and kt == tq)
    top = min(TOPK_MAX, n_keys // 4)

    def per_block(*dims):
        return pl.BlockSpec((1,) + dims, lambda bi, i: (bi,) + (0,) * len(dims))

    if n_seq > 1:
        nb = batch // n_seq
        kidx = kidx.reshape(nb, n_seq, lp, IDX_DIM).transpose(0, 2, 1, 3).reshape(nb, lp, n_seq * IDX_DIM)
        k = (k.reshape(nb, n_seq, lp, N_KV_HEADS, HEAD_DIM).transpose(0, 3, 2, 1, 4)
             .reshape(nb, N_KV_HEADS, lp, n_seq * HEAD_DIM))
        vt = (vt.reshape(nb, n_seq, N_KV_HEADS, HEAD_DIM, lp).transpose(0, 2, 1, 3, 4)
              .reshape(nb, N_KV_HEADS, n_seq * HEAD_DIM, lp))
        key_specs = [per_block(lp, n_seq * IDX_DIM), per_block(N_KV_HEADS, lp, n_seq * HEAD_DIM),
                     per_block(N_KV_HEADS, n_seq * HEAD_DIM, lp)]
    elif vt.ndim == 2:
        assert lp == t
        vt = vt.reshape(1, KV_COLS, batch * t)
        key_specs = [per_block(lp, IDX_DIM), per_block(lp, KV_COLS),
                     pl.BlockSpec((1, KV_COLS, lp), lambda bi, i: (0, 0, bi))]
    else:
        key_specs = [per_block(lp, IDX_DIM), per_block(lp, KV_COLS), per_block(KV_COLS, lp)]
    transpose_in_kernel = tq % LANES == 0
    if transpose_in_kernel:
        out_shape = jax.ShapeDtypeStruct((batch * t, Q_COLS), _MXU_DTYPE)
        out_spec = pl.BlockSpec((tq, Q_COLS), lambda bi, i: (bi * nq + i, 0))
        wit_spec = pl.BlockSpec((IDX_HEADS, tq), lambda bi, i: (0, bi * nq + i))
    else:
        assert n_seq == 1
        out_shape = jax.ShapeDtypeStruct((batch, Q_COLS, t), _MXU_DTYPE)
        out_spec = pl.BlockSpec((None, Q_COLS, tq), lambda bi, i: (bi, 0, i))
        wit = wit.reshape(IDX_HEADS, batch, t).transpose(1, 0, 2)
        wit_spec = pl.BlockSpec((None, IDX_HEADS, tq), lambda bi, i: (bi, 0, i))
    single_tile = kt == lp
    spare = SUBLANES if single_tile else kt
    out = pl.pallas_call(
        functools.partial(_attn_kernel, past=past, tq=tq, kt=kt, top=top, seq_q=seq_q),
        grid=(batch // n_seq, nq),
        in_specs=[
            pl.BlockSpec((tq, IDXQ_COLS), lambda bi, i: (bi * nq + i, 0)),
            wit_spec,
            pl.BlockSpec((tq, Q_COLS), lambda bi, i: (bi * nq + i, 0)),
        ] + key_specs,
        out_specs=out_spec,
        out_shape=out_shape,
        scratch_shapes=[
            pltpu.VMEM((lp, tq), jnp.float32),
            pltpu.VMEM((kt, tq), jnp.float32),
            pltpu.VMEM((spare, tq), jnp.float32),
            pltpu.VMEM((N_HEADS, kt, tq), jnp.float32),
            pltpu.VMEM((N_HEADS, spare, tq), jnp.float32),
            pltpu.VMEM((N_HEADS, HEAD_DIM, tq), jnp.float32),
        ],
        compiler_params=_params(("parallel", "arbitrary")),
        name=f"attn_t{t}_tq{tq}",
    )(qi, wit, q, kidx, k, vt)
    if transpose_in_kernel:
        return out
    return out.transpose(0, 2, 1).reshape(batch * t, Q_COLS)


def _merge_kernel(h_ref, c_ref, halo_ref, at_ref, ga_ref, gc_ref, cw_ref, cb_ref, lng_ref, lnb_ref,
                  wco_ref, wao_ref, wo_ref, o_ref, win_ref, shift_ref, *, seq_len):
    tm = c_ref.shape[0]
    n_chunk = tm // CHUNK
    lead = _HALO - (CONV_WIDTH - 1)
    dcs = []
    for j in range(n_chunk):
        if seq_len == CHUNK:
            history = halo_ref[j]
        elif j == 0:
            starts_sequence = (pl.program_id(0) * tm) % seq_len == 0
            history = jnp.where(starts_sequence, 0.0, halo_ref[...])
        else:
            history = c_ref[j * CHUNK - _HALO:j * CHUNK, :]
        win_ref[j, 0:_HALO, :] = history
        win_ref[j, _HALO:_HALO + CHUNK, :] = c_ref[j * CHUNK:(j + 1) * CHUNK, :]
        for s in range(1, SUBLANES):
            shift_ref[j, s - 1] = win_ref[j, s:s + shift_ref.shape[2], :]
        acc = jnp.zeros((CHUNK, CONV_CH), jnp.float32)
        for tap in range(CONV_WIDTH):
            s, base = (lead + tap) % SUBLANES, (lead + tap) // SUBLANES * SUBLANES
            rows = win_ref[j, base:base + CHUNK, :] if s == 0 else shift_ref[j, s - 1, base:base + CHUNK, :]
            acc = acc + rows * cw_ref[tap:tap + 1, :]
        dcs.append(acc)
    dc = jnp.concatenate(dcs, axis=0) + cb_ref[...]
    mu = jnp.mean(dc, axis=-1, keepdims=True)
    var = jnp.mean(jnp.square(dc - mu), axis=-1, keepdims=True)
    y = (dc - mu) * lax.rsqrt(var + EPS) * lng_ref[...] + lnb_ref[...]
    conv_out = _dot((y * jax.nn.sigmoid(y)).astype(wco_ref.dtype), wco_ref[...])
    attn_out = _dot(at_ref[...], wao_ref[...])
    merged = ga_ref[...] * attn_out + gc_ref[...] * conv_out
    o_ref[...] = h_ref[...] + _dot(merged.astype(wo_ref.dtype), wo_ref[...])


def _merge(h, c, state, attn, ga, gc, weights, tm, seq_len):
    n = h.shape[0]
    cw, conv_b, ln_g, ln_b, wco, wao, wo = weights

    def row(width):
        return pl.BlockSpec((tm, width), lambda i: (i, 0))

    if seq_len == CHUNK:
        halo, halo_spec = state, pl.BlockSpec((tm // CHUNK, _HALO, CONV_CH), lambda i: (i, 0, 0))
    else:
        assert state is None and seq_len % tm == 0
        per_tile = tm // _HALO
        halo, halo_spec = c, pl.BlockSpec((_HALO, CONV_CH), lambda i: (jnp.maximum(i * per_tile - 1, 0), 0))
    vec = _resident((1, CONV_CH))
    return pl.pallas_call(
        functools.partial(_merge_kernel, seq_len=seq_len),
        grid=(n // tm,),
        in_specs=[
            row(D_MODEL), row(CONV_CH), halo_spec,
            row(Q_COLS), row(D_MODEL), row(D_MODEL),
            _resident(cw.shape), vec, vec, vec,
            _resident((CONV_CH, D_MODEL)), _resident((Q_COLS, D_MODEL)), _resident((D_MODEL, D_MODEL)),
        ],
        out_specs=row(D_MODEL),
        out_shape=jax.ShapeDtypeStruct((n, D_MODEL), jnp.float32),
        scratch_shapes=[pltpu.VMEM((tm // CHUNK, _HALO + CHUNK, CONV_CH), jnp.float32),
                        pltpu.VMEM((tm // CHUNK, SUBLANES - 1, _HALO + CHUNK - SUBLANES, CONV_CH), jnp.float32)],
        compiler_params=_params(("parallel",)),
        name=f"merge_n{n}",
    )(h, c, halo, attn, ga, gc, cw, conv_b, ln_g, ln_b, wco, wao, wo)


def _row_tile(n, candidates):
    for tm in candidates:
        if n % tm == 0:
            return tm
    raise ValueError(f"row count {n} is not a multiple of {candidates[-1]}")


def _rope_tables(pos, rows):
    inv = ROPE_THETA ** (-jnp.arange(0, HEAD_DIM, 2, dtype=jnp.float32) / HEAD_DIM)
    ang = pos.astype(jnp.float32)[:, None] * inv[None, :]
    cos, sin = jnp.cos(ang), jnp.sin(ang)
    reps = rows // pos.shape[0]
    return (jnp.tile(cos, (reps, 4)), jnp.tile(jnp.concatenate([-sin, sin], axis=1), (reps, 2)))


def _main_weight(w_in):
    sizes = (Q_COLS, KV_COLS, KV_COLS, IDXQ_COLS, IDX_DIM, IDX_HEADS, 2 * CONV_CH, 2 * D_MODEL)
    offs = [0]
    for s in sizes:
        offs.append(offs[-1] + s)
    wq, wk, wv, wqi, wki, wwi, wconv, wgate = (w_in[:, offs[i]:offs[i + 1]] for i in range(len(sizes)))
    pad = jnp.zeros((D_MODEL, LANES - IDX_DIM - IDX_HEADS), w_in.dtype)
    return jnp.concatenate([wq, wqi, wk, wki, wwi, pad, wv, wconv, wgate], axis=1).astype(_MXU_DTYPE)


def _layer(x, caches, weights, final_norm):
    (ffn1_norm, ffn1_w, mix_norm, w_main, b_gate, merge_w, ffn2_norm, ffn2_w) = weights
    b, t, _ = x.shape
    n = b * t
    assert t % CHUNK == 0
    tm = _row_tile(n, (512, 256, 128, 64))
    tm_merge = _row_tile(n, (256, 128, 64))
    past = 0 if caches is None else caches[0].shape[1]

    h = _ffn(x.reshape(n, D_MODEL), ffn1_norm, ffn1_w, None, tm)
    if t >= tm:
        assert t % tm == 0
    else:
        assert tm % t == 0
    cos, sin = _rope_tables(past + jnp.arange(t, dtype=jnp.int32), max(t, tm))
    q, qi, k, ki, v, c, ga, gc, kb, kib, vt, wit = _proj(h, mix_norm, w_main, b_gate, cos, sin, tm)

    if caches is None:
        tq = _row_tile(t, (512, 256, 128, 64))
        attn = _attn(qi, wit, q, kib.reshape(b, t, IDX_DIM), kb.reshape(b, t, KV_COLS), vt,
                     seq_len=t, past=0, tq=tq, kt=tq)
        state = None
        conv_tail = c.reshape(b, t, CONV_CH)[:, t - (CONV_WIDTH - 1):]
    else:
        cache_k, cache_v, cache_idx_k, state_conv = caches
        n_keys = past + t
        pad_keys = -n_keys % LANES

        def with_cache(cache, new):
            cache = cache.reshape(b, past, -1).astype(_MXU_DTYPE)
            pad = jnp.zeros((b, pad_keys, cache.shape[2]), _MXU_DTYPE)
            return jnp.concatenate([cache, new.reshape(b, t, -1), pad], axis=1)

        vt_all = with_cache(cache_v, vt.reshape(KV_COLS, b, t).transpose(1, 2, 0)).transpose(0, 2, 1)
        side_by_side = next(g for g in (4, 2, 1) if b % g == 0)
        attn = _attn(qi, wit, q, with_cache(cache_idx_k, kib), with_cache(cache_k, kb), vt_all,
                     seq_len=t, past=past, tq=side_by_side * t, kt=n_keys + pad_keys)
        lead = jnp.zeros((b, _HALO - (CONV_WIDTH - 1), CONV_CH), jnp.float32)
        state = jnp.concatenate([lead, state_conv], axis=1)
        conv_tail = jnp.concatenate([state_conv, c.reshape(b, t, CONV_CH)], axis=1)[:, -(CONV_WIDTH - 1):]
    if t != CHUNK:
        assert state is None, "a carried conv state is supported for 64-row sequences only"

    h2 = _merge(h, c, state, attn, ga, gc, merge_w, tm_merge, t)
    y = _ffn(h2, ffn2_norm, ffn2_w, final_norm, tm)
    return (y.reshape(b, t, D_MODEL),
            k.reshape(1, b, t, N_KV_HEADS, HEAD_DIM), v.reshape(1, b, t, N_KV_HEADS, HEAD_DIM),
            ki.reshape(1, b, t, IDX_DIM), conv_tail[None])


def kernel(x_prompt, x_sample, cache_k, cache_v, cache_idx_k, state_conv, ffn1_norm, ffn1_w_in, ffn1_w_out, mix_norm, w_in, b_gate, conv_w, conv_b, conv_ln_g, conv_ln_b, conv_w_out, attn_w_out, w_out, ffn2_norm, ffn2_w_in, ffn2_w_out, final_norm):
    assert ffn1_norm.shape[0] == 1, "one layer"
    assert x_prompt.shape[1] >= CONV_WIDTH - 1
    conv_taps = jnp.zeros((_HALO, CONV_CH), jnp.float32).at[:CONV_WIDTH].set(conv_w[0])
    merge_w = (conv_taps, conv_b[0].reshape(1, CONV_CH), conv_ln_g[0].reshape(1, CONV_CH),
               conv_ln_b[0].reshape(1, CONV_CH), conv_w_out[0].astype(_MXU_DTYPE),
               attn_w_out[0].astype(_MXU_DTYPE), w_out[0].astype(_MXU_DTYPE))
    weights = (ffn1_norm[0], _ffn_weights(ffn1_w_in[0], ffn1_w_out[0]), mix_norm[0],
               _main_weight(w_in[0]), b_gate[0], merge_w,
               ffn2_norm[0], _ffn_weights(ffn2_w_in[0], ffn2_w_out[0]))
    y_p, k_p, v_p, ki_p, conv_p = _layer(x_prompt, None, weights, final_norm)
    y_s, k_s, v_s, ki_s, conv_s = _layer(
        x_sample, (cache_k[0], cache_v[0], cache_idx_k[0], state_conv[0]), weights, final_norm)
    return (y_p, y_s, k_p, v_p, ki_p, conv_p, k_s, v_s, ki_s, conv_s)
```

```python
import functools
import math

import jax
import jax.numpy as jnp
from jax import lax
from jax.experimental import pallas as pl
from jax.experimental.pallas import tpu as pltpu

D_MODEL = 1024
CHUNK = 64
N_HEADS = 8
N_KV_HEADS = 2
HEAD_DIM = 64
GROUP = N_HEADS // N_KV_HEADS
IDX_HEADS = 8
IDX_DIM = 64
TOPK_MAX = 256
CONV_CH = 512
CONV_WIDTH = 31
D_FF = 2816
ROPE_THETA = 10000.0
EPS = 1e-6

Q_COLS = N_HEADS * HEAD_DIM
KV_COLS = N_KV_HEADS * HEAD_DIM
IDXQ_COLS = IDX_HEADS * IDX_DIM

LANES = 128
SUBLANES = 8
_MXU_DTYPE = jnp.bfloat16
_VMEM_LIMIT = 56 * 1024 * 1024
_FF_CHUNK = 256
_HALO = 32
_NEG = -1e30
_Q_SCALE = HEAD_DIM ** -0.5 * math.log2(math.e)

_C_Q = 0
_C_QI = _C_Q + Q_COLS
_C_K = _C_QI + IDXQ_COLS
_C_KW = _C_K + KV_COLS
_C_V = _C_KW + LANES
_C_CA = _C_V + KV_COLS
_C_CB = _C_CA + CONV_CH
_C_GA = _C_CB + CONV_CH
_C_GC = _C_GA + D_MODEL
_C_END = _C_GC + D_MODEL

_KEY_NEG_FLT_MAX = -2139095040
_KEY_POS_INF = 0x7F800000
_KEY_MIN_NORMAL = 0x00800000
_MAX_SEARCH_PASSES = 96
_PROBE_MARGIN = 0.1


def _params(sem):
    return pltpu.CompilerParams(dimension_semantics=sem, vmem_limit_bytes=_VMEM_LIMIT)


def _resident(shape):
    nd = len(shape)
    return pl.BlockSpec(shape, lambda *_: (0,) * nd, pipeline_mode=pl.Buffered(1))


def _rms(x, g):
    return x * lax.rsqrt(jnp.mean(x * x, axis=-1, keepdims=True) + EPS) * g


def _dot(a, b):
    return jnp.dot(a, b, preferred_element_type=jnp.float32)


def _dot_nt(a, b):
    return lax.dot_general(a, b, (((1,), (1,)), ((), ())), preferred_element_type=jnp.float32)


def _ffn_kernel(*refs, final_norm):
    if final_norm:
        x_ref, g_ref, wa_ref, wb_ref, wo_ref, gf_ref, o_ref = refs
    else:
        x_ref, g_ref, wa_ref, wb_ref, wo_ref, o_ref = refs
    x = x_ref[...]
    u = _rms(x, g_ref[...]).astype(wa_ref.dtype)
    acc = jnp.zeros_like(x)
    for c in range(D_FF // _FF_CHUNK):
        sl = slice(c * _FF_CHUNK, (c + 1) * _FF_CHUNK)
        a = _dot(u, wa_ref[:, sl])
        b = _dot(u, wb_ref[:, sl])
        act = (a * jax.nn.sigmoid(a) * b).astype(wo_ref.dtype)
        acc = acc + _dot(act, wo_ref[sl, :])
    h = x + 0.5 * acc
    if final_norm:
        h = _rms(h, gf_ref[...])
    o_ref[...] = h


def _ffn_weights(w_in, w_out):
    return w_in[:, :D_FF].astype(_MXU_DTYPE), w_in[:, D_FF:].astype(_MXU_DTYPE), w_out.astype(_MXU_DTYPE)


def _ffn(x, g, weights, g_final, tm):
    n = x.shape[0]
    wa, wb, wo = weights
    row = pl.BlockSpec((tm, D_MODEL), lambda i: (i, 0))
    vec = _resident((1, D_MODEL))
    in_specs = [row, vec, _resident(wa.shape), _resident(wb.shape), _resident(wo.shape)]
    args = [x, g.reshape(1, D_MODEL), wa, wb, wo]
    if g_final is not None:
        in_specs.append(vec)
        args.append(g_final.reshape(1, D_MODEL))
    return pl.pallas_call(
        functools.partial(_ffn_kernel, final_norm=g_final is not None),
        grid=(n // tm,),
        in_specs=in_specs,
        out_specs=row,
        out_shape=jax.ShapeDtypeStruct((n, D_MODEL), jnp.float32),
        compiler_params=_params(("parallel",)),
        name=f"ffn_final_n{n}" if g_final is not None else f"ffn_n{n}",
    )(*args)


def _swap_halves(x):
    lane = lax.broadcasted_iota(jnp.int32, x.shape, 1)
    first = (lane & (HEAD_DIM - 1)) < HEAD_DIM // 2
    return jnp.where(first, pltpu.roll(x, LANES - HEAD_DIM // 2, 1), pltpu.roll(x, HEAD_DIM // 2, 1))


def _rope(x, cos, sin_signed):
    tiles = []
    for j in range(x.shape[1] // LANES):
        xt = x[:, j * LANES:(j + 1) * LANES]
        tiles.append(xt * cos + _swap_halves(xt) * sin_signed)
    return tiles[0] if len(tiles) == 1 else jnp.concatenate(tiles, axis=1)


def _proj_kernel(h_ref, g_ref, w_ref, bg_ref, cos_ref, sin_ref,
                 q_ref, qi_ref, k_ref, ki_ref, v_ref, c_ref, ga_ref, gc_ref,
                 kb_ref, kib_ref, vt_ref, wit_ref):
    u = _rms(h_ref[...], g_ref[...]).astype(w_ref.dtype)
    cos = cos_ref[...]
    sin = sin_ref[...]

    def proj(lo, hi):
        return _dot(u, w_ref[:, lo:hi])

    q_ref[...] = (_rope(proj(_C_Q, _C_QI), cos, sin) * _Q_SCALE).astype(q_ref.dtype)
    qi_ref[...] = _rope(proj(_C_QI, _C_K), cos, sin).astype(qi_ref.dtype)
    k = _rope(proj(_C_K, _C_KW), cos, sin)
    for kv in range(N_KV_HEADS):
        k_ref[:, kv, :] = k[:, kv * HEAD_DIM:(kv + 1) * HEAD_DIM]
    kb_ref[...] = k.astype(kb_ref.dtype)
    zkw = proj(_C_KW, _C_V)
    ki = _rope(zkw, cos, sin)[:, :IDX_DIM]
    ki_ref[...] = ki
    kib_ref[...] = ki.astype(kib_ref.dtype)
    wit_ref[...] = (zkw * (IDX_HEADS ** -0.5)).T[IDX_DIM:IDX_DIM + IDX_HEADS, :]
    v = proj(_C_V, _C_CA)
    for kv in range(N_KV_HEADS):
        v_ref[:, kv, :] = v[:, kv * HEAD_DIM:(kv + 1) * HEAD_DIM]
    vt_ref[...] = v.T.astype(vt_ref.dtype)
    c_ref[...] = proj(_C_CA, _C_CB) * jax.nn.sigmoid(proj(_C_CB, _C_GA))
    ga_ref[...] = jax.nn.sigmoid(proj(_C_GA, _C_GC) + bg_ref[:, :D_MODEL])
    gc_ref[...] = jax.nn.sigmoid(proj(_C_GC, _C_END) + bg_ref[:, D_MODEL:])


def _proj(h, g, w_main, b_gate, cos, sin, tm):
    n = h.shape[0]
    n_tab = cos.shape[0] // tm

    def row(width):
        return pl.BlockSpec((tm, width), lambda i: (i, 0))

    def col(height):
        return pl.BlockSpec((height, tm), lambda i: (0, i))

    table = pl.BlockSpec((tm, LANES), lambda i: (i % n_tab, 0))
    f32 = jnp.float32
    out_shape = (
        jax.ShapeDtypeStruct((n, Q_COLS), _MXU_DTYPE),
        jax.ShapeDtypeStruct((n, IDXQ_COLS), _MXU_DTYPE),
        jax.ShapeDtypeStruct((n, N_KV_HEADS, HEAD_DIM), f32),
        jax.ShapeDtypeStruct((n, IDX_DIM), f32),
        jax.ShapeDtypeStruct((n, N_KV_HEADS, HEAD_DIM), f32),
        jax.ShapeDtypeStruct((n, CONV_CH), f32),
        jax.ShapeDtypeStruct((n, D_MODEL), f32),
        jax.ShapeDtypeStruct((n, D_MODEL), f32),
        jax.ShapeDtypeStruct((n, KV_COLS), _MXU_DTYPE),
        jax.ShapeDtypeStruct((n, IDX_DIM), _MXU_DTYPE),
        jax.ShapeDtypeStruct((KV_COLS, n), _MXU_DTYPE),
        jax.ShapeDtypeStruct((IDX_HEADS, n), f32),
    )
    heads = pl.BlockSpec((tm, N_KV_HEADS, HEAD_DIM), lambda i: (i, 0, 0))
    out_specs = tuple(heads if len(s.shape) == 3 else row(s.shape[1]) for s in out_shape[:10]) + (col(KV_COLS), col(IDX_HEADS))
    return pl.pallas_call(
        _proj_kernel,
        grid=(n // tm,),
        in_specs=[row(D_MODEL), _resident((1, D_MODEL)), _resident(w_main.shape),
                  _resident((1, 2 * D_MODEL)), table, table],
        out_specs=out_specs,
        out_shape=out_shape,
        compiler_params=_params(("parallel",)),
        name=f"proj_n{n}",
    )(h, g.reshape(1, D_MODEL), w_main, b_gate.reshape(1, 2 * D_MODEL), cos, sin)


def _float_to_key(x):
    bits = lax.bitcast_convert_type(x, jnp.int32)
    return jnp.where(bits >= 0, bits, bits ^ 0x7FFFFFFF)


def _key_to_float(key):
    return lax.bitcast_convert_type(jnp.where(key >= 0, key, key ^ 0x7FFFFFFF), jnp.float32)


def _normal_quantile(p):
    tail = jnp.minimum(p, 1.0 - p)
    t = jnp.sqrt(-2.0 * jnp.log(tail))
    z = t - (2.515517 + t * (0.802853 + t * 0.010328)) / (1.0 + t * (1.432788 + t * (0.189269 + t * 0.001308)))
    return jnp.where(p < 0.5, z, -z)


def _kth_largest(s_ref, count_ge, n_valid, top, kt):
    cols = s_ref.shape[1]
    shape = (1, cols)
    head = s_ref[0:min(kt, 256), :]
    mean = head.mean(axis=0, keepdims=True)
    std = jnp.sqrt(jnp.maximum((head * head).mean(axis=0, keepdims=True) - mean * mean, 1e-30))
    guess = mean + std * _normal_quantile(jnp.minimum((top - 0.5) / jnp.maximum(n_valid, 1.0), 0.999))
    log_top = jnp.log(top - 0.5)

    def adjacent(lo_k, hi_k):
        return (hi_k <= lo_k + 1) | ((lo_k == 0) & (hi_k == _KEY_MIN_NORMAL))

    def active(state):
        _, lo_k, hi_k, clo = state[:4]
        return (clo > top) & jnp.logical_not(adjacent(lo_k, hi_k))

    def cond(state):
        return (state[0] < _MAX_SEARCH_PASSES) & (jnp.max(jnp.where(active(state), 1.0, 0.0)) > 0.0)

    def body(state):
        it, lo_k, hi_k, clo, chi, lo_real, hi_real, run, last_up = state
        lo_v, hi_v = _key_to_float(lo_k), _key_to_float(hi_k)
        both = (lo_real > 0) & (hi_real > 0)
        log_lo = jnp.log(clo)
        frac = (log_lo - log_top) / (log_lo - jnp.log(jnp.maximum(chi, 0.5)))
        frac = jnp.where(clo - chi > 16.0, jnp.clip(frac, _PROBE_MARGIN, 1.0 - _PROBE_MARGIN), 0.5)
        step = std * 0.25 * lax.shift_left(jnp.int32(1), jnp.minimum(run, 20)).astype(jnp.float32)
        t = jnp.where(both, lo_v + (hi_v - lo_v) * frac, jnp.where(lo_real > 0, lo_v + step, hi_v - step))
        t = jnp.where(it == 0, guess, t)
        t_k = _float_to_key(t)
        mid_k = (lo_k >> 1) + (hi_k >> 1) + (lo_k & hi_k & 1)
        bisect = (both & (run >= 2)) | (it >= 24)
        t_k = jnp.where(bisect, mid_k, t_k)
        t_k = jnp.where((lo_k < 0) & (hi_k > _KEY_MIN_NORMAL) & (it >= 3), 0, t_k)
        t_k = jnp.where((lo_k == 0) & (hi_k > _KEY_MIN_NORMAL), _KEY_MIN_NORMAL, t_k)
        t_k = jnp.minimum(jnp.maximum(t_k, lo_k + 1), hi_k - 1)
        c = count_ge(_key_to_float(t_k))
        live = active(state)
        up = c >= top
        go_lo, go_hi = live & up, live & jnp.logical_not(up)
        same = jnp.where(up, 1, -1) == last_up
        return (it + 1,
                jnp.where(go_lo, t_k, lo_k), jnp.where(go_hi, t_k, hi_k),
                jnp.where(go_lo, c, clo), jnp.where(go_hi, c, chi),
                jnp.where(go_lo, 1, lo_real), jnp.where(go_hi, 1, hi_real),
                jnp.where(bisect, 0, jnp.where(same, run + 1, 1)),
                jnp.where(up, 1, -1))

    zeros = jnp.zeros(shape, jnp.int32)
    init = (jnp.int32(0), jnp.full(shape, _KEY_NEG_FLT_MAX, jnp.int32), jnp.full(shape, _KEY_POS_INF, jnp.int32),
            n_valid, jnp.zeros(shape, jnp.float32), zeros, zeros, zeros, zeros)
    out = lax.while_loop(cond, body, init)
    return _key_to_float(out[1]), out[3]


def _count_ge(ref, x, n_pairs, rows):
    cols = ref.shape[1]

    def tile(j, acc):
        blk = ref[pl.ds(pl.multiple_of(j * rows, rows), rows), :]
        ones = jnp.where(blk >= x, 1.0, 0.0)
        return acc + ones.reshape(rows // 8, 8, cols).sum(axis=0)

    def pair(jj, acc):
        return tile(2 * jj + 1, tile(2 * jj, acc))

    acc = lax.fori_loop(0, n_pairs, pair, jnp.zeros((8, cols), jnp.float32))
    return acc.sum(axis=0, keepdims=True)


def _attn_kernel(qi_ref, wit_ref, q_ref, kidx_ref, k_ref, vt_ref, o_ref,
                 s_ref, bias0_ref, bias1_ref, lg0_ref, lg1_ref, acc_ref, *, past, tq, kt, top, seq_q):
    n_seq = tq // seq_q
    t0 = pl.program_id(1) * seq_q
    n_kt = (past + t0 + seq_q + kt - 1) // kt
    lane_q = lax.broadcasted_iota(jnp.int32, (1, tq), 1)
    limit = past + ((t0 + lane_q % seq_q) // CHUNK + 1) * CHUNK
    w = wit_ref[...]

    own = (lax.broadcasted_iota(jnp.int32, (tq, n_seq * HEAD_DIM), 0) // seq_q
           == lax.broadcasted_iota(jnp.int32, (tq, n_seq * HEAD_DIM), 1) // HEAD_DIM)

    def keys_times_queries(keys, queries_ref, cols):
        qh = queries_ref[:, cols]
        if n_seq > 1:
            qh = jnp.where(own, jnp.concatenate([qh] * n_seq, axis=1), jnp.zeros((), qh.dtype))
        return _dot_nt(keys, qh)

    def score_rows(off, rows, last):
        acc = jnp.zeros((rows, tq), jnp.float32)
        for h in range(IDX_HEADS):
            s = keys_times_queries(kidx_ref[0, pl.ds(off, rows), :], qi_ref, slice(h * IDX_DIM, (h + 1) * IDX_DIM))
            acc = acc + w[h:h + 1, :] * jnp.maximum(s, 0.0)
        if last:
            key = off + lax.broadcasted_iota(jnp.int32, (rows, tq), 0)
            acc = jnp.where(key < limit, acc, -jnp.inf)
        s_ref[pl.ds(off, rows), :] = acc

    def score_pair(jj, carry):
        score_rows(pl.multiple_of(jj * 2 * kt, 2 * kt), 2 * kt, False)
        return carry

    n_full = n_kt - 1
    lax.fori_loop(0, n_full // 2, score_pair, 0)

    @pl.when(n_full % 2 == 1)
    def _():
        score_rows(pl.multiple_of((n_full - 1) * kt, kt), kt, False)

    score_rows(pl.multiple_of(n_full * kt, kt), kt, True)

    def count_ge(thr):
        return _count_ge(s_ref, thr, n_kt, kt // 2)

    thr, cnt = _kth_largest(s_ref, count_ge, limit.astype(jnp.float32), top, kt)

    def resolve_ties(cnt):
        excess = cnt - top
        row = lax.broadcasted_iota(jnp.int32, (kt, kt), 0)
        col = lax.broadcasted_iota(jnp.int32, (kt, kt), 1)
        at_or_after = jnp.where(col >= row, 1.0, 0.0).astype(_MXU_DTYPE)

        def tile(jr, after):
            rows = pl.ds(pl.multiple_of((n_kt - 1 - jr) * kt, kt), kt)
            blk = s_ref[rows, :]
            equal = blk == thr
            suffix = after + _dot(at_or_after, jnp.where(equal, 1.0, 0.0).astype(_MXU_DTYPE))
            s_ref[rows, :] = jnp.where(equal, jnp.where(suffix <= excess, -jnp.inf, blk), blk)
            return suffix[0:1, :]

        lax.fori_loop(0, n_kt, tile, jnp.zeros((1, tq), jnp.float32))
        return count_ge(thr)

    cnt = lax.cond(jnp.max(cnt) > top, resolve_ties, lambda c: c, cnt)

    def drop_lowest(cnt):
        over = cnt > top

        def tile(j):
            off = pl.multiple_of(j * kt, kt)
            return off, s_ref[pl.ds(off, kt), :]

        def min_body(j, vmin):
            _, blk = tile(j)
            cand = jnp.where(blk >= thr, blk, jnp.inf)
            return jnp.minimum(vmin, cand.reshape(kt // 8, 8, tq).min(axis=0))

        vmin = lax.fori_loop(0, n_kt, min_body, jnp.full((8, tq), jnp.inf, jnp.float32))
        vmin = vmin.min(axis=0, keepdims=True)

        def idx_body(j, imax):
            off, blk = tile(j)
            key_idx = off + lax.broadcasted_iota(jnp.int32, (kt, tq), 0)
            cand = jnp.where(blk == vmin, key_idx, -1)
            return jnp.maximum(imax, cand.reshape(kt // 8, 8, tq).max(axis=0))

        imax = lax.fori_loop(0, n_kt, idx_body, jnp.full((8, tq), -1, jnp.int32))
        kill = jnp.where(over, imax.max(axis=0, keepdims=True), -1)

        def kill_body(j, carry):
            off, blk = tile(j)
            key_idx = off + lax.broadcasted_iota(jnp.int32, (kt, tq), 0)
            s_ref[pl.ds(off, kt), :] = jnp.where(key_idx == kill, -jnp.inf, blk)
            return carry

        lax.fori_loop(0, n_kt, kill_body, 0)
        return jnp.where(over, cnt - 1.0, cnt)

    lax.while_loop(lambda cnt: jnp.max(cnt) > top, drop_lowest, cnt)

    acc_ref[...] = jnp.zeros(acc_ref.shape, jnp.float32)

    def logits_sweep(j, m, bias_buf, lg_buf):
        off = pl.multiple_of(jnp.minimum(j, n_kt - 1) * kt, kt)
        bias_buf[...] = jnp.where(s_ref[pl.ds(off, kt), :] >= jnp.where(j < n_kt, thr, jnp.inf), 0.0, _NEG)
        m_new = []
        for h in range(N_HEADS):
            c = h // GROUP
            kx = (k_ref[0, c, pl.ds(off, kt), :] if n_seq > 1
                  else k_ref[0, pl.ds(off, kt), c * HEAD_DIM:(c + 1) * HEAD_DIM])
            lg = keys_times_queries(kx, q_ref, slice(h * HEAD_DIM, (h + 1) * HEAD_DIM)) + bias_buf[...]
            lg_buf[h] = lg
            m_new.append(jnp.maximum(m[h], lg.max(axis=0, keepdims=True)))
        return tuple(m_new)

    def softmax_sweep(j, m_old, m_new, l, lg_buf):
        off = pl.multiple_of(jnp.minimum(j, n_kt - 1) * kt, kt)
        l_new = []
        for h in range(N_HEADS):
            c = h // GROUP
            alpha = jnp.exp2(m_old[h] - m_new[h])
            p = jnp.exp2(lg_buf[h] - m_new[h])
            if n_seq == 1:
                pv = _dot(vt_ref[0, c * HEAD_DIM:(c + 1) * HEAD_DIM, pl.ds(off, kt)], p.astype(vt_ref.dtype))
            else:
                full = _dot(vt_ref[0, c, :, pl.ds(off, kt)], p.astype(vt_ref.dtype))
                lane_seq = lax.broadcasted_iota(jnp.int32, (HEAD_DIM, tq), 1) // seq_q
                pv = sum(jnp.where(lane_seq == g, full[g * HEAD_DIM:(g + 1) * HEAD_DIM, :], 0.0)
                         for g in range(n_seq))
            acc_ref[h] = alpha * acc_ref[h] + pv
            l_new.append(alpha * l[h] + p.sum(axis=0, keepdims=True))
        return tuple(l_new)

    def attn_step(jj, carry):
        m_a, m_b, l = carry
        m_c = logits_sweep(2 * jj + 1, m_b, bias1_ref, lg1_ref)
        l = softmax_sweep(2 * jj, m_a, m_b, l, lg0_ref)
        m_d = logits_sweep(2 * jj + 2, m_c, bias0_ref, lg0_ref)
        l = softmax_sweep(2 * jj + 1, m_b, m_c, l, lg1_ref)
        return m_c, m_d, l

    m_init = tuple(jnp.full((1, tq), _NEG, jnp.float32) for _ in range(N_HEADS))
    l_init = tuple(jnp.zeros((1, tq), jnp.float32) for _ in range(N_HEADS))
    m_first = logits_sweep(0, m_init, bias0_ref, lg0_ref)
    if kt == s_ref.shape[0]:
        l = softmax_sweep(0, m_init, m_first, l_init, lg0_ref)
    else:
        _, _, l = lax.fori_loop(0, (n_kt + 1) // 2, attn_step, (m_init, m_first, l_init))
    if o_ref.shape[0] == tq:
        o = jnp.concatenate([acc_ref[h] / l[h] for h in range(N_HEADS)], axis=0)
        o_ref[...] = o.T.astype(o_ref.dtype)
    else:
        for h in range(N_HEADS):
            o_ref[h * HEAD_DIM:(h + 1) * HEAD_DIM, :] = (acc_ref[h] / l[h]).astype(o_ref.dtype)


def _attn(qi, wit, q, kidx, k, vt, *, seq_len, past, tq, kt):
    batch, lp = k.shape[0], k.shape[1]
    t = seq_len
    n_seq = max(1, tq // t)
    seq_q = tq // n_seq
    nq = t // seq_q
    n_keys = past + t
    assert batch % n_seq == 0 and t % seq_q == 0 and seq_q % CHUNK == 0 and lp % kt == 0 and lp >= n_keys
    assert t == seq_q or (past == 0 and kt == tq)
    top = min(TOPK_MAX, n_keys // 4)

    def per_block(*dims):
        return pl.BlockSpec((1,) + dims, lambda bi, i: (bi,) + (0,) * len(dims))

    if n_seq > 1:
        nb = batch // n_seq
        kidx = kidx.reshape(nb, n_seq, lp, IDX_DIM).transpose(0, 2, 1, 3).reshape(nb, lp, n_seq * IDX_DIM)
        k = (k.reshape(nb, n_seq, lp, N_KV_HEADS, HEAD_DIM).transpose(0, 3, 2, 1, 4)
             .reshape(nb, N_KV_HEADS, lp, n_seq * HEAD_DIM))
        vt = (vt.reshape(nb, n_seq, N_KV_HEADS, HEAD_DIM, lp).transpose(0, 2, 1, 3, 4)
              .reshape(nb, N_KV_HEADS, n_seq * HEAD_DIM, lp))
        key_specs = [per_block(lp, n_seq * IDX_DIM), per_block(N_KV_HEADS, lp, n_seq * HEAD_DIM),
                     per_block(N_KV_HEADS, n_seq * HEAD_DIM, lp)]
    elif vt.ndim == 2:
        assert lp == t
        vt = vt.reshape(1, KV_COLS, batch * t)
        key_specs = [per_block(lp, IDX_DIM), per_block(lp, KV_COLS),
                     pl.BlockSpec((1, KV_COLS, lp), lambda bi, i: (0, 0, bi))]
    else:
        key_specs = [per_block(lp, IDX_DIM), per_block(lp, KV_COLS), per_block(KV_COLS, lp)]
    transpose_in_kernel = tq % LANES == 0
    if transpose_in_kernel:
        out_shape = jax.ShapeDtypeStruct((batch * t, Q_COLS), _MXU_DTYPE)
        out_spec = pl.BlockSpec((tq, Q_COLS), lambda bi, i: (bi * nq + i, 0))
        wit_spec = pl.BlockSpec((IDX_HEADS, tq), lambda bi, i: (0, bi * nq + i))
    else:
        assert n_seq == 1
        out_shape = jax.ShapeDtypeStruct((batch, Q_COLS, t), _MXU_DTYPE)
        out_spec = pl.BlockSpec((None, Q_COLS, tq), lambda bi, i: (bi, 0, i))
        wit = wit.reshape(IDX_HEADS, batch, t).transpose(1, 0, 2)
        wit_spec = pl.BlockSpec((None, IDX_HEADS, tq), lambda bi, i: (bi, 0, i))
    single_tile = kt == lp
    spare = SUBLANES if single_tile else kt
    out = pl.pallas_call(
        functools.partial(_attn_kernel, past=past, tq=tq, kt=kt, top=top, seq_q=seq_q),
        grid=(batch // n_seq, nq),
        in_specs=[
            pl.BlockSpec((tq, IDXQ_COLS), lambda bi, i: (bi * nq + i, 0)),
            wit_spec,
            pl.BlockSpec((tq, Q_COLS), lambda bi, i: (bi * nq + i, 0)),
        ] + key_specs,
        out_specs=out_spec,
        out_shape=out_shape,
        scratch_shapes=[
            pltpu.VMEM((lp, tq), jnp.float32),
            pltpu.VMEM((kt, tq), jnp.float32),
            pltpu.VMEM((spare, tq), jnp.float32),
            pltpu.VMEM((N_HEADS, kt, tq), jnp.float32),
            pltpu.VMEM((N_HEADS, spare, tq), jnp.float32),
            pltpu.VMEM((N_HEADS, HEAD_DIM, tq), jnp.float32),
        ],
        compiler_params=_params(("parallel", "arbitrary")),
        name=f"attn_t{t}_tq{tq}",
    )(qi, wit, q, kidx, k, vt)
    if transpose_in_kernel:
        return out
    return out.transpose(0, 2, 1).reshape(batch * t, Q_COLS)


def _merge_kernel(h_ref, c_ref, halo_ref, at_ref, ga_ref, gc_ref, cw_ref, cb_ref, lng_ref, lnb_ref,
                  wco_ref, wao_ref, wo_ref, o_ref, win_ref, shift_ref, *, seq_len):
    tm = c_ref.shape[0]
    n_chunk = tm // CHUNK
    lead = _HALO - (CONV_WIDTH - 1)
    dcs = []
    for j in range(n_chunk):
        if seq_len == CHUNK:
            history = halo_ref[j]
        elif j == 0:
            starts_sequence = (pl.program_id(0) * tm) % seq_len == 0
            history = jnp.where(starts_sequence, 0.0, halo_ref[...])
        else:
            history = c_ref[j * CHUNK - _HALO:j * CHUNK, :]
        win_ref[j, 0:_HALO, :] = history
        win_ref[j, _HALO:_HALO + CHUNK, :] = c_ref[j * CHUNK:(j + 1) * CHUNK, :]
        for s in range(1, SUBLANES):
            shift_ref[j, s - 1] = win_ref[j, s:s + shift_ref.shape[2], :]
        acc = jnp.zeros((CHUNK, CONV_CH), jnp.float32)
        for tap in range(CONV_WIDTH):
            s, base = (lead + tap) % SUBLANES, (lead + tap) // SUBLANES * SUBLANES
            rows = win_ref[j, base:base + CHUNK, :] if s == 0 else shift_ref[j, s - 1, base:base + CHUNK, :]
            acc = acc + rows * cw_ref[tap:tap + 1, :]
        dcs.append(acc)
    dc = jnp.concatenate(dcs, axis=0) + cb_ref[...]
    mu = jnp.mean(dc, axis=-1, keepdims=True)
    var = jnp.mean(jnp.square(dc - mu), axis=-1, keepdims=True)
    y = (dc - mu) * lax.rsqrt(var + EPS) * lng_ref[...] + lnb_ref[...]
    conv_out = _dot((y * jax.nn.sigmoid(y)).astype(wco_ref.dtype), wco_ref[...])
    attn_out = _dot(at_ref[...], wao_ref[...])
    merged = ga_ref[...] * attn_out + gc_ref[...] * conv_out
    o_ref[...] = h_ref[...] + _dot(merged.astype(wo_ref.dtype), wo_ref[...])


def _merge(h, c, state, attn, ga, gc, weights, tm, seq_len):
    n = h.shape[0]
    cw, conv_b, ln_g, ln_b, wco, wao, wo = weights

    def row(width):
        return pl.BlockSpec((tm, width), lambda i: (i, 0))

    if seq_len == CHUNK:
        halo, halo_spec = state, pl.BlockSpec((tm // CHUNK, _HALO, CONV_CH), lambda i: (i, 0, 0))
    else:
        assert state is None and seq_len % tm == 0
        per_tile = tm // _HALO
        halo, halo_spec = c, pl.BlockSpec((_HALO, CONV_CH), lambda i: (jnp.maximum(i * per_tile - 1, 0), 0))
    vec = _resident((1, CONV_CH))
    return pl.pallas_call(
        functools.partial(_merge_kernel, seq_len=seq_len),
        grid=(n // tm,),
        in_specs=[
            row(D_MODEL), row(CONV_CH), halo_spec,
            row(Q_COLS), row(D_MODEL), row(D_MODEL),
            _resident(cw.shape), vec, vec, vec,
            _resident((CONV_CH, D_MODEL)), _resident((Q_COLS, D_MODEL)), _resident((D_MODEL, D_MODEL)),
        ],
        out_specs=row(D_MODEL),
        out_shape=jax.ShapeDtypeStruct((n, D_MODEL), jnp.float32),
        scratch_shapes=[pltpu.VMEM((tm // CHUNK, _HALO + CHUNK, CONV_CH), jnp.float32),
                        pltpu.VMEM((tm // CHUNK, SUBLANES - 1, _HALO + CHUNK - SUBLANES, CONV_CH), jnp.float32)],
        compiler_params=_params(("parallel",)),
        name=f"merge_n{n}",
    )(h, c, halo, attn, ga, gc, cw, conv_b, ln_g, ln_b, wco, wao, wo)


def _row_tile(n, candidates):
    for tm in candidates:
        if n % tm == 0:
            return tm
    raise ValueError(f"row count {n} is not a multiple of {candidates[-1]}")


def _rope_tables(pos, rows):
    inv = ROPE_THETA ** (-jnp.arange(0, HEAD_DIM, 2, dtype=jnp.float32) / HEAD_DIM)
    ang = pos.astype(jnp.float32)[:, None] * inv[None, :]
    cos, sin = jnp.cos(ang), jnp.sin(ang)
    reps = rows // pos.shape[0]
    return (jnp.tile(cos, (reps, 4)), jnp.tile(jnp.concatenate([-sin, sin], axis=1), (reps, 2)))


def _main_weight(w_in):
    sizes = (Q_COLS, KV_COLS, KV_COLS, IDXQ_COLS, IDX_DIM, IDX_HEADS, 2 * CONV_CH, 2 * D_MODEL)
    offs = [0]
    for s in sizes:
        offs.append(offs[-1] + s)
    wq, wk, wv, wqi, wki, wwi, wconv, wgate = (w_in[:, offs[i]:offs[i + 1]] for i in range(len(sizes)))
    pad = jnp.zeros((D_MODEL, LANES - IDX_DIM - IDX_HEADS), w_in.dtype)
    return jnp.concatenate([wq, wqi, wk, wki, wwi, pad, wv, wconv, wgate], axis=1).astype(_MXU_DTYPE)


def _layer(x, caches, weights, final_norm):
    (ffn1_norm, ffn1_w, mix_norm, w_main, b_gate, merge_w, ffn2_norm, ffn2_w) = weights
    b, t, _ = x.shape
    n = b * t
    assert t % CHUNK == 0
    tm = _row_tile(n, (512, 256, 128, 64))
    tm_merge = _row_tile(n, (256, 128, 64))
    past = 0 if caches is None else caches[0].shape[1]

    h = _ffn(x.reshape(n, D_MODEL), ffn1_norm, ffn1_w, None, tm)
    if t >= tm:
        assert t % tm == 0
    else:
        assert tm % t == 0
    cos, sin = _rope_tables(past + jnp.arange(t, dtype=jnp.int32), max(t, tm))
    q, qi, k, ki, v, c, ga, gc, kb, kib, vt, wit = _proj(h, mix_norm, w_main, b_gate, cos, sin, tm)

    if caches is None:
        tq = _row_tile(t, (512, 256, 128, 64))
        attn = _attn(qi, wit, q, kib.reshape(b, t, IDX_DIM), kb.reshape(b, t, KV_COLS), vt,
                     seq_len=t, past=0, tq=tq, kt=tq)
        state = None
        conv_tail = c.reshape(b, t, CONV_CH)[:, t - (CONV_WIDTH - 1):]
    else:
        cache_k, cache_v, cache_idx_k, state_conv = caches
        n_keys = past + t
        pad_keys = -n_keys % LANES

        def with_cache(cache, new):
            cache = cache.reshape(b, past, -1).astype(_MXU_DTYPE)
            pad = jnp.zeros((b, pad_keys, cache.shape[2]), _MXU_DTYPE)
            return jnp.concatenate([cache, new.reshape(b, t, -1), pad], axis=1)

        vt_all = with_cache(cache_v, vt.reshape(KV_COLS, b, t).transpose(1, 2, 0)).transpose(0, 2, 1)
        side_by_side = next(g for g in (4, 2, 1) if b % g == 0)
        attn = _attn(qi, wit, q, with_cache(cache_idx_k, kib), with_cache(cache_k, kb), vt_all,
                     seq_len=t, past=past, tq=side_by_side * t, kt=n_keys + pad_keys)
        lead = jnp.zeros((b, _HALO - (CONV_WIDTH - 1), CONV_CH), jnp.float32)
        state = jnp.concatenate([lead, state_conv], axis=1)
        conv_tail = jnp.concatenate([state_conv, c.reshape(b, t, CONV_CH)], axis=1)[:, -(CONV_WIDTH - 1):]
    if t != CHUNK:
        assert state is None, "a carried conv state is supported for 64-row sequences only"

    h2 = _merge(h, c, state, attn, ga, gc, merge_w, tm_merge, t)
    y = _ffn(h2, ffn2_norm, ffn2_w, final_norm, tm)
    return (y.reshape(b, t, D_MODEL),
            k.reshape(1, b, t, N_KV_HEADS, HEAD_DIM), v.reshape(1, b, t, N_KV_HEADS, HEAD_DIM),
            ki.reshape(1, b, t, IDX_DIM), conv_tail[None])


def kernel(x_prompt, x_sample, cache_k, cache_v, cache_idx_k, state_conv, ffn1_norm, ffn1_w_in, ffn1_w_out, mix_norm, w_in, b_gate, conv_w, conv_b, conv_ln_g, conv_ln_b, conv_w_out, attn_w_out, w_out, ffn2_norm, ffn2_w_in, ffn2_w_out, final_norm):
    assert ffn1_norm.shape[0] == 1, "one layer"
    assert x_prompt.shape[1] >= CONV_WIDTH - 1
    conv_taps = jnp.zeros((_HALO, CONV_CH), jnp.float32).at[:CONV_WIDTH].set(conv_w[0])
    merge_w = (conv_taps, conv_b[0].reshape(1, CONV_CH), conv_ln_g[0].reshape(1, CONV_CH),
               conv_ln_b[0].reshape(1, CONV_CH), conv_w_out[0].astype(_MXU_DTYPE),
               attn_w_out[0].astype(_MXU_DTYPE), w_out[0].astype(_MXU_DTYPE))
    weights = (ffn1_norm[0], _ffn_weights(ffn1_w_in[0], ffn1_w_out[0]), mix_norm[0],
               _main_weight(w_in[0]), b_gate[0], merge_w,
               ffn2_norm[0], _ffn_weights(ffn2_w_in[0], ffn2_w_out[0]))
    y_p, k_p, v_p, ki_p, conv_p = _layer(x_prompt, None, weights, final_norm)
    y_s, k_s, v_s, ki_s, conv_s = _layer(
        x_sample, (cache_k[0], cache_v[0], cache_idx_k[0], state_conv[0]), weights, final_norm)
    return (y_p, y_s, k_p, v_p, ki_p, conv_p, k_s, v_s, ki_s, conv_s)
```

```python
import functools
import math

import jax
import jax.numpy as jnp
from jax import lax
from jax.experimental import pallas as pl
from jax.experimental.pallas import tpu as pltpu

D_MODEL = 1024
CHUNK = 64
N_HEADS = 8
N_KV_HEADS = 2
HEAD_DIM = 64
GROUP = N_HEADS // N_KV_HEADS
IDX_HEADS = 8
IDX_DIM = 64
TOPK_MAX = 256
CONV_CH = 512
CONV_WIDTH = 31
D_FF = 2816
ROPE_THETA = 10000.0
EPS = 1e-6

Q_COLS = N_HEADS * HEAD_DIM
KV_COLS = N_KV_HEADS * HEAD_DIM
IDXQ_COLS = IDX_HEADS * IDX_DIM

LANES = 128
SUBLANES = 8
_MXU_DTYPE = jnp.bfloat16
_VMEM_LIMIT = 56 * 1024 * 1024
_FF_CHUNK = 256
_HALO = 32
_NEG = -1e30
_Q_SCALE = HEAD_DIM ** -0.5 * math.log2(math.e)

_C_Q = 0
_C_QI = _C_Q + Q_COLS
_C_K = _C_QI + IDXQ_COLS
_C_KW = _C_K + KV_COLS
_C_V = _C_KW + LANES
_C_CA = _C_V + KV_COLS
_C_CB = _C_CA + CONV_CH
_C_GA = _C_CB + CONV_CH
_C_GC = _C_GA + D_MODEL
_C_END = _C_GC + D_MODEL

_KEY_NEG_FLT_MAX = -2139095040
_KEY_POS_INF = 0x7F800000
_KEY_MIN_NORMAL = 0x00800000
_MAX_SEARCH_PASSES = 96
_PROBE_MARGIN = 0.1


def _params(sem):
    return pltpu.CompilerParams(dimension_semantics=sem, vmem_limit_bytes=_VMEM_LIMIT)


def _resident(shape):
    nd = len(shape)
    return pl.BlockSpec(shape, lambda *_: (0,) * nd, pipeline_mode=pl.Buffered(1))


def _rms(x, g):
    return x * lax.rsqrt(jnp.mean(x * x, axis=-1, keepdims=True) + EPS) * g


def _dot(a, b):
    return jnp.dot(a, b, preferred_element_type=jnp.float32)


def _dot_nt(a, b):
    return lax.dot_general(a, b, (((1,), (1,)), ((), ())), preferred_element_type=jnp.float32)


def _ffn_kernel(*refs, final_norm):
    if final_norm:
        x_ref, g_ref, wa_ref, wb_ref, wo_ref, gf_ref, o_ref = refs
    else:
        x_ref, g_ref, wa_ref, wb_ref, wo_ref, o_ref = refs
    x = x_ref[...]
    u = _rms(x, g_ref[...]).astype(wa_ref.dtype)
    acc = jnp.zeros_like(x)
    for c in range(D_FF // _FF_CHUNK):
        sl = slice(c * _FF_CHUNK, (c + 1) * _FF_CHUNK)
        a = _dot(u, wa_ref[:, sl])
        b = _dot(u, wb_ref[:, sl])
        act = (a * jax.nn.sigmoid(a) * b).astype(wo_ref.dtype)
        acc = acc + _dot(act, wo_ref[sl, :])
    h = x + 0.5 * acc
    if final_norm:
        h = _rms(h, gf_ref[...])
    o_ref[...] = h


def _ffn_weights(w_in, w_out):
    return w_in[:, :D_FF].astype(_MXU_DTYPE), w_in[:, D_FF:].astype(_MXU_DTYPE), w_out.astype(_MXU_DTYPE)


def _ffn(x, g, weights, g_final, tm):
    n = x.shape[0]
    wa, wb, wo = weights
    row = pl.BlockSpec((tm, D_MODEL), lambda i: (i, 0))
    vec = _resident((1, D_MODEL))
    in_specs = [row, vec, _resident(wa.shape), _resident(wb.shape), _resident(wo.shape)]
    args = [x, g.reshape(1, D_MODEL), wa, wb, wo]
    if g_final is not None:
        in_specs.append(vec)
        args.append(g_final.reshape(1, D_MODEL))
    return pl.pallas_call(
        functools.partial(_ffn_kernel, final_norm=g_final is not None),
        grid=(n // tm,),
        in_specs=in_specs,
        out_specs=row,
        out_shape=jax.ShapeDtypeStruct((n, D_MODEL), jnp.float32),
        compiler_params=_params(("parallel",)),
        name=f"ffn_final_n{n}" if g_final is not None else f"ffn_n{n}",
    )(*args)


def _swap_halves(x):
    lane = lax.broadcasted_iota(jnp.int32, x.shape, 1)
    first = (lane & (HEAD_DIM - 1)) < HEAD_DIM // 2
    return jnp.where(first, pltpu.roll(x, LANES - HEAD_DIM // 2, 1), pltpu.roll(x, HEAD_DIM // 2, 1))


def _rope(x, cos, sin_signed):
    tiles = []
    for j in range(x.shape[1] // LANES):
        xt = x[:, j * LANES:(j + 1) * LANES]
        tiles.append(xt * cos + _swap_halves(xt) * sin_signed)
    return tiles[0] if len(tiles) == 1 else jnp.concatenate(tiles, axis=1)


def _proj_kernel(h_ref, g_ref, w_ref, bg_ref, cos_ref, sin_ref,
                 q_ref, qi_ref, k_ref, ki_ref, v_ref, c_ref, ga_ref, gc_ref,
                 kb_ref, kib_ref, vt_ref, wit_ref):
    u = _rms(h_ref[...], g_ref[...]).astype(w_ref.dtype)
    cos = cos_ref[...]
    sin = sin_ref[...]

    def proj(lo, hi):
        return _dot(u, w_ref[:, lo:hi])

    q_ref[...] = (_rope(proj(_C_Q, _C_QI), cos, sin) * _Q_SCALE).astype(q_ref.dtype)
    qi_ref[...] = _rope(proj(_C_QI, _C_K), cos, sin).astype(qi_ref.dtype)
    k = _rope(proj(_C_K, _C_KW), cos, sin)
    for kv in range(N_KV_HEADS):
        k_ref[:, kv, :] = k[:, kv * HEAD_DIM:(kv + 1) * HEAD_DIM]
    kb_ref[...] = k.astype(kb_ref.dtype)
    zkw = proj(_C_KW, _C_V)
    ki = _rope(zkw, cos, sin)[:, :IDX_DIM]
    ki_ref[...] = ki
    kib_ref[...] = ki.astype(kib_ref.dtype)
    wit_ref[...] = (zkw * (IDX_HEADS ** -0.5)).T[IDX_DIM:IDX_DIM + IDX_HEADS, :]
    v = proj(_C_V, _C_CA)
    for kv in range(N_KV_HEADS):
        v_ref[:, kv, :] = v[:, kv * HEAD_DIM:(kv + 1) * HEAD_DIM]
    vt_ref[...] = v.T.astype(vt_ref.dtype)
    c_ref[...] = proj(_C_CA, _C_CB) * jax.nn.sigmoid(proj(_C_CB, _C_GA))
    ga_ref[...] = jax.nn.sigmoid(proj(_C_GA, _C_GC) + bg_ref[:, :D_MODEL])
    gc_ref[...] = jax.nn.sigmoid(proj(_C_GC, _C_END) + bg_ref[:, D_MODEL:])


def _proj(h, g, w_main, b_gate, cos, sin, tm):
    n = h.shape[0]
    n_tab = cos.shape[0] // tm

    def row(width):
        return pl.BlockSpec((tm, width), lambda i: (i, 0))

    def col(height):
        return pl.BlockSpec((height, tm), lambda i: (0, i))

    table = pl.BlockSpec((tm, LANES), lambda i: (i % n_tab, 0))
    f32 = jnp.float32
    out_shape = (
        jax.ShapeDtypeStruct((n, Q_COLS), _MXU_DTYPE),
        jax.ShapeDtypeStruct((n, IDXQ_COLS), _MXU_DTYPE),
        jax.ShapeDtypeStruct((n, N_KV_HEADS, HEAD_DIM), f32),
        jax.ShapeDtypeStruct((n, IDX_DIM), f32),
        jax.ShapeDtypeStruct((n, N_KV_HEADS, HEAD_DIM), f32),
        jax.ShapeDtypeStruct((n, CONV_CH), f32),
        jax.ShapeDtypeStruct((n, D_MODEL), f32),
        jax.ShapeDtypeStruct((n, D_MODEL), f32),
        jax.ShapeDtypeStruct((n, KV_COLS), _MXU_DTYPE),
        jax.ShapeDtypeStruct((n, IDX_DIM), _MXU_DTYPE),
        jax.ShapeDtypeStruct((KV_COLS, n), _MXU_DTYPE),
        jax.ShapeDtypeStruct((IDX_HEADS, n), f32),
    )
    heads = pl.BlockSpec((tm, N_KV_HEADS, HEAD_DIM), lambda i: (i, 0, 0))
    out_specs = tuple(heads if len(s.shape) == 3 else row(s.shape[1]) for s in out_shape[:10]) + (col(KV_COLS), col(IDX_HEADS))
    return pl.pallas_call(
        _proj_kernel,
        grid=(n // tm,),
        in_specs=[row(D_MODEL), _resident((1, D_MODEL)), _resident(w_main.shape),
                  _resident((1, 2 * D_MODEL)), table, table],
        out_specs=out_specs,
        out_shape=out_shape,
        compiler_params=_params(("parallel",)),
        name=f"proj_n{n}",
    )(h, g.reshape(1, D_MODEL), w_main, b_gate.reshape(1, 2 * D_MODEL), cos, sin)


def _float_to_key(x):
    bits = lax.bitcast_convert_type(x, jnp.int32)
    return jnp.where(bits >= 0, bits, bits ^ 0x7FFFFFFF)


def _key_to_float(key):
    return lax.bitcast_convert_type(jnp.where(key >= 0, key, key ^ 0x7FFFFFFF), jnp.float32)


def _normal_quantile(p):
    tail = jnp.minimum(p, 1.0 - p)
    t = jnp.sqrt(-2.0 * jnp.log(tail))
    z = t - (2.515517 + t * (0.802853 + t * 0.010328)) / (1.0 + t * (1.432788 + t * (0.189269 + t * 0.001308)))
    return jnp.where(p < 0.5, z, -z)


def _kth_largest(s_ref, count_ge, n_valid, top, kt):
    cols = s_ref.shape[1]
    shape = (1, cols)
    head = s_ref[0:min(kt, 256), :]
    mean = head.mean(axis=0, keepdims=True)
    std = jnp.sqrt(jnp.maximum((head * head).mean(axis=0, keepdims=True) - mean * mean, 1e-30))
    guess = mean + std * _normal_quantile(jnp.minimum((top - 0.5) / jnp.maximum(n_valid, 1.0), 0.999))
    log_top = jnp.log(top - 0.5)

    def adjacent(lo_k, hi_k):
        return (hi_k <= lo_k + 1) | ((lo_k == 0) & (hi_k == _KEY_MIN_NORMAL))

    def active(state):
        _, lo_k, hi_k, clo = state[:4]
        return (clo > top) & jnp.logical_not(adjacent(lo_k, hi_k))

    def cond(state):
        return (state[0] < _MAX_SEARCH_PASSES) & (jnp.max(jnp.where(active(state), 1.0, 0.0)) > 0.0)

    def body(state):
        it, lo_k, hi_k, clo, chi, lo_real, hi_real, run, last_up = state
        lo_v, hi_v = _key_to_float(lo_k), _key_to_float(hi_k)
        both = (lo_real > 0) & (hi_real > 0)
        log_lo = jnp.log(clo)
        frac = (log_lo - log_top) / (log_lo - jnp.log(jnp.maximum(chi, 0.5)))
        frac = jnp.where(clo - chi > 16.0, jnp.clip(frac, _PROBE_MARGIN, 1.0 - _PROBE_MARGIN), 0.5)
        step = std * 0.25 * lax.shift_left(jnp.int32(1), jnp.minimum(run, 20)).astype(jnp.float32)
        t = jnp.where(both, lo_v + (hi_v - lo_v) * frac, jnp.where(lo_real > 0, lo_v + step, hi_v - step))
        t = jnp.where(it == 0, guess, t)
        t_k = _float_to_key(t)
        mid_k = (lo_k >> 1) + (hi_k >> 1) + (lo_k & hi_k & 1)
        bisect = (both & (run >= 2)) | (it >= 24)
        t_k = jnp.where(bisect, mid_k, t_k)
        t_k = jnp.where((lo_k < 0) & (hi_k > _KEY_MIN_NORMAL) & (it >= 3), 0, t_k)
        t_k = jnp.where((lo_k == 0) & (hi_k > _KEY_MIN_NORMAL), _KEY_MIN_NORMAL, t_k)
        t_k = jnp.minimum(jnp.maximum(t_k, lo_k + 1), hi_k - 1)
        c = count_ge(_key_to_float(t_k))
        live = active(state)
        up = c >= top
        go_lo, go_hi = live & up, live & jnp.logical_not(up)
        same = jnp.where(up, 1, -1) == last_up
        return (it + 1,
                jnp.where(go_lo, t_k, lo_k), jnp.where(go_hi, t_k, hi_k),
                jnp.where(go_lo, c, clo), jnp.where(go_hi, c, chi),
                jnp.where(go_lo, 1, lo_real), jnp.where(go_hi, 1, hi_real),
                jnp.where(bisect, 0, jnp.where(same, run + 1, 1)),
                jnp.where(up, 1, -1))

    zeros = jnp.zeros(shape, jnp.int32)
    init = (jnp.int32(0), jnp.full(shape, _KEY_NEG_FLT_MAX, jnp.int32), jnp.full(shape, _KEY_POS_INF, jnp.int32),
            n_valid, jnp.zeros(shape, jnp.float32), zeros, zeros, zeros, zeros)
    out = lax.while_loop(cond, body, init)
    return _key_to_float(out[1]), out[3]


def _count_ge(ref, x, n_pairs, rows):
    cols = ref.shape[1]

    def tile(j, acc):
        blk = ref[pl.ds(pl.multiple_of(j * rows, rows), rows), :]
        ones = jnp.where(blk >= x, 1.0, 0.0)
        return acc + ones.reshape(rows // 8, 8, cols).sum(axis=0)

    def pair(jj, acc):
        return tile(2 * jj + 1, tile(2 * jj, acc))

    acc = lax.fori_loop(0, n_pairs, pair, jnp.zeros((8, cols), jnp.float32))
    return acc.sum(axis=0, keepdims=True)


def _attn_kernel(qi_ref, wit_ref, q_ref, kidx_ref, k_ref, vt_ref, o_ref,
                 s_ref, bias0_ref, bias1_ref, lg0_ref, lg1_ref, acc_ref, stat_ref, qh_ref, *, past, tq, kt, top, seq_q):
    n_seq = tq // seq_q
    t0 = pl.program_id(1) * seq_q
    n_kt = (past + t0 + seq_q + kt - 1) // kt
    lane_q = lax.broadcasted_iota(jnp.int32, (1, tq), 1)
    limit = past + ((t0 + lane_q % seq_q) // CHUNK + 1) * CHUNK
    w = wit_ref[...]

    own = (lax.broadcasted_iota(jnp.int32, (tq, n_seq * HEAD_DIM), 0) // seq_q
           == lax.broadcasted_iota(jnp.int32, (tq, n_seq * HEAD_DIM), 1) // HEAD_DIM)

    def keys_times_queries(keys, queries_ref, cols):
        qh = queries_ref[:, cols]
        if n_seq > 1:
            qh = jnp.where(own, jnp.concatenate([qh] * n_seq, axis=1), jnp.zeros((), qh.dtype))
        return _dot_nt(keys, qh)

    def score_rows(off, rows, last):
        acc = jnp.zeros((rows, tq), jnp.float32)
        for h in range(IDX_HEADS):
            s = keys_times_queries(kidx_ref[0, pl.ds(off, rows), :], qi_ref, slice(h * IDX_DIM, (h + 1) * IDX_DIM))
            acc = acc + w[h:h + 1, :] * jnp.maximum(s, 0.0)
        if last:
            key = off + lax.broadcasted_iota(jnp.int32, (rows, tq), 0)
            acc = jnp.where(key < limit, acc, -jnp.inf)
        s_ref[pl.ds(off, rows), :] = acc

    def score_pair(jj, carry):
        score_rows(pl.multiple_of(jj * 2 * kt, 2 * kt), 2 * kt, False)
        return carry

    n_full = n_kt - 1
    lax.fori_loop(0, n_full // 2, score_pair, 0)

    @pl.when(n_full % 2 == 1)
    def _():
        score_rows(pl.multiple_of((n_full - 1) * kt, kt), kt, False)

    score_rows(pl.multiple_of(n_full * kt, kt), kt, True)

    def count_ge(thr):
        return _count_ge(s_ref, thr, n_kt, kt // 2)

    thr, cnt = _kth_largest(s_ref, count_ge, limit.astype(jnp.float32), top, kt)

    def resolve_ties(cnt):
        excess = cnt - top
        row = lax.broadcasted_iota(jnp.int32, (kt, kt), 0)
        col = lax.broadcasted_iota(jnp.int32, (kt, kt), 1)
        at_or_after = jnp.where(col >= row, 1.0, 0.0).astype(_MXU_DTYPE)

        def tile(jr, after):
            rows = pl.ds(pl.multiple_of((n_kt - 1 - jr) * kt, kt), kt)
            blk = s_ref[rows, :]
            equal = blk == thr
            suffix = after + _dot(at_or_after, jnp.where(equal, 1.0, 0.0).astype(_MXU_DTYPE))
            s_ref[rows, :] = jnp.where(equal, jnp.where(suffix <= excess, -jnp.inf, blk), blk)
            return suffix[0:1, :]

        lax.fori_loop(0, n_kt, tile, jnp.zeros((1, tq), jnp.float32))
        return count_ge(thr)

    cnt = lax.cond(jnp.max(cnt) > top, resolve_ties, lambda c: c, cnt)

    def drop_lowest(cnt):
        over = cnt > top

        def tile(j):
            off = pl.multiple_of(j * kt, kt)
            return off, s_ref[pl.ds(off, kt), :]

        def min_body(j, vmin):
            _, blk = tile(j)
            cand = jnp.where(blk >= thr, blk, jnp.inf)
            return jnp.minimum(vmin, cand.reshape(kt // 8, 8, tq).min(axis=0))

        vmin = lax.fori_loop(0, n_kt, min_body, jnp.full((8, tq), jnp.inf, jnp.float32))
        vmin = vmin.min(axis=0, keepdims=True)

        def idx_body(j, imax):
            off, blk = tile(j)
            key_idx = off + lax.broadcasted_iota(jnp.int32, (kt, tq), 0)
            cand = jnp.where(blk == vmin, key_idx, -1)
            return jnp.maximum(imax, cand.reshape(kt // 8, 8, tq).max(axis=0))

        imax = lax.fori_loop(0, n_kt, idx_body, jnp.full((8, tq), -1, jnp.int32))
        kill = jnp.where(over, imax.max(axis=0, keepdims=True), -1)

        def kill_body(j, carry):
            off, blk = tile(j)
            key_idx = off + lax.broadcasted_iota(jnp.int32, (kt, tq), 0)
            s_ref[pl.ds(off, kt), :] = jnp.where(key_idx == kill, -jnp.inf, blk)
            return carry

        lax.fori_loop(0, n_kt, kill_body, 0)
        return jnp.where(over, cnt - 1.0, cnt)

    lax.while_loop(lambda cnt: jnp.max(cnt) > top, drop_lowest, cnt)

    acc_ref[...] = jnp.zeros(acc_ref.shape, jnp.float32)

    def logits_sweep(j, m, bias_buf, lg_buf):
        off = pl.multiple_of(jnp.minimum(j, n_kt - 1) * kt, kt)
        bias_buf[...] = jnp.where(s_ref[pl.ds(off, kt), :] >= jnp.where(j < n_kt, thr, jnp.inf), 0.0, _NEG)
        m_new = []
        for h in range(N_HEADS):
            c = h // GROUP
            kx = (k_ref[0, c, pl.ds(off, kt), :] if n_seq > 1
                  else k_ref[0, pl.ds(off, kt), c * HEAD_DIM:(c + 1) * HEAD_DIM])
            lg = keys_times_queries(kx, q_ref, slice(h * HEAD_DIM, (h + 1) * HEAD_DIM)) + bias_buf[...]
            lg_buf[h] = lg
            m_new.append(jnp.maximum(m[h], lg.max(axis=0, keepdims=True)))
        return tuple(m_new)

    def softmax_sweep(j, m_old, m_new, l, lg_buf):
        off = pl.multiple_of(jnp.minimum(j, n_kt - 1) * kt, kt)
        l_new = []
        for h in range(N_HEADS):
            c = h // GROUP
            alpha = jnp.exp2(m_old[h] - m_new[h])
            p = jnp.exp2(lg_buf[h] - m_new[h])
            if n_seq == 1:
                pv = _dot(vt_ref[0, c * HEAD_DIM:(c + 1) * HEAD_DIM, pl.ds(off, kt)], p.astype(vt_ref.dtype))
            else:
                full = _dot(vt_ref[0, c, :, pl.ds(off, kt)], p.astype(vt_ref.dtype))
                lane_seq = lax.broadcasted_iota(jnp.int32, (HEAD_DIM, tq), 1) // seq_q
                pv = sum(jnp.where(lane_seq == g, full[g * HEAD_DIM:(g + 1) * HEAD_DIM, :], 0.0)
                         for g in range(n_seq))
            acc_ref[h] = alpha * acc_ref[h] + pv
            l_new.append(alpha * l[h] + p.sum(axis=0, keepdims=True))
        return tuple(l_new)

    def fused_sweep(j_next, j_cur, bias_buf, lg_next, lg_cur, old, cur, nxt):
        off_n = pl.multiple_of(jnp.minimum(j_next, n_kt - 1) * kt, kt)
        off_c = pl.multiple_of(jnp.minimum(j_cur, n_kt - 1) * kt, kt)
        bias_buf[...] = jnp.where(s_ref[pl.ds(off_n, kt), :] >= jnp.where(j_next < n_kt, thr, jnp.inf), 0.0, _NEG)

        def step(g, carry):
            for c in range(N_KV_HEADS):
                h = c * GROUP + g
                lg = _dot_nt(k_ref[0, pl.ds(off_n, kt), c * HEAD_DIM:(c + 1) * HEAD_DIM], qh_ref[h]) + bias_buf[...]
                lg_next[h] = lg
                m_n = jnp.maximum(stat_ref[cur, h][0:1, :], lg.max(axis=0, keepdims=True))
                stat_ref[nxt, h] = jnp.broadcast_to(m_n, (SUBLANES, tq))
                m_c = stat_ref[cur, h][0:1, :]
                alpha = jnp.exp2(stat_ref[old, h][0:1, :] - m_c)
                p = jnp.exp2(lg_cur[h] - m_c)
                pv = _dot(vt_ref[0, c * HEAD_DIM:(c + 1) * HEAD_DIM, pl.ds(off_c, kt)], p.astype(vt_ref.dtype))
                acc_ref[h] = alpha * acc_ref[h] + pv
                l_h = alpha * stat_ref[3, h][0:1, :] + p.sum(axis=0, keepdims=True)
                stat_ref[3, h] = jnp.broadcast_to(l_h, (SUBLANES, tq))
            return carry

        lax.fori_loop(0, GROUP, step, 0)

    def attn_step(jj, carry):
        fused_sweep(2 * jj + 1, 2 * jj, bias1_ref, lg1_ref, lg0_ref, 0, 1, 2)
        fused_sweep(2 * jj + 2, 2 * jj + 1, bias0_ref, lg0_ref, lg1_ref, 1, 2, 0)
        for h in range(N_HEADS):
            stat_ref[1, h] = stat_ref[0, h]
            stat_ref[0, h] = stat_ref[2, h]
        return carry

    m_init = tuple(jnp.full((1, tq), _NEG, jnp.float32) for _ in range(N_HEADS))
    l_init = tuple(jnp.zeros((1, tq), jnp.float32) for _ in range(N_HEADS))
    m_first = logits_sweep(0, m_init, bias0_ref, lg0_ref)
    if kt == s_ref.shape[0]:
        l = softmax_sweep(0, m_init, m_first, l_init, lg0_ref)
    else:
        for h in range(N_HEADS):
            qh_ref[h] = q_ref[:, h * HEAD_DIM:(h + 1) * HEAD_DIM]
            stat_ref[0, h] = jnp.broadcast_to(m_init[h], (SUBLANES, tq))
            stat_ref[1, h] = jnp.broadcast_to(m_first[h], (SUBLANES, tq))
            stat_ref[3, h] = jnp.zeros((SUBLANES, tq), jnp.float32)
        lax.fori_loop(0, (n_kt + 1) // 2, attn_step, 0)
        l = tuple(stat_ref[3, h][0:1, :] for h in range(N_HEADS))
    if o_ref.shape[0] == tq:
        o = jnp.concatenate([acc_ref[h] / l[h] for h in range(N_HEADS)], axis=0)
        o_ref[...] = o.T.astype(o_ref.dtype)
    else:
        for h in range(N_HEADS):
            o_ref[h * HEAD_DIM:(h + 1) * HEAD_DIM, :] = (acc_ref[h] / l[h]).astype(o_ref.dtype)


def _attn(qi, wit, q, kidx, k, vt, *, seq_len, past, tq, kt):
    batch, lp = k.shape[0], k.shape[1]
    t = seq_len
    n_seq = max(1, tq // t)
    seq_q = tq // n_seq
    nq = t // seq_q
    n_keys = past + t
    assert batch % n_seq == 0 and t % seq_q == 0 and seq_q % CHUNK == 0 and lp % kt == 0 and lp >= n_keys
    assert t == seq_q or (past == 0 and kt == tq)
    top = min(TOPK_MAX, n_keys // 4)

    def per_block(*dims):
        return pl.BlockSpec((1,) + dims, lambda bi, i: (bi,) + (0,) * len(dims))

    if n_seq > 1:
        nb = batch // n_seq
        kidx = kidx.reshape(nb, n_seq, lp, IDX_DIM).transpose(0, 2, 1, 3).reshape(nb, lp, n_seq * IDX_DIM)
        k = (k.reshape(nb, n_seq, lp, N_KV_HEADS, HEAD_DIM).transpose(0, 3, 2, 1, 4)
             .reshape(nb, N_KV_HEADS, lp, n_seq * HEAD_DIM))
        vt = (vt.reshape(nb, n_seq, N_KV_HEADS, HEAD_DIM, lp).transpose(0, 2, 1, 3, 4)
              .reshape(nb, N_KV_HEADS, n_seq * HEAD_DIM, lp))
        key_specs = [per_block(lp, n_seq * IDX_DIM), per_block(N_KV_HEADS, lp, n_seq * HEAD_DIM),
                     per_block(N_KV_HEADS, n_seq * HEAD_DIM, lp)]
    elif vt.ndim == 2:
        assert lp == t
        vt = vt.reshape(1, KV_COLS, batch * t)
        key_specs = [per_block(lp, IDX_DIM), per_block(lp, KV_COLS),
                     pl.BlockSpec((1, KV_COLS, lp), lambda bi, i: (0, 0, bi))]
    else:
        key_specs = [per_block(lp, IDX_DIM), per_block(lp, KV_COLS), per_block(KV_COLS, lp)]
    transpose_in_kernel = tq % LANES == 0
    if transpose_in_kernel:
        out_shape = jax.ShapeDtypeStruct((batch * t, Q_COLS), _MXU_DTYPE)
        out_spec = pl.BlockSpec((tq, Q_COLS), lambda bi, i: (bi * nq + i, 0))
        wit_spec = pl.BlockSpec((IDX_HEADS, tq), lambda bi, i: (0, bi * nq + i))
    else:
        assert n_seq == 1
        out_shape = jax.ShapeDtypeStruct((batch, Q_COLS, t), _MXU_DTYPE)
        out_spec = pl.BlockSpec((None, Q_COLS, tq), lambda bi, i: (bi, 0, i))
        wit = wit.reshape(IDX_HEADS, batch, t).transpose(1, 0, 2)
        wit_spec = pl.BlockSpec((None, IDX_HEADS, tq), lambda bi, i: (bi, 0, i))
    single_tile = kt == lp
    spare = SUBLANES if single_tile else kt
    out = pl.pallas_call(
        functools.partial(_attn_kernel, past=past, tq=tq, kt=kt, top=top, seq_q=seq_q),
        grid=(batch // n_seq, nq),
        in_specs=[
            pl.BlockSpec((tq, IDXQ_COLS), lambda bi, i: (bi * nq + i, 0)),
            wit_spec,
            pl.BlockSpec((tq, Q_COLS), lambda bi, i: (bi * nq + i, 0)),
        ] + key_specs,
        out_specs=out_spec,
        out_shape=out_shape,
        scratch_shapes=[
            pltpu.VMEM((lp, tq), jnp.float32),
            pltpu.VMEM((kt, tq), jnp.float32),
            pltpu.VMEM((spare, tq), jnp.float32),
            pltpu.VMEM((N_HEADS, kt, tq), jnp.float32),
            pltpu.VMEM((N_HEADS, spare, tq), jnp.float32),
            pltpu.VMEM((N_HEADS, HEAD_DIM, tq), jnp.float32),
            pltpu.VMEM((4, N_HEADS, SUBLANES, tq), jnp.float32),
            pltpu.VMEM((N_HEADS, tq, HEAD_DIM), q.dtype),
        ],
        compiler_params=_params(("parallel", "arbitrary")),
        name=f"attn_t{t}_tq{tq}",
    )(qi, wit, q, kidx, k, vt)
    if transpose_in_kernel:
        return out
    return out.transpose(0, 2, 1).reshape(batch * t, Q_COLS)


def _merge_kernel(h_ref, c_ref, halo_ref, at_ref, ga_ref, gc_ref, cw_ref, cb_ref, lng_ref, lnb_ref,
                  wco_ref, wao_ref, wo_ref, o_ref, win_ref, shift_ref, *, seq_len):
    tm = c_ref.shape[0]
    n_chunk = tm // CHUNK
    lead = _HALO - (CONV_WIDTH - 1)
    dcs = []
    for j in range(n_chunk):
        if seq_len == CHUNK:
            history = halo_ref[j]
        elif j == 0:
            starts_sequence = (pl.program_id(0) * tm) % seq_len == 0
            history = jnp.where(starts_sequence, 0.0, halo_ref[...])
        else:
            history = c_ref[j * CHUNK - _HALO:j * CHUNK, :]
        win_ref[j, 0:_HALO, :] = history
        win_ref[j, _HALO:_HALO + CHUNK, :] = c_ref[j * CHUNK:(j + 1) * CHUNK, :]
        for s in range(1, SUBLANES):
            shift_ref[j, s - 1] = win_ref[j, s:s + shift_ref.shape[2], :]
        acc = jnp.zeros((CHUNK, CONV_CH), jnp.float32)
        for tap in range(CONV_WIDTH):
            s, base = (lead + tap) % SUBLANES, (lead + tap) // SUBLANES * SUBLANES
            rows = win_ref[j, base:base + CHUNK, :] if s == 0 else shift_ref[j, s - 1, base:base + CHUNK, :]
            acc = acc + rows * cw_ref[tap:tap + 1, :]
        dcs.append(acc)
    dc = jnp.concatenate(dcs, axis=0) + cb_ref[...]
    mu = jnp.mean(dc, axis=-1, keepdims=True)
    var = jnp.mean(jnp.square(dc - mu), axis=-1, keepdims=True)
    y = (dc - mu) * lax.rsqrt(var + EPS) * lng_ref[...] + lnb_ref[...]
    conv_out = _dot((y * jax.nn.sigmoid(y)).astype(wco_ref.dtype), wco_ref[...])
    attn_out = _dot(at_ref[...], wao_ref[...])
    merged = ga_ref[...] * attn_out + gc_ref[...] * conv_out
    o_ref[...] = h_ref[...] + _dot(merged.astype(wo_ref.dtype), wo_ref[...])


def _merge(h, c, state, attn, ga, gc, weights, tm, seq_len):
    n = h.shape[0]
    cw, conv_b, ln_g, ln_b, wco, wao, wo = weights

    def row(width):
        return pl.BlockSpec((tm, width), lambda i: (i, 0))

    if seq_len == CHUNK:
        halo, halo_spec = state, pl.BlockSpec((tm // CHUNK, _HALO, CONV_CH), lambda i: (i, 0, 0))
    else:
        assert state is None and seq_len % tm == 0
        per_tile = tm // _HALO
        halo, halo_spec = c, pl.BlockSpec((_HALO, CONV_CH), lambda i: (jnp.maximum(i * per_tile - 1, 0), 0))
    vec = _resident((1, CONV_CH))
    return pl.pallas_call(
        functools.partial(_merge_kernel, seq_len=seq_len),
        grid=(n // tm,),
        in_specs=[
            row(D_MODEL), row(CONV_CH), halo_spec,
            row(Q_COLS), row(D_MODEL), row(D_MODEL),
            _resident(cw.shape), vec, vec, vec,
            _resident((CONV_CH, D_MODEL)), _resident((Q_COLS, D_MODEL)), _resident((D_MODEL, D_MODEL)),
        ],
        out_specs=row(D_MODEL),
        out_shape=jax.ShapeDtypeStruct((n, D_MODEL), jnp.float32),
        scratch_shapes=[pltpu.VMEM((tm // CHUNK, _HALO + CHUNK, CONV_CH), jnp.float32),
                        pltpu.VMEM((tm // CHUNK, SUBLANES - 1, _HALO + CHUNK - SUBLANES, CONV_CH), jnp.float32)],
        compiler_params=_params(("parallel",)),
        name=f"merge_n{n}",
    )(h, c, halo, attn, ga, gc, cw, conv_b, ln_g, ln_b, wco, wao, wo)


def _row_tile(n, candidates):
    for tm in candidates:
        if n % tm == 0:
            return tm
    raise ValueError(f"row count {n} is not a multiple of {candidates[-1]}")


def _rope_tables(pos, rows):
    inv = ROPE_THETA ** (-jnp.arange(0, HEAD_DIM, 2, dtype=jnp.float32) / HEAD_DIM)
    ang = pos.astype(jnp.float32)[:, None] * inv[None, :]
    cos, sin = jnp.cos(ang), jnp.sin(ang)
    reps = rows // pos.shape[0]
    return (jnp.tile(cos, (reps, 4)), jnp.tile(jnp.concatenate([-sin, sin], axis=1), (reps, 2)))


def _main_weight(w_in):
    sizes = (Q_COLS, KV_COLS, KV_COLS, IDXQ_COLS, IDX_DIM, IDX_HEADS, 2 * CONV_CH, 2 * D_MODEL)
    offs = [0]
    for s in sizes:
        offs.append(offs[-1] + s)
    wq, wk, wv, wqi, wki, wwi, wconv, wgate = (w_in[:, offs[i]:offs[i + 1]] for i in range(len(sizes)))
    pad = jnp.zeros((D_MODEL, LANES - IDX_DIM - IDX_HEADS), w_in.dtype)
    return jnp.concatenate([wq, wqi, wk, wki, wwi, pad, wv, wconv, wgate], axis=1).astype(_MXU_DTYPE)


def _layer(x, caches, weights, final_norm):
    (ffn1_norm, ffn1_w, mix_norm, w_main, b_gate, merge_w, ffn2_norm, ffn2_w) = weights
    b, t, _ = x.shape
    n = b * t
    assert t % CHUNK == 0
    tm = _row_tile(n, (512, 256, 128, 64))
    tm_merge = _row_tile(n, (256, 128, 64))
    past = 0 if caches is None else caches[0].shape[1]

    h = _ffn(x.reshape(n, D_MODEL), ffn1_norm, ffn1_w, None, tm)
    if t >= tm:
        assert t % tm == 0
    else:
        assert tm % t == 0
    cos, sin = _rope_tables(past + jnp.arange(t, dtype=jnp.int32), max(t, tm))
    q, qi, k, ki, v, c, ga, gc, kb, kib, vt, wit = _proj(h, mix_norm, w_main, b_gate, cos, sin, tm)

    if caches is None:
        tq = _row_tile(t, (512, 256, 128, 64))
        attn = _attn(qi, wit, q, kib.reshape(b, t, IDX_DIM), kb.reshape(b, t, KV_COLS), vt,
                     seq_len=t, past=0, tq=tq, kt=tq)
        state = None
        conv_tail = c.reshape(b, t, CONV_CH)[:, t - (CONV_WIDTH - 1):]
    else:
        cache_k, cache_v, cache_idx_k, state_conv = caches
        n_keys = past + t
        pad_keys = -n_keys % LANES

        def with_cache(cache, new):
            cache = cache.reshape(b, past, -1).astype(_MXU_DTYPE)
            pad = jnp.zeros((b, pad_keys, cache.shape[2]), _MXU_DTYPE)
            return jnp.concatenate([cache, new.reshape(b, t, -1), pad], axis=1)

        vt_all = with_cache(cache_v, vt.reshape(KV_COLS, b, t).transpose(1, 2, 0)).transpose(0, 2, 1)
        side_by_side = next(g for g in (4, 2, 1) if b % g == 0)
        attn = _attn(qi, wit, q, with_cache(cache_idx_k, kib), with_cache(cache_k, kb), vt_all,
                     seq_len=t, past=past, tq=side_by_side * t, kt=n_keys + pad_keys)
        lead = jnp.zeros((b, _HALO - (CONV_WIDTH - 1), CONV_CH), jnp.float32)
        state = jnp.concatenate([lead, state_conv], axis=1)
        conv_tail = jnp.concatenate([state_conv, c.reshape(b, t, CONV_CH)], axis=1)[:, -(CONV_WIDTH - 1):]
    if t != CHUNK:
        assert state is None, "a carried conv state is supported for 64-row sequences only"

    h2 = _merge(h, c, state, attn, ga, gc, merge_w, tm_merge, t)
    y = _ffn(h2, ffn2_norm, ffn2_w, final_norm, tm)
    return (y.reshape(b, t, D_MODEL),
            k.reshape(1, b, t, N_KV_HEADS, HEAD_DIM), v.reshape(1, b, t, N_KV_HEADS, HEAD_DIM),
            ki.reshape(1, b, t, IDX_DIM), conv_tail[None])


def kernel(x_prompt, x_sample, cache_k, cache_v, cache_idx_k, state_conv, ffn1_norm, ffn1_w_in, ffn1_w_out, mix_norm, w_in, b_gate, conv_w, conv_b, conv_ln_g, conv_ln_b, conv_w_out, attn_w_out, w_out, ffn2_norm, ffn2_w_in, ffn2_w_out, final_norm):
    assert ffn1_norm.shape[0] == 1, "one layer"
    assert x_prompt.shape[1] >= CONV_WIDTH - 1
    conv_taps = jnp.zeros((_HALO, CONV_CH), jnp.float32).at[:CONV_WIDTH].set(conv_w[0])
    merge_w = (conv_taps, conv_b[0].reshape(1, CONV_CH), conv_ln_g[0].reshape(1, CONV_CH),
               conv_ln_b[0].reshape(1, CONV_CH), conv_w_out[0].astype(_MXU_DTYPE),
               attn_w_out[0].astype(_MXU_DTYPE), w_out[0].astype(_MXU_DTYPE))
    weights = (ffn1_norm[0], _ffn_weights(ffn1_w_in[0], ffn1_w_out[0]), mix_norm[0],
               _main_weight(w_in[0]), b_gate[0], merge_w,
               ffn2_norm[0], _ffn_weights(ffn2_w_in[0], ffn2_w_out[0]))
    y_p, k_p, v_p, ki_p, conv_p = _layer(x_prompt, None, weights, final_norm)
    y_s, k_s, v_s, ki_s, conv_s = _layer(
        x_sample, (cache_k[0], cache_v[0], cache_idx_k[0], state_conv[0]), weights, final_norm)
    return (y_p, y_s, k_p, v_p, ki_p, conv_p, k_s, v_s, ki_s, conv_s)
```
